```python
import math
import jax, jax.numpy as jnp
from jax import lax
import numpy as np

D_MODEL = 2048
BATCH = 16
SEQ = 256
DEPTH = 2
DEC_BATCH = 8
DEC_SEQ = 1024
PAST_LEN = 512

GRID_W = 64
HEAD_DIM = 128
ATT_WIDTH = D_MODEL // 2
N_ATT_HEADS = ATT_WIDTH // HEAD_DIM
N_KV_HEADS = N_ATT_HEADS // 4
GQA_GROUP = N_ATT_HEADS // N_KV_HEADS
KV_WIDTH = N_KV_HEADS * HEAD_DIM
RET_WIDTH = D_MODEL // 4
N_RET_HEADS = RET_WIDTH // HEAD_DIM
HYENA_WIDTH = D_MODEL // 4
HYENA_ORDER = 2
MIX_WIDTH = ATT_WIDTH + RET_WIDTH + HYENA_WIDTH
SPLIT_SIZES = (ATT_WIDTH, KV_WIDTH, KV_WIDTH, RET_WIDTH, RET_WIDTH, RET_WIDTH, RET_WIDTH,
               (HYENA_ORDER + 1) * HYENA_WIDTH)
IN_WIDTH = ATT_WIDTH + 2 * KV_WIDTH + 4 * RET_WIDTH + (HYENA_ORDER + 1) * HYENA_WIDTH
D_FF = 5632
Q_BLOCK = 128
RET_CHUNK = 128
ROPE_THETA = 10000.0
FILTER_BANDS = 16
FILTER_EMB = 1 + 2 * FILTER_BANDS
FILTER_HID = 64
HYENA_DECAY_TARGET = 1e-2
HYENA_DECAY_PCT_MIN = 0.3
HYENA_DECAY_PCT_MAX = 1.5
DEEPNORM_ALPHA = (2 * DEPTH) ** 0.25
DEEPNORM_BETA = (8 * DEPTH) ** -0.25
EPS = 1e-6

kernel_name = 'hybrid_dit_attn_retention_hyena_step'


def _split_points(sizes):
    pts, acc = [], 0
    for s in sizes[:-1]:
        acc += s
        pts.append(acc)
    return pts


def layer_norm(x, w, b):
    xf = x.astype(jnp.float32)
    xc = xf - jnp.mean(xf, -1, keepdims=True)
    var = jnp.mean(xc * xc, -1, keepdims=True)
    return (xc * lax.rsqrt(var + EPS) * w.astype(jnp.float32) + b.astype(jnp.float32)).astype(x.dtype)


def head_norm(x):
    xf = x.astype(jnp.float32)
    xc = xf - jnp.mean(xf, -1, keepdims=True)
    var = jnp.mean(xc * xc, -1, keepdims=True)
    return (xc * lax.rsqrt(var + EPS)).astype(x.dtype)


def rms_norm(x, w):
    xf = x.astype(jnp.float32)
    return (xf * lax.rsqrt(jnp.mean(xf * xf, -1, keepdims=True) + EPS) * w.astype(jnp.float32)).astype(x.dtype)


def axial_rope_tables(n_tokens):
    rows = n_tokens // GRID_W
    row = jnp.repeat(jnp.arange(rows, dtype=jnp.float32), GRID_W)
    col = jnp.tile(jnp.arange(GRID_W, dtype=jnp.float32), rows)
    n_freq = HEAD_DIM // 4
    inv_freq = ROPE_THETA ** (-jnp.arange(n_freq, dtype=jnp.float32) / n_freq)
    ang = jnp.concatenate([row[:, None] * inv_freq[None], col[:, None] * inv_freq[None]], -1)
    return jnp.cos(ang), jnp.sin(ang)


def apply_rope(x, cos, sin):
    x1, x2 = jnp.split(x, 2, axis=-1)
    c = cos.astype(x.dtype)
    s = sin.astype(x.dtype)
    return jnp.concatenate([x1 * c - x2 * s, x1 * s + x2 * c], -1)


def centred_dwconv3(x, w):
    xp = jnp.pad(x, ((0, 0), (1, 1), (0, 0)))
    return xp[:, :-2] * w[0] + xp[:, 1:-1] * w[1] + xp[:, 2:] * w[2]


def block_attention(q, k, v):
    B, KV, G, Lq, hd = q.shape
    nb = Lq // Q_BLOCK
    qb = q.reshape(B, KV, G, nb, Q_BLOCK, hd).transpose(3, 0, 1, 2, 4, 5)
    scale = hd ** -0.5

    def one_block(qi):
        s = jnp.einsum('bkgqd,bksd->bkgqs', qi, k).astype(jnp.float32) * scale
        p = jax.nn.softmax(s, axis=-1).astype(v.dtype)
        return jnp.einsum('bkgqs,bksd->bkgqd', p, v)

    out = lax.map(one_block, qb)
    return out.transpose(1, 2, 3, 0, 4, 5).reshape(B, KV, G, Lq, hd)


def retention_scan(q, k, v, log_gamma, s0):
    B, H, L, d = q.shape
    C = RET_CHUNK
    n = L // C
    idx = jnp.arange(C, dtype=jnp.float32)
    lg = log_gamma[:, None]
    decay_q = jnp.exp(lg * (idx + 1.0))
    decay_k = jnp.exp(lg * (C - 1.0 - idx))
    diff = idx[:, None] - idx[None, :]
    dmat = jnp.where(diff >= 0, jnp.exp(lg[:, :, None] * jnp.maximum(diff, 0.0)), 0.0)
    chunk_decay = jnp.exp(log_gamma * C)
    qc = q.reshape(B, H, n, C, d)
    kc = k.reshape(B, H, n, C, d)
    vc = v.reshape(B, H, n, C, d)
    inner = jnp.einsum('bhnid,bhnjd->bhnij', qc, kc) * dmat[None, :, None].astype(q.dtype)
    inner_out = jnp.einsum('bhnij,bhnje->bhnie', inner, vc)
    kv = jnp.einsum('bhnjd,bhnje->bhnde', kc * decay_k[None, :, None, :, None].astype(q.dtype), vc)
    kv = kv.astype(jnp.float32).transpose(2, 0, 1, 3, 4)

    def step(s, kv_n):
        return chunk_decay[None, :, None, None] * s + kv_n, s

    s_final, s_prev = lax.scan(step, s0.astype(jnp.float32), kv)
    s_prev = s_prev.transpose(1, 2, 0, 3, 4)
    cross = jnp.einsum('bhnid,bhnde->bhnie', qc.astype(jnp.float32) * decay_q[None, :, None, :, None], s_prev)
    out = (inner_out.astype(jnp.float32) + cross).reshape(B, H, L, d)
    return out.astype(q.dtype), s_final.astype(q.dtype)


def hyena_filters(L, w1, b1, freq, w2, b2, w3):
    pos = jnp.arange(L, dtype=jnp.float32)
    t = pos / max(L - 1, 1)
    w = 2.0 * math.pi * pos / L
    f = jnp.linspace(1e-4, FILTER_BANDS - 1, FILTER_BANDS, dtype=jnp.float32)
    ang = w[:, None] * f[None, :]
    z = jnp.concatenate([t[:, None], jnp.cos(ang), -jnp.sin(ang)], -1).astype(w1.dtype)
    hdn = jnp.sin(freq * (z @ w1 + b1))
    hdn = jnp.sin(freq * (hdn @ w2 + b2))
    h = hdn @ w3
    deltas = jnp.linspace(math.log(HYENA_DECAY_TARGET) / HYENA_DECAY_PCT_MIN,
                          math.log(HYENA_DECAY_TARGET) / HYENA_DECAY_PCT_MAX, HYENA_WIDTH, dtype=jnp.float32)
    decay = jnp.exp(-t[:, None] * jnp.abs(deltas)[None, :]).astype(h.dtype)
    return h.reshape(L, HYENA_ORDER, 2, HYENA_WIDTH) * decay[:, None, None, :]


def centred_long_conv(u, h_fwd, h_bwd):
    L = u.shape[1]
    taps = jnp.concatenate([h_fwd, h_bwd[::-1]], 0)
    U = jnp.fft.rfft(u.astype(jnp.float32), n=2 * L, axis=1)
    K = jnp.fft.rfft(taps.astype(jnp.float32), n=2 * L, axis=0)
    y = jnp.fft.irfft(U * K[None], n=2 * L, axis=1)[:, :L]
    return y.astype(u.dtype)


def mixing(h, lp, rope, ctx_k, ctx_v, ret_init):
    B, L, _ = h.shape
    proj = h @ lp['w_in']
    q, k, v, rq, rk, rv, rg, hy = jnp.split(proj, _split_points(SPLIT_SIZES), axis=-1)

    q = rms_norm(q.reshape(B, L, N_KV_HEADS, GQA_GROUP, HEAD_DIM).transpose(0, 2, 3, 1, 4), lp['q_norm'])
    k = rms_norm(k.reshape(B, L, N_KV_HEADS, HEAD_DIM).transpose(0, 2, 1, 3), lp['k_norm'])
    v = v.reshape(B, L, N_KV_HEADS, HEAD_DIM).transpose(0, 2, 1, 3)
    own_k, own_v = k, v
    if rope is not None:
        cos, sin = rope
        q = apply_rope(q, cos, sin)
        k = apply_rope(k, cos, sin)
    if ctx_k is not None:
        k = jnp.concatenate([ctx_k, k], axis=2)
        v = jnp.concatenate([ctx_v, v], axis=2)
    att = block_attention(q, k, v).transpose(0, 3, 1, 2, 4).reshape(B, L, ATT_WIDTH)

    def heads(t):
        return t.reshape(B, L, N_RET_HEADS, HEAD_DIM).transpose(0, 2, 1, 3)
    rq_, rk_, rv_ = heads(rq), heads(rk) * (HEAD_DIM ** -0.5), heads(rv)
    log_gamma = jax.nn.log_sigmoid(lp['ret_decay'].astype(jnp.float32))
    o_f, s_f = retention_scan(rq_, rk_, rv_, log_gamma[0], ret_init[:, 0])
    o_b, s_b = retention_scan(rq_[:, :, ::-1], rk_[:, :, ::-1], rv_[:, :, ::-1], log_gamma[1], ret_init[:, 1])
    ret = head_norm(o_f + o_b[:, :, ::-1]).transpose(0, 2, 1, 3).reshape(B, L, RET_WIDTH)
    ret = jax.nn.silu(rg) * ret

    hy = centred_dwconv3(hy, lp['hy_conv'])
    parts = jnp.split(hy, HYENA_ORDER + 1, axis=-1)
    z = parts[0]
    filt = hyena_filters(L, lp['hf_w1'], lp['hf_b1'], lp['hf_freq'], lp['hf_w2'], lp['hf_b2'], lp['hf_w3'])
    for o in range(HYENA_ORDER):
        z = parts[o + 1] * (centred_long_conv(z, filt[:, o, 0], filt[:, o, 1]) + z * lp['hy_bias'][o])

    out = jnp.concatenate([att, ret, z], -1) @ lp['w_out']
    return out, (own_k, own_v, jnp.stack([s_f, s_b], axis=1))


def conv_ffn(h, lp):
    up = centred_dwconv3(h @ lp['w_up'], lp['ffn_conv'])
    g, u = jnp.split(up, 2, axis=-1)
    return (jax.nn.silu(g) * u) @ lp['w_down']


def trunk_layer(x, mod, lp, rope, ctx_k, ctx_v, ret_init):
    sh1, sc1, g1, sh2, sc2, g2 = jnp.split(mod, 6, axis=-1)
    mix, ctx_state = mixing(x * (1 + sc1) + sh1, lp, rope, ctx_k, ctx_v, ret_init)
    x = layer_norm(DEEPNORM_ALPHA * x + g1 * mix, lp['ln1_w'], lp['ln1_b'])
    ffn = conv_ffn(x * (1 + sc2) + sh2, lp)
    x = layer_norm(DEEPNORM_ALPHA * x + g2 * ffn, lp['ln2_w'], lp['ln2_b'])
    return x, ctx_state


def setup_inputs(seed: int = 0) -> dict:
    key = jax.random.key(seed)
    ks = iter(jax.random.split(key, 40))

    def nrm(shape, std):
        return std * jax.random.normal(next(ks), shape, jnp.float32)

    D = D_MODEL
    x_prompt = nrm((BATCH, SEQ, D), 1.0)
    x_sample = nrm((DEC_BATCH, DEC_SEQ, D), 1.0)
    cache_k = nrm((DEC_BATCH, DEPTH, N_KV_HEADS, PAST_LEN, HEAD_DIM), 1.0)
    cache_v = nrm((DEC_BATCH, DEPTH, N_KV_HEADS, PAST_LEN, HEAD_DIM), 1.0)
    state_ret = nrm((DEC_BATCH, DEPTH, 2, N_RET_HEADS, HEAD_DIM, HEAD_DIM), 1.0)
    c = nrm((DEC_BATCH, D), 1.0)
    c_ctx = nrm((D,), 1.0)
    w_ada = nrm((DEPTH, D, 6 * D), 0.5 * D ** -0.5)
    b_ada = nrm((DEPTH, 6 * D), 0.01)
    w_in = nrm((DEPTH, D, IN_WIDTH), D ** -0.5)
    q_norm = 1.0 + nrm((DEPTH, HEAD_DIM), 0.02)
    k_norm = 1.0 + nrm((DEPTH, HEAD_DIM), 0.02)
    ret_base = jnp.log(2.0 ** (5.0 + jnp.arange(N_RET_HEADS, dtype=jnp.float32)) - 1.0)
    ret_decay = ret_base[None, None, :] + nrm((DEPTH, 2, N_RET_HEADS), 0.1)
    hy_conv = nrm((DEPTH, 3, (HYENA_ORDER + 1) * HYENA_WIDTH), 0.5)
    hf_w1 = nrm((DEPTH, FILTER_EMB, FILTER_HID), FILTER_EMB ** -0.5)
    hf_b1 = nrm((DEPTH, FILTER_HID), 0.1)
    hf_freq = 1.0 + nrm((DEPTH, FILTER_HID), 0.02)
    hf_w2 = nrm((DEPTH, FILTER_HID, FILTER_HID), FILTER_HID ** -0.5)
    hf_b2 = nrm((DEPTH, FILTER_HID), 0.1)
    hf_w3 = nrm((DEPTH, FILTER_HID, HYENA_ORDER * 2 * HYENA_WIDTH), 0.1 * FILTER_HID ** -0.5)
    hy_bias = nrm((DEPTH, HYENA_ORDER, HYENA_WIDTH), 0.5)
    w_out = nrm((DEPTH, MIX_WIDTH, D), DEEPNORM_BETA * MIX_WIDTH ** -0.5)
    ln1_w = 1.0 + nrm((DEPTH, D), 0.02)
    ln1_b = nrm((DEPTH, D), 0.01)
    w_up = nrm((DEPTH, D, 2 * D_FF), D ** -0.5)
    ffn_conv = nrm((DEPTH, 3, 2 * D_FF), 0.5)
    w_down = nrm((DEPTH, D_FF, D), DEEPNORM_BETA * D_FF ** -0.5)
    ln2_w = 1.0 + nrm((DEPTH, D), 0.02)
    ln2_b = nrm((DEPTH, D), 0.01)
    return {'x_prompt': x_prompt, 'x_sample': x_sample, 'cache_k': cache_k, 'cache_v': cache_v,
            'state_ret': state_ret, 'c': c, 'c_ctx': c_ctx, 'w_ada': w_ada, 'b_ada': b_ada,
            'w_in': w_in, 'q_norm': q_norm, 'k_norm': k_norm, 'ret_decay': ret_decay,
            'hy_conv': hy_conv, 'hf_w1': hf_w1, 'hf_b1': hf_b1, 'hf_freq': hf_freq, 'hf_w2': hf_w2,
            'hf_b2': hf_b2, 'hf_w3': hf_w3, 'hy_bias': hy_bias, 'w_out': w_out, 'ln1_w': ln1_w,
            'ln1_b': ln1_b, 'w_up': w_up, 'ffn_conv': ffn_conv, 'w_down': w_down,
            'ln2_w': ln2_w, 'ln2_b': ln2_b}


def reference(x_prompt, x_sample, cache_k, cache_v, state_ret, c, c_ctx, w_ada, b_ada, w_in,
              q_norm, k_norm, ret_decay, hy_conv, hf_w1, hf_b1, hf_freq, hf_w2, hf_b2, hf_w3,
              hy_bias, w_out, ln1_w, ln1_b, w_up, ffn_conv, w_down, ln2_w, ln2_b):
    rope = axial_rope_tables(x_sample.shape[1])
    ret_zero = jnp.zeros((x_prompt.shape[0], 2, N_RET_HEADS, HEAD_DIM, HEAD_DIM), x_prompt.dtype)
    y_p, y_s = x_prompt, x_sample
    ks_out, vs_out, ss_out = [], [], []
    for l in range(DEPTH):
        lp = {'w_in': w_in[l], 'q_norm': q_norm[l], 'k_norm': k_norm[l], 'ret_decay': ret_decay[l],
              'hy_conv': hy_conv[l], 'hf_w1': hf_w1[l], 'hf_b1': hf_b1[l], 'hf_freq': hf_freq[l],
              'hf_w2': hf_w2[l], 'hf_b2': hf_b2[l], 'hf_w3': hf_w3[l], 'hy_bias': hy_bias[l],
              'w_out': w_out[l], 'ln1_w': ln1_w[l], 'ln1_b': ln1_b[l], 'w_up': w_up[l],
              'ffn_conv': ffn_conv[l], 'w_down': w_down[l], 'ln2_w': ln2_w[l], 'ln2_b': ln2_b[l]}
        mod_ctx = (jax.nn.silu(c_ctx) @ w_ada[l] + b_ada[l])[None, None, :]
        mod_lat = (jax.nn.silu(c) @ w_ada[l] + b_ada[l])[:, None, :]
        y_p, (k_l, v_l, s_l) = trunk_layer(y_p, mod_ctx, lp, None, None, None, ret_zero)
        ks_out.append(k_l)
        vs_out.append(v_l)
        ss_out.append(s_l)
        y_s, _ = trunk_layer(y_s, mod_lat, lp, rope, cache_k[:, l], cache_v[:, l], state_ret[:, l])
    new_cache_k = jnp.stack(ks_out, axis=1)
    new_cache_v = jnp.stack(vs_out, axis=1)
    new_state_ret = jnp.stack(ss_out, axis=1)
    return (y_p, y_s, new_cache_k, new_cache_v, new_state_ret)
```

```python
import functools
import math

import jax
import jax.numpy as jnp
from jax import lax
from jax.experimental import pallas as pl
from jax.experimental.pallas import tpu as pltpu

F32 = jnp.float32
BF16 = jnp.bfloat16

D_MODEL = 2048
DEPTH = 2
GRID_W = 64
HEAD_DIM = 128
ATT_WIDTH = D_MODEL // 2
N_ATT_HEADS = ATT_WIDTH // HEAD_DIM
N_KV_HEADS = N_ATT_HEADS // 4
GQA_GROUP = N_ATT_HEADS // N_KV_HEADS
KV_WIDTH = N_KV_HEADS * HEAD_DIM
RET_WIDTH = D_MODEL // 4
N_RET_HEADS = RET_WIDTH // HEAD_DIM
HYENA_WIDTH = D_MODEL // 4
HYENA_ORDER = 2
IN_WIDTH = ATT_WIDTH + 2 * KV_WIDTH + 4 * RET_WIDTH + (HYENA_ORDER + 1) * HYENA_WIDTH
D_FF = 5632
ROPE_THETA = 10000.0
FILTER_BANDS = 16
FILTER_EMB = 1 + 2 * FILTER_BANDS
FILTER_HID = 64
HYENA_DECAY_TARGET = 1e-2
HYENA_DECAY_PCT_MIN = 0.3
HYENA_DECAY_PCT_MAX = 1.5
DEEPNORM_ALPHA = (2 * DEPTH) ** 0.25
EPS = 1e-6

COL_Q = 0
COL_K = ATT_WIDTH
COL_V = COL_K + KV_WIDTH
COL_RQ = COL_V + KV_WIDTH
COL_RK = COL_RQ + RET_WIDTH
COL_RV = COL_RK + RET_WIDTH
COL_RG = COL_RV + RET_WIDTH
COL_HY = COL_RG + RET_WIDTH

LANES = 128
HALO = 16
VMEM_LIMIT = 56 * 1024 * 1024
COND_ROWS = 16

ADA_TN = 1024
INPROJ_TM, INPROJ_TN = 512, 512
OUTPROJ_TM = 256
FFN_TM, FFN_TF = 512, 512
ATT_TQ = 256
HYENA_TW = 256
FILTER_PAD = 128

NT_DIMS = (((1,), (1,)), ((), ()))
TN_DIMS = (((0,), (0,)), ((), ()))


def _params(n_grid):
    return pltpu.CompilerParams(dimension_semantics=("arbitrary",) * n_grid,
                                vmem_limit_bytes=VMEM_LIMIT)


def _dot(a, b):
    return jnp.dot(a, b, preferred_element_type=F32)


def _split(a):
    hi = a.astype(BF16)
    lo = (a - hi.astype(F32)).astype(BF16)
    return hi, lo


def _dot3(a_hi, a_lo, b_hi, b_lo):
    return _dot(a_hi, b_hi) + _dot(a_lo, b_hi) + _dot(a_hi, b_lo)


def _silu(x):
    return x * jax.nn.sigmoid(x)


def _layer_norm(y, w, b):
    yc = y - jnp.mean(y, -1, keepdims=True)
    var = jnp.mean(yc * yc, -1, keepdims=True)
    return yc * lax.rsqrt(var + EPS) * w + b


def _shift_rows(x, down):
    n = x.shape[0]
    return pltpu.roll(x, 1 if down else n - 1, axis=0)


def _ada_kernel(cond_ref, w_ref, b_ref, o_ref):
    a = _silu(cond_ref[...]).astype(BF16)
    o_ref[...] = _dot(a, w_ref[...].astype(BF16)) + b_ref[...]


def _ada(cond, w_ada, b_ada):
    n_out = w_ada.shape[-1]
    return pl.pallas_call(
        _ada_kernel,
        grid=(DEPTH, n_out // ADA_TN),
        in_specs=[
            pl.BlockSpec((COND_ROWS, D_MODEL), lambda l, j: (0, 0)),
            pl.BlockSpec((None, D_MODEL, ADA_TN), lambda l, j: (l, 0, j)),
            pl.BlockSpec((None, 1, ADA_TN), lambda l, j: (l, 0, j)),
        ],
        out_specs=pl.BlockSpec((None, COND_ROWS, ADA_TN), lambda l, j: (l, 0, j)),
        out_shape=jax.ShapeDtypeStruct((DEPTH, COND_ROWS, n_out), F32),
        compiler_params=_params(2),
        name="ada",
    )(cond, w_ada, b_ada.reshape(DEPTH, 1, n_out))


def _inproj_kernel(x_ref, sh_ref, sc_ref, w_ref, o_ref, h_scr):
    @pl.when(pl.program_id(1) == 0)
    def _():
        h_scr[...] = (x_ref[...] * (1.0 + sc_ref[...]) + sh_ref[...]).astype(BF16)

    o_ref[...] = _dot(h_scr[...], w_ref[...])


def _inproj(x, mod, w_in):
    rows = x.shape[0]
    tiles_per_mod = rows // mod.shape[0] // INPROJ_TM
    return pl.pallas_call(
        _inproj_kernel,
        grid=(rows // INPROJ_TM, IN_WIDTH // INPROJ_TN),
        in_specs=[
            pl.BlockSpec((INPROJ_TM, D_MODEL), lambda i, j: (i, 0)),
            pl.BlockSpec((None, 1, D_MODEL), lambda i, j: (i // tiles_per_mod, 0, 0)),
            pl.BlockSpec((None, 1, D_MODEL), lambda i, j: (i // tiles_per_mod, 0, 1)),
            pl.BlockSpec((D_MODEL, INPROJ_TN), lambda i, j: (0, j)),
        ],
        out_specs=pl.BlockSpec((INPROJ_TM, INPROJ_TN), lambda i, j: (i, j)),
        out_shape=jax.ShapeDtypeStruct((rows, IN_WIDTH), F32),
        scratch_shapes=[pltpu.VMEM((INPROJ_TM, D_MODEL), BF16)],
        compiler_params=_params(2),
        name="inproj",
    )(x, mod, mod, w_in)


def _rms_norm(x, w):
    return x * lax.rsqrt(jnp.mean(x * x, -1, keepdims=True) + EPS) * w


def _rope(x, cos_full, sin_signed):
    return x * cos_full + pltpu.roll(x, HEAD_DIM // 2, axis=1) * sin_signed


def _attn_kernel(*refs, tq, rope, has_ctx, emit_cache):
    refs = list(refs)
    q_ref, k_ref, v_ref, qw_ref, kw_ref = refs[:5]
    del refs[:5]
    if rope:
        qcos_ref, qsin_ref, kcos_ref, ksin_ref = refs[:4]
        del refs[:4]
    if has_ctx:
        ck_ref, cv_ref = refs[:2]
        del refs[:2]
    o_ref = refs.pop(0)
    if emit_cache:
        ownk_ref, ownv_ref = refs[:2]
        del refs[:2]
    k_scr, v_scr = refs

    @pl.when(pl.program_id(2) == 0)
    def _():
        kn = _rms_norm(k_ref[...], kw_ref[...])
        if emit_cache:
            ownk_ref[...] = kn
            ownv_ref[...] = v_ref[...]
        if rope:
            kn = _rope(kn, kcos_ref[...], ksin_ref[...])
        k_scr[...] = kn.astype(BF16)
        v_scr[...] = v_ref[...].astype(BF16)

    heads = []
    for g in range(GQA_GROUP):
        qn = _rms_norm(q_ref[:, g * HEAD_DIM:(g + 1) * HEAD_DIM], qw_ref[...])
        if rope:
            qn = _rope(qn, qcos_ref[...], qsin_ref[...])
        heads.append(qn.astype(BF16))
    qs = jnp.concatenate(heads, axis=0)

    scale = HEAD_DIM ** -0.5
    s = lax.dot_general(qs, k_scr[...], NT_DIMS, preferred_element_type=F32) * scale
    m = jnp.max(s, -1, keepdims=True)
    if has_ctx:
        sc = lax.dot_general(qs, ck_ref[...].astype(BF16), NT_DIMS, preferred_element_type=F32) * scale
        m = jnp.maximum(m, jnp.max(sc, -1, keepdims=True))
    p = jnp.exp(s - m)
    den = jnp.sum(p, -1, keepdims=True)
    o = _dot(p.astype(BF16), v_scr[...])
    if has_ctx:
        pc = jnp.exp(sc - m)
        den = den + jnp.sum(pc, -1, keepdims=True)
        o = o + _dot(pc.astype(BF16), cv_ref[...].astype(BF16))
    o = o / den
    for g in range(GQA_GROUP):
        o_ref[:, g * HEAD_DIM:(g + 1) * HEAD_DIM] = o[g * tq:(g + 1) * tq]


def _attention(proj, n_batch, seq, q_norm, k_norm, rope_tabs, ctx_kv, layer, emit_cache):
    rows = proj.shape[0]
    tq = min(ATT_TQ, seq)
    nq = seq // tq
    group_w = GQA_GROUP * HEAD_DIM
    rope = rope_tabs is not None
    has_ctx = ctx_kv is not None

    in_specs = [
        pl.BlockSpec((tq, group_w), lambda b, kv, qi: (b * nq + qi, COL_Q // group_w + kv)),
        pl.BlockSpec((seq, HEAD_DIM), lambda b, kv, qi: (b, COL_K // HEAD_DIM + kv)),
        pl.BlockSpec((seq, HEAD_DIM), lambda b, kv, qi: (b, COL_V // HEAD_DIM + kv)),
        pl.BlockSpec((1, HEAD_DIM), lambda b, kv, qi: (0, 0)),
        pl.BlockSpec((1, HEAD_DIM), lambda b, kv, qi: (0, 0)),
    ]
    args = [proj, proj, proj, q_norm.reshape(1, HEAD_DIM), k_norm.reshape(1, HEAD_DIM)]
    if rope:
        cos_full, sin_signed = rope_tabs
        in_specs += [
            pl.BlockSpec((tq, HEAD_DIM), lambda b, kv, qi: (qi, 0)),
            pl.BlockSpec((tq, HEAD_DIM), lambda b, kv, qi: (qi, 0)),
            pl.BlockSpec((seq, HEAD_DIM), lambda b, kv, qi: (0, 0)),
            pl.BlockSpec((seq, HEAD_DIM), lambda b, kv, qi: (0, 0)),
        ]
        args += [cos_full, sin_signed, cos_full, sin_signed]
    if has_ctx:
        ck, cv = ctx_kv
        past = ck.shape[3]
        ctx_spec = pl.BlockSpec((None, None, None, past, HEAD_DIM), lambda b, kv, qi: (b, layer, kv, 0, 0))
        in_specs += [ctx_spec, ctx_spec]
        args += [ck, cv]

    out_specs = [pl.BlockSpec((tq, group_w), lambda b, kv, qi: (b * nq + qi, kv))]
    out_shape = [jax.ShapeDtypeStruct((rows, ATT_WIDTH), F32)]
    if emit_cache:
        own_spec = pl.BlockSpec((None, None, seq, HEAD_DIM), lambda b, kv, qi: (b, kv, 0, 0))
        out_specs += [own_spec, own_spec]
        out_shape += [jax.ShapeDtypeStruct((n_batch, N_KV_HEADS, seq, HEAD_DIM), F32)] * 2

    return pl.pallas_call(
        functools.partial(_attn_kernel, tq=tq, rope=rope, has_ctx=has_ctx, emit_cache=emit_cache),
        grid=(n_batch, N_KV_HEADS, nq),
        in_specs=in_specs,
        out_specs=out_specs,
        out_shape=out_shape,
        scratch_shapes=[pltpu.VMEM((seq, HEAD_DIM), BF16), pltpu.VMEM((seq, HEAD_DIM), BF16)],
        compiler_params=_params(3),
        name="attention",
    )(*args)


def _retention_kernel(lg_ref, q_ref, k_ref, v_ref, g_ref, s0_ref, o_ref, sf_ref, *, seq):
    h = pl.program_id(1)
    lg_f = lg_ref[h]
    lg_b = lg_ref[N_RET_HEADS + h]

    q = q_ref[...]
    k = k_ref[...] * (HEAD_DIM ** -0.5)
    vb = v_ref[...].astype(BF16)
    a = lax.dot_general(q.astype(BF16), k.astype(BF16), NT_DIMS, preferred_element_type=F32)

    i = lax.broadcasted_iota(jnp.int32, (seq, seq), 0)
    j = lax.broadcasted_iota(jnp.int32, (seq, seq), 1)
    d = (i - j).astype(F32)
    decay = jnp.exp(jnp.where(d >= 0, lg_f, lg_b) * jnp.abs(d))
    decay = jnp.where(d == 0, 2.0, decay)
    o = _dot((a * decay).astype(BF16), vb)

    pos = lax.broadcasted_iota(jnp.int32, (seq, 1), 0).astype(F32)
    s0_f = s0_ref[0]
    s0_b = s0_ref[1]
    o = o + _dot((q * jnp.exp(lg_f * (pos + 1.0))).astype(BF16), s0_f.astype(BF16))
    o = o + _dot((q * jnp.exp(lg_b * (seq - pos))).astype(BF16), s0_b.astype(BF16))

    kf = (k * jnp.exp(lg_f * (seq - 1.0 - pos))).astype(BF16)
    kb = (k * jnp.exp(lg_b * pos)).astype(BF16)
    ones = jnp.ones((1, HEAD_DIM), F32)
    sf_ref[0] = jnp.exp(ones * (lg_f * seq)) * s0_f + lax.dot_general(kf, vb, TN_DIMS, preferred_element_type=F32)
    sf_ref[1] = jnp.exp(ones * (lg_b * seq)) * s0_b + lax.dot_general(kb, vb, TN_DIMS, preferred_element_type=F32)

    oc = o - jnp.mean(o, -1, keepdims=True)
    var = jnp.mean(oc * oc, -1, keepdims=True)
    o_ref[...] = _silu(g_ref[...]) * (oc * lax.rsqrt(var + EPS))


def _retention(proj, n_batch, seq, log_gamma, s0, layer):
    rows = proj.shape[0]

    def col(offset):
        return pl.BlockSpec((seq, HEAD_DIM), lambda b, h: (b, offset // HEAD_DIM + h))

    return pl.pallas_call(
        functools.partial(_retention_kernel, seq=seq),
        grid=(n_batch, N_RET_HEADS),
        in_specs=[
            pl.BlockSpec(memory_space=pltpu.SMEM),
            col(COL_RQ), col(COL_RK), col(COL_RV), col(COL_RG),
            pl.BlockSpec((None, None, 2, None, HEAD_DIM, HEAD_DIM), lambda b, h: (b, layer, 0, h, 0, 0)),
        ],
        out_specs=[
            pl.BlockSpec((seq, HEAD_DIM), lambda b, h: (b, h)),
            pl.BlockSpec((None, 2, None, HEAD_DIM, HEAD_DIM), lambda b, h: (b, 0, h, 0, 0)),
        ],
        out_shape=[
            jax.ShapeDtypeStruct((rows, RET_WIDTH), F32),
            jax.ShapeDtypeStruct((n_batch, 2, N_RET_HEADS, HEAD_DIM, HEAD_DIM), F32),
        ],
        compiler_params=_params(2),
        name="retention",
    )(log_gamma.reshape(2 * N_RET_HEADS), proj, proj, proj, proj, s0)


def _alt_sign(n):
    r = lax.broadcasted_iota(jnp.int32, (n, 1), 0)
    return jnp.where((r & 1) == 0, 1.0, -1.0).astype(F32)


def _filter_kernel(z_ref, w1_ref, b1_ref, fr_ref, w2_ref, b2_ref, w3f_ref, w3b_ref, dl_ref,
                   ch_ref, cl_ref, sh_ref, sl_ref, kr_ref, ki_ref, kn_ref, *, seq):
    freq = fr_ref[...]
    hdn = jnp.sin(freq * (_dot3(*_split(z_ref[...]), *_split(w1_ref[...])) + b1_ref[...]))
    hdn = jnp.sin(freq * (_dot3(*_split(hdn), *_split(w2_ref[...])) + b2_ref[...]))
    hdn_hi, hdn_lo = _split(hdn)

    pos = lax.broadcasted_iota(jnp.int32, (seq, 1), 0)
    t = pos.astype(F32) / max(seq - 1, 1)
    decay = jnp.exp(-t * dl_ref[...])
    h_fwd = _dot3(hdn_hi, hdn_lo, *_split(w3f_ref[...])) * decay
    h_bwd = _dot3(hdn_hi, hdn_lo, *_split(w3b_ref[...])) * decay

    h_bwd_shift = jnp.where(pos == 0, 0.0, _shift_rows(h_bwd, down=True))
    alt = _alt_sign(seq)
    kr = _dot3(ch_ref[...], cl_ref[...], *_split(h_fwd + h_bwd_shift)) + alt * h_bwd[seq - 1:seq, :]
    ki = _dot3(sh_ref[...], sl_ref[...], *_split(h_bwd_shift - h_fwd))
    wgt = jnp.where(pos == 0, 1.0, 2.0) * (0.5 / seq)
    kr_ref[...] = kr * wgt
    ki_ref[...] = ki * wgt
    kn_ref[...] = jnp.sum(alt * (h_fwd - h_bwd), axis=0, keepdims=True) * (0.5 / seq)


def _hyena_filters(seq, zfeat, w1, b1, freq, w2, b2, w3, abs_deltas, dft):
    tw = HYENA_TW
    n_w = HYENA_WIDTH // tw
    const = lambda o, wj: (0, 0)
    mat = pl.BlockSpec((seq, seq), const)
    vec = pl.BlockSpec((1, FILTER_PAD), const)
    return pl.pallas_call(
        functools.partial(_filter_kernel, seq=seq),
        grid=(HYENA_ORDER, n_w),
        in_specs=[
            pl.BlockSpec((seq, FILTER_PAD), const),
            pl.BlockSpec((FILTER_PAD, FILTER_PAD), const), vec, vec,
            pl.BlockSpec((FILTER_PAD, FILTER_PAD), const), vec,
            pl.BlockSpec((FILTER_PAD, tw), lambda o, wj: (0, o * 2 * n_w + wj)),
            pl.BlockSpec((FILTER_PAD, tw), lambda o, wj: (0, o * 2 * n_w + n_w + wj)),
            pl.BlockSpec((1, tw), lambda o, wj: (0, wj)),
            mat, mat, mat, mat,
        ],
        out_specs=[
            pl.BlockSpec((None, seq, tw), lambda o, wj: (o, 0, wj)),
            pl.BlockSpec((None, seq, tw), lambda o, wj: (o, 0, wj)),
            pl.BlockSpec((None, 1, tw), lambda o, wj: (o, 0, wj)),
        ],
        out_shape=[
            jax.ShapeDtypeStruct((HYENA_ORDER, seq, HYENA_WIDTH), F32),
            jax.ShapeDtypeStruct((HYENA_ORDER, seq, HYENA_WIDTH), F32),
            jax.ShapeDtypeStruct((HYENA_ORDER, 1, HYENA_WIDTH), F32),
        ],
        compiler_params=_params(2),
        name="hyena_filters",
    )(zfeat, w1, b1, freq, w2, b2, w3, w3, abs_deltas, *dft)


def _dwconv3_seq(x, w, pos, seq):
    prev = jnp.where(pos == 0, 0.0, _shift_rows(x, down=True))
    nxt = jnp.where(pos == seq - 1, 0.0, _shift_rows(x, down=False))
    return prev * w[0:1] + x * w[1:2] + nxt * w[2:3]


def _hyena_kernel(v_ref, x1_ref, x2_ref, cv_ref, c1_ref, c2_ref, bias_ref, kr_ref, ki_ref, kn_ref,
                  ch_ref, cl_ref, sh_ref, sl_ref, o_ref, *, seq):
    pos = lax.broadcasted_iota(jnp.int32, (seq, 1), 0)
    alt = _alt_sign(seq)
    z = _dwconv3_seq(v_ref[...], cv_ref[...], pos, seq)
    gates = (_dwconv3_seq(x1_ref[...], c1_ref[...], pos, seq),
             _dwconv3_seq(x2_ref[...], c2_ref[...], pos, seq))
    c_hi, c_lo, s_hi, s_lo = ch_ref[...], cl_ref[...], sh_ref[...], sl_ref[...]
    for o in range(HYENA_ORDER):
        z_hi, z_lo = _split(z)
        ur = _dot3(c_hi, c_lo, z_hi, z_lo)
        ui = _dot3(s_hi, s_lo, z_hi, z_lo)
        u_nyq = jnp.sum(alt * z, axis=0, keepdims=True)
        kr, ki = kr_ref[o], ki_ref[o]
        yr = ur * kr + ui * ki
        yi = ui * kr - ur * ki
        y = (_dot3(c_hi, c_lo, *_split(yr)) + _dot3(s_hi, s_lo, *_split(yi))
             + alt * (u_nyq * kn_ref[o]))
        z = gates[o] * (y + z * bias_ref[o:o + 1, :])
    o_ref[...] = z


def _hyena(proj, n_batch, seq, hy_conv, hy_bias, spectra, dft):
    rows = proj.shape[0]
    tw = HYENA_TW
    n_w = HYENA_WIDTH // tw
    kr, ki, kn = spectra

    def part(p):
        return pl.BlockSpec((seq, tw), lambda b, wj: (b, (COL_HY + p * HYENA_WIDTH) // tw + wj))

    def conv(p):
        return pl.BlockSpec((3, tw), lambda b, wj: (0, p * n_w + wj))

    mat = pl.BlockSpec((seq, seq), lambda b, wj: (0, 0))
    return pl.pallas_call(
        functools.partial(_hyena_kernel, seq=seq),
        grid=(n_batch, n_w),
        in_specs=[
            part(0), part(1), part(2), conv(0), conv(1), conv(2),
            pl.BlockSpec((HYENA_ORDER, tw), lambda b, wj: (0, wj)),
            pl.BlockSpec((HYENA_ORDER, seq, tw), lambda b, wj: (0, 0, wj)),
            pl.BlockSpec((HYENA_ORDER, seq, tw), lambda b, wj: (0, 0, wj)),
            pl.BlockSpec((HYENA_ORDER, 1, tw), lambda b, wj: (0, 0, wj)),
            mat, mat, mat, mat,
        ],
        out_specs=pl.BlockSpec((seq, tw), lambda b, wj: (b, wj)),
        out_shape=jax.ShapeDtypeStruct((rows, HYENA_WIDTH), F32),
        compiler_params=_params(2),
        name="hyena",
    )(proj, proj, proj, hy_conv, hy_conv, hy_conv, hy_bias, kr, ki, kn, *dft)


def _outproj_kernel(att_ref, ret_ref, hy_ref, x_ref, g_ref, w_ref, lw_ref, lb_ref, o_ref):
    r0, r1 = ATT_WIDTH, ATT_WIDTH + RET_WIDTH
    mix = (_dot(att_ref[...].astype(BF16), w_ref[0:r0, :])
           + _dot(ret_ref[...].astype(BF16), w_ref[r0:r1, :])
           + _dot(hy_ref[...].astype(BF16), w_ref[r1:, :]))
    y = DEEPNORM_ALPHA * x_ref[...] + g_ref[...] * mix
    o_ref[...] = _layer_norm(y, lw_ref[...], lb_ref[...])


def _outproj(att, ret, hyz, x, mod, w_out, ln_w, ln_b):
    rows = x.shape[0]
    tm = OUTPROJ_TM
    tiles_per_mod = rows // mod.shape[0] // tm
    row = lambda width: pl.BlockSpec((tm, width), lambda i: (i, 0))
    vec = pl.BlockSpec((1, D_MODEL), lambda i: (0, 0))
    return pl.pallas_call(
        _outproj_kernel,
        grid=(rows // tm,),
        in_specs=[
            row(ATT_WIDTH), row(RET_WIDTH), row(HYENA_WIDTH), row(D_MODEL),
            pl.BlockSpec((None, 1, D_MODEL), lambda i: (i // tiles_per_mod, 0, 2)),
            pl.BlockSpec((D_MODEL, D_MODEL), lambda i: (0, 0)),
            vec, vec,
        ],
        out_specs=row(D_MODEL),
        out_shape=jax.ShapeDtypeStruct((rows, D_MODEL), F32),
        compiler_params=_params(1),
        name="outproj_ln",
    )(att, ret, hyz, x, mod, w_out, ln_w.reshape(1, D_MODEL), ln_b.reshape(1, D_MODEL))


def _ffn_kernel(x_ref, xp_ref, xn_ref, sh_ref, sc_ref, g_ref, wg_ref, wu_ref, cg_ref, cu_ref, wd_ref,
                lw_ref, lb_ref, o_ref, h_scr, acc_scr, *, seq, tm):
    i = pl.program_id(0)
    j = pl.program_id(1)

    @pl.when(j == 0)
    def _():
        sc = 1.0 + sc_ref[...]
        sh = sh_ref[...]
        h_scr[0:HALO, :] = (xp_ref[...] * sc + sh).astype(BF16)
        h_scr[HALO:HALO + tm, :] = (x_ref[...] * sc + sh).astype(BF16)
        h_scr[HALO + tm:, :] = (xn_ref[...] * sc + sh).astype(BF16)
        acc_scr[...] = jnp.zeros_like(acc_scr)

    pos = (i * tm + lax.broadcasted_iota(jnp.int32, (tm, 1), 0)) & (seq - 1)
    h = h_scr[...]

    def up_conv(w_ref, c_ref):
        up = _dot(h, w_ref[...])
        prev = _shift_rows(up, down=True)[HALO:HALO + tm]
        nxt = _shift_rows(up, down=False)[HALO:HALO + tm]
        c = c_ref[...]
        return (jnp.where(pos == 0, 0.0, prev) * c[0:1] + up[HALO:HALO + tm] * c[1:2]
                + jnp.where(pos == seq - 1, 0.0, nxt) * c[2:3])

    act = _silu(up_conv(wg_ref, cg_ref)) * up_conv(wu_ref, cu_ref)
    acc_scr[...] += _dot(act.astype(BF16), wd_ref[...])

    @pl.when(j == pl.num_programs(1) - 1)
    def _():
        y = DEEPNORM_ALPHA * x_ref[...] + g_ref[...] * acc_scr[...]
        o_ref[...] = _layer_norm(y, lw_ref[...], lb_ref[...])


def _ffn(x, seq, mod, w_up, ffn_conv, w_down, ln_w, ln_b):
    rows = x.shape[0]
    tm, tf = FFN_TM, FFN_TF
    n_f = D_FF // tf
    tiles_per_mod = rows // mod.shape[0] // tm
    halo_per_tile = tm // HALO
    last_halo = rows // HALO - 1

    def mod_spec(chunk):
        return pl.BlockSpec((None, 1, D_MODEL), lambda i, j: (i // tiles_per_mod, 0, chunk))

    vec = pl.BlockSpec((1, D_MODEL), lambda i, j: (0, 0))
    return pl.pallas_call(
        functools.partial(_ffn_kernel, seq=seq, tm=tm),
        grid=(rows // tm, n_f),
        in_specs=[
            pl.BlockSpec((tm, D_MODEL), lambda i, j: (i, 0)),
            pl.BlockSpec((HALO, D_MODEL), lambda i, j: (jnp.maximum(i * halo_per_tile - 1, 0), 0)),
            pl.BlockSpec((HALO, D_MODEL), lambda i, j: (jnp.minimum((i + 1) * halo_per_tile, last_halo), 0)),
            mod_spec(3), mod_spec(4), mod_spec(5),
            pl.BlockSpec((D_MODEL, tf), lambda i, j: (0, j)),
            pl.BlockSpec((D_MODEL, tf), lambda i, j: (0, n_f + j)),
            pl.BlockSpec((3, tf), lambda i, j: (0, j)),
            pl.BlockSpec((3, tf), lambda i, j: (0, n_f + j)),
            pl.BlockSpec((tf, D_MODEL), lambda i, j: (j, 0)),
            vec, vec,
        ],
        out_specs=pl.BlockSpec((tm, D_MODEL), lambda i, j: (i, 0)),
        out_shape=jax.ShapeDtypeStruct((rows, D_MODEL), F32),
        scratch_shapes=[pltpu.VMEM((tm + 2 * HALO, D_MODEL), BF16), pltpu.VMEM((tm, D_MODEL), F32)],
        compiler_params=_params(2),
        name="ffn_ln",
    )(x, x, x, mod, mod, mod, w_up, w_up, ffn_conv, ffn_conv, w_down,
      ln_w.reshape(1, D_MODEL), ln_b.reshape(1, D_MODEL))


def _rope_tables(n_tokens):
    rows = n_tokens // GRID_W
    row = jnp.repeat(jnp.arange(rows, dtype=F32), GRID_W)
    col = jnp.tile(jnp.arange(GRID_W, dtype=F32), rows)
    n_freq = HEAD_DIM // 4
    inv_freq = ROPE_THETA ** (-jnp.arange(n_freq, dtype=F32) / n_freq)
    ang = jnp.concatenate([row[:, None] * inv_freq[None], col[:, None] * inv_freq[None]], -1)
    cos, sin = jnp.cos(ang), jnp.sin(ang)
    return jnp.concatenate([cos, cos], -1), jnp.concatenate([-sin, sin], -1)


def _dft_tables(seq):
    idx = jnp.arange(seq, dtype=jnp.int32)
    kn = (idx[:, None] * idx[None, :]) % (2 * seq)
    ang = kn.astype(F32) * (math.pi / seq)
    return (*_split(jnp.cos(ang)), *_split(jnp.sin(ang)))


def _filter_features(seq):
    pos = jnp.arange(seq, dtype=F32)
    t = pos / max(seq - 1, 1)
    w = 2.0 * math.pi * pos / seq
    f = jnp.linspace(1e-4, FILTER_BANDS - 1, FILTER_BANDS, dtype=F32)
    ang = w[:, None] * f[None, :]
    z = jnp.concatenate([t[:, None], jnp.cos(ang), -jnp.sin(ang)], -1)
    return jnp.pad(z, ((0, 0), (0, FILTER_PAD - FILTER_EMB)))


def _pad_to(a, shape):
    return jnp.pad(a, [(0, s - d) for d, s in zip(a.shape, shape)])


def kernel(x_prompt, x_sample, cache_k, cache_v, state_ret, c, c_ctx, w_ada, b_ada, w_in, q_norm, k_norm, ret_decay, hy_conv, hf_w1, hf_b1, hf_freq, hf_w2, hf_b2, hf_w3, hy_bias, w_out, ln1_w, ln1_b, w_up, ffn_conv, w_down, ln2_w, ln2_b):
    n_ctx, seq_ctx, _ = x_prompt.shape
    n_lat, seq_lat, _ = x_sample.shape

    cond = _pad_to(jnp.concatenate([c_ctx[None, :], c], 0), (COND_ROWS, D_MODEL))
    mod_all = _ada(cond, w_ada, b_ada)

    w_in_b, w_out_b = w_in.astype(BF16), w_out.astype(BF16)
    w_up_b, w_down_b = w_up.astype(BF16), w_down.astype(BF16)
    log_gamma = jax.nn.log_sigmoid(ret_decay.astype(F32))
    abs_deltas = jnp.abs(jnp.linspace(math.log(HYENA_DECAY_TARGET) / HYENA_DECAY_PCT_MIN,
                                      math.log(HYENA_DECAY_TARGET) / HYENA_DECAY_PCT_MAX,
                                      HYENA_WIDTH, dtype=F32))[None, :]
    ret_zero = jnp.zeros((n_ctx, 1, 2, N_RET_HEADS, HEAD_DIM, HEAD_DIM), F32)

    groups = {
        "ctx": dict(n=n_ctx, seq=seq_ctx, rope=None, dft=_dft_tables(seq_ctx), feat=_filter_features(seq_ctx)),
        "lat": dict(n=n_lat, seq=seq_lat, rope=_rope_tables(seq_lat), dft=_dft_tables(seq_lat),
                    feat=_filter_features(seq_lat)),
    }
    xs = {"ctx": x_prompt.reshape(n_ctx * seq_ctx, D_MODEL), "lat": x_sample.reshape(n_lat * seq_lat, D_MODEL)}
    ks_out, vs_out, ss_out = [], [], []

    for l in range(DEPTH):
        mods = {"ctx": mod_all[l, 0:1, None, :], "lat": mod_all[l, 1:1 + n_lat, None, :]}
        pad2 = (FILTER_PAD, FILTER_PAD)
        filt_w = (_pad_to(hf_w1[l], pad2), _pad_to(hf_b1[l][None, :], (1, FILTER_PAD)),
                  _pad_to(hf_freq[l][None, :], (1, FILTER_PAD)), _pad_to(hf_w2[l], pad2),
                  _pad_to(hf_b2[l][None, :], (1, FILTER_PAD)), _pad_to(hf_w3[l], (FILTER_PAD, hf_w3.shape[-1])))
        for name, g in groups.items():
            n, seq, x, mod = g["n"], g["seq"], xs[name], mods[name]
            is_ctx = name == "ctx"
            proj = _inproj(x, mod, w_in_b[l])
            att_out = _attention(proj, n, seq, q_norm[l], k_norm[l], g["rope"],
                                 None if is_ctx else (cache_k, cache_v), l, emit_cache=is_ctx)
            ret, s_fin = _retention(proj, n, seq, log_gamma[l],
                                    ret_zero if is_ctx else state_ret, 0 if is_ctx else l)
            spectra = _hyena_filters(seq, g["feat"], *filt_w, abs_deltas, g["dft"])
            hyz = _hyena(proj, n, seq, hy_conv[l], hy_bias[l], spectra, g["dft"])
            if is_ctx:
                att, own_k, own_v = att_out
                ks_out.append(own_k)
                vs_out.append(own_v)
                ss_out.append(s_fin)
            else:
                att = att_out[0]
            x = _outproj(att, ret, hyz, x, mod, w_out_b[l], ln1_w[l], ln1_b[l])
            xs[name] = _ffn(x, seq, mod, w_up_b[l], ffn_conv[l], w_down_b[l], ln2_w[l], ln2_b[l])

    return (xs["ctx"].reshape(n_ctx, seq_ctx, D_MODEL), xs["lat"].reshape(n_lat, seq_lat, D_MODEL),
            jnp.stack(ks_out, axis=1), jnp.stack(vs_out, axis=1), jnp.stack(ss_out, axis=1))
```

```python
import functools
import math

import jax
import jax.numpy as jnp
from jax import lax
from jax.experimental import pallas as pl
from jax.experimental.pallas import tpu as pltpu

F32 = jnp.float32
BF16 = jnp.bfloat16

D_MODEL = 2048
DEPTH = 2
GRID_W = 64
HEAD_DIM = 128
ATT_WIDTH = D_MODEL // 2
N_ATT_HEADS = ATT_WIDTH // HEAD_DIM
N_KV_HEADS = N_ATT_HEADS // 4
GQA_GROUP = N_ATT_HEADS // N_KV_HEADS
KV_WIDTH = N_KV_HEADS * HEAD_DIM
RET_WIDTH = D_MODEL // 4
N_RET_HEADS = RET_WIDTH // HEAD_DIM
HYENA_WIDTH = D_MODEL // 4
HYENA_ORDER = 2
IN_WIDTH = ATT_WIDTH + 2 * KV_WIDTH + 4 * RET_WIDTH + (HYENA_ORDER + 1) * HYENA_WIDTH
D_FF = 5632
ROPE_THETA = 10000.0
FILTER_BANDS = 16
FILTER_EMB = 1 + 2 * FILTER_BANDS
FILTER_HID = 64
HYENA_DECAY_TARGET = 1e-2
HYENA_DECAY_PCT_MIN = 0.3
HYENA_DECAY_PCT_MAX = 1.5
DEEPNORM_ALPHA = (2 * DEPTH) ** 0.25
EPS = 1e-6

COL_Q = 0
COL_K = ATT_WIDTH
COL_V = COL_K + KV_WIDTH
COL_RQ = COL_V + KV_WIDTH
COL_RK = COL_RQ + RET_WIDTH
COL_RV = COL_RK + RET_WIDTH
COL_RG = COL_RV + RET_WIDTH
COL_HY = COL_RG + RET_WIDTH

LANES = 128
HALO = 16
VMEM_LIMIT = 56 * 1024 * 1024
COND_ROWS = 16

ADA_TN = 1024
INPROJ_TM, INPROJ_TN = 256, 512
OUTPROJ_TM = 256
FFN_TM, FFN_TF = 512, 512
ATT_TQ = 256
HYENA_TW = 256
FILTER_PAD = 128

NT_DIMS = (((1,), (1,)), ((), ()))
TN_DIMS = (((0,), (0,)), ((), ()))


def _params(n_grid):
    return pltpu.CompilerParams(dimension_semantics=("arbitrary",) * n_grid,
                                vmem_limit_bytes=VMEM_LIMIT)


def _dot(a, b):
    return jnp.dot(a, b, preferred_element_type=F32)


def _split(a):
    hi = a.astype(BF16)
    lo = (a - hi.astype(F32)).astype(BF16)
    return hi, lo


def _dot3(a_hi, a_lo, b_hi, b_lo):
    return _dot(a_hi, b_hi) + _dot(a_lo, b_hi) + _dot(a_hi, b_lo)


def _silu(x):
    return x * jax.nn.sigmoid(x)


def _layer_norm(y, w, b):
    yc = y - jnp.mean(y, -1, keepdims=True)
    var = jnp.mean(yc * yc, -1, keepdims=True)
    return yc * lax.rsqrt(var + EPS) * w + b


def _shift_rows(x, down):
    n = x.shape[0]
    return pltpu.roll(x, 1 if down else n - 1, axis=0)


def _ada_kernel(cond_ref, w_ref, b_ref, o_ref):
    a = _silu(cond_ref[...]).astype(BF16)
    o_ref[...] = _dot(a, w_ref[...].astype(BF16)) + b_ref[...]


def _ada(cond, w_ada, b_ada):
    n_out = w_ada.shape[-1]
    return pl.pallas_call(
        _ada_kernel,
        grid=(DEPTH, n_out // ADA_TN),
        in_specs=[
            pl.BlockSpec((COND_ROWS, D_MODEL), lambda l, j: (0, 0)),
            pl.BlockSpec((None, D_MODEL, ADA_TN), lambda l, j: (l, 0, j)),
            pl.BlockSpec((None, 1, ADA_TN), lambda l, j: (l, 0, j)),
        ],
        out_specs=pl.BlockSpec((None, COND_ROWS, ADA_TN), lambda l, j: (l, 0, j)),
        out_shape=jax.ShapeDtypeStruct((DEPTH, COND_ROWS, n_out), F32),
        compiler_params=_params(2),
        name="ada",
    )(cond, w_ada, b_ada.reshape(DEPTH, 1, n_out))


def _inproj_kernel(x_ref, sh_ref, sc_ref, w_ref, o_ref):
    h = (x_ref[...] * (1.0 + sc_ref[...]) + sh_ref[...]).astype(BF16)
    for n0 in range(0, IN_WIDTH, INPROJ_TN):
        o_ref[:, n0:n0 + INPROJ_TN] = _dot(h, w_ref[:, n0:n0 + INPROJ_TN])


def _inproj(x, mod, w_in, layer):
    rows = x.shape[0]
    tm = INPROJ_TM
    tiles_per_mod = rows // mod.shape[0] // tm
    return pl.pallas_call(
        _inproj_kernel,
        grid=(rows // tm,),
        in_specs=[
            pl.BlockSpec((tm, D_MODEL), lambda i: (i, 0)),
            pl.BlockSpec((None, 1, D_MODEL), lambda i: (i // tiles_per_mod, 0, 0)),
            pl.BlockSpec((None, 1, D_MODEL), lambda i: (i // tiles_per_mod, 0, 1)),
            pl.BlockSpec((None, D_MODEL, IN_WIDTH), lambda i: (layer, 0, 0), pipeline_mode=pl.Buffered(1)),
        ],
        out_specs=pl.BlockSpec((tm, IN_WIDTH), lambda i: (i, 0)),
        out_shape=jax.ShapeDtypeStruct((rows, IN_WIDTH), F32),
        compiler_params=_params(1),
        name="inproj",
    )(x, mod, mod, w_in)


def _rms_norm(x, w):
    return x * lax.rsqrt(jnp.mean(x * x, -1, keepdims=True) + EPS) * w


def _rope(x, cos_full, sin_signed):
    return x * cos_full + pltpu.roll(x, HEAD_DIM // 2, axis=1) * sin_signed


def _attn_kernel(*refs, rope, has_ctx, emit_cache):
    refs = list(refs)
    q_ref, k_ref, v_ref, qw_ref, kw_ref = refs[:5]
    del refs[:5]
    if rope:
        qcos_ref, qsin_ref, kcos_ref, ksin_ref = refs[:4]
        del refs[:4]
    if has_ctx:
        ck_ref, cv_ref = refs[:2]
        del refs[:2]
    o_ref = refs.pop(0)
    if emit_cache:
        ownk_ref, ownv_ref = refs[:2]
        del refs[:2]
    k_scr, v_scr = refs[:2]
    if has_ctx:
        ck_scr, cv_scr = refs[2:]

    @pl.when(pl.program_id(2) == 0)
    def _():
        kn = _rms_norm(k_ref[...], kw_ref[...])
        if emit_cache:
            ownk_ref[...] = kn
            ownv_ref[...] = v_ref[...]
        if rope:
            kn = _rope(kn, kcos_ref[...], ksin_ref[...])
        k_scr[...] = kn.astype(BF16)
        v_scr[...] = v_ref[...].astype(BF16)
        if has_ctx:
            ck_scr[...] = ck_ref[...].astype(BF16)
            cv_scr[...] = cv_ref[...].astype(BF16)

    for g in range(GQA_GROUP):
        qn = _rms_norm(q_ref[:, g * HEAD_DIM:(g + 1) * HEAD_DIM], qw_ref[...])
        if rope:
            qn = _rope(qn, qcos_ref[...], qsin_ref[...])
        qb = (qn * (HEAD_DIM ** -0.5)).astype(BF16)
        s = lax.dot_general(qb, k_scr[...], NT_DIMS, preferred_element_type=F32)
        m = jnp.max(s, -1, keepdims=True)
        if has_ctx:
            sc = lax.dot_general(qb, ck_scr[...], NT_DIMS, preferred_element_type=F32)
            m = jnp.maximum(m, jnp.max(sc, -1, keepdims=True))
        p = jnp.exp(s - m)
        den = jnp.sum(p, -1, keepdims=True)
        o = _dot(p.astype(BF16), v_scr[...])
        if has_ctx:
            pc = jnp.exp(sc - m)
            den = den + jnp.sum(pc, -1, keepdims=True)
            o = o + _dot(pc.astype(BF16), cv_scr[...])
        o_ref[:, g * HEAD_DIM:(g + 1) * HEAD_DIM] = o / den


def _attention(proj, n_batch, seq, q_norm, k_norm, rope_tabs, ctx_kv, layer, emit_cache):
    rows = proj.shape[0]
    tq = min(ATT_TQ, seq)
    nq = seq // tq
    group_w = GQA_GROUP * HEAD_DIM
    rope = rope_tabs is not None
    has_ctx = ctx_kv is not None

    in_specs = [
        pl.BlockSpec((tq, group_w), lambda b, kv, qi: (b * nq + qi, COL_Q // group_w + kv)),
        pl.BlockSpec((seq, HEAD_DIM), lambda b, kv, qi: (b, COL_K // HEAD_DIM + kv)),
        pl.BlockSpec((seq, HEAD_DIM), lambda b, kv, qi: (b, COL_V // HEAD_DIM + kv)),
        pl.BlockSpec((1, HEAD_DIM), lambda b, kv, qi: (0, 0)),
        pl.BlockSpec((1, HEAD_DIM), lambda b, kv, qi: (0, 0)),
    ]
    args = [proj, proj, proj, q_norm.reshape(1, HEAD_DIM), k_norm.reshape(1, HEAD_DIM)]
    if rope:
        cos_full, sin_signed = rope_tabs
        in_specs += [
            pl.BlockSpec((tq, HEAD_DIM), lambda b, kv, qi: (qi, 0)),
            pl.BlockSpec((tq, HEAD_DIM), lambda b, kv, qi: (qi, 0)),
            pl.BlockSpec((seq, HEAD_DIM), lambda b, kv, qi: (0, 0)),
            pl.BlockSpec((seq, HEAD_DIM), lambda b, kv, qi: (0, 0)),
        ]
        args += [cos_full, sin_signed, cos_full, sin_signed]
    if has_ctx:
        ck, cv = ctx_kv
        past = ck.shape[3]
        ctx_spec = pl.BlockSpec((None, None, None, past, HEAD_DIM), lambda b, kv, qi: (b, layer, kv, 0, 0))
        in_specs += [ctx_spec, ctx_spec]
        args += [ck, cv]

    out_specs = [pl.BlockSpec((tq, group_w), lambda b, kv, qi: (b * nq + qi, kv))]
    out_shape = [jax.ShapeDtypeStruct((rows, ATT_WIDTH), F32)]
    if emit_cache:
        own_spec = pl.BlockSpec((None, None, seq, HEAD_DIM), lambda b, kv, qi: (b, kv, 0, 0))
        out_specs += [own_spec, own_spec]
        out_shape += [jax.ShapeDtypeStruct((n_batch, N_KV_HEADS, seq, HEAD_DIM), F32)] * 2

    return pl.pallas_call(
        functools.partial(_attn_kernel, rope=rope, has_ctx=has_ctx, emit_cache=emit_cache),
        grid=(n_batch, N_KV_HEADS, nq),
        in_specs=in_specs,
        out_specs=out_specs,
        out_shape=out_shape,
        scratch_shapes=[pltpu.VMEM((n_keys, HEAD_DIM), BF16)
                        for n_keys in [seq, seq] + ([past, past] if has_ctx else [])],
        compiler_params=_params(3),
        name="attention",
    )(*args)


def _retention_kernel(lg_ref, q_ref, k_ref, v_ref, g_ref, s0_ref, o_ref, sf_ref, *, seq):
    h = pl.program_id(1)
    lg_f = lg_ref[h]
    lg_b = lg_ref[N_RET_HEADS + h]

    q = q_ref[...]
    k = k_ref[...] * (HEAD_DIM ** -0.5)
    vb = v_ref[...].astype(BF16)
    a = lax.dot_general(q.astype(BF16), k.astype(BF16), NT_DIMS, preferred_element_type=F32)

    i = lax.broadcasted_iota(jnp.int32, (seq, seq), 0)
    j = lax.broadcasted_iota(jnp.int32, (seq, seq), 1)
    d = (i - j).astype(F32)
    decay = jnp.exp(jnp.where(d >= 0, lg_f, lg_b) * jnp.abs(d))
    decay = jnp.where(d == 0, 2.0, decay)
    o = _dot((a * decay).astype(BF16), vb)

    pos = lax.broadcasted_iota(jnp.int32, (seq, 1), 0).astype(F32)
    s0_f = s0_ref[0]
    s0_b = s0_ref[1]
    o = o + _dot((q * jnp.exp(lg_f * (pos + 1.0))).astype(BF16), s0_f.astype(BF16))
    o = o + _dot((q * jnp.exp(lg_b * (seq - pos))).astype(BF16), s0_b.astype(BF16))

    kf = (k * jnp.exp(lg_f * (seq - 1.0 - pos))).astype(BF16)
    kb = (k * jnp.exp(lg_b * pos)).astype(BF16)
    ones = jnp.ones((1, HEAD_DIM), F32)
    sf_ref[0] = jnp.exp(ones * (lg_f * seq)) * s0_f + lax.dot_general(kf, vb, TN_DIMS, preferred_element_type=F32)
    sf_ref[1] = jnp.exp(ones * (lg_b * seq)) * s0_b + lax.dot_general(kb, vb, TN_DIMS, preferred_element_type=F32)

    oc = o - jnp.mean(o, -1, keepdims=True)
    var = jnp.mean(oc * oc, -1, keepdims=True)
    o_ref[...] = _silu(g_ref[...]) * (oc * lax.rsqrt(var + EPS))


def _retention(proj, n_batch, seq, log_gamma, s0, layer):
    rows = proj.shape[0]

    def col(offset):
        return pl.BlockSpec((seq, HEAD_DIM), lambda b, h: (b, offset // HEAD_DIM + h))

    return pl.pallas_call(
        functools.partial(_retention_kernel, seq=seq),
        grid=(n_batch, N_RET_HEADS),
        in_specs=[
            pl.BlockSpec(memory_space=pltpu.SMEM),
            col(COL_RQ), col(COL_RK), col(COL_RV), col(COL_RG),
            pl.BlockSpec((None, None, 2, None, HEAD_DIM, HEAD_DIM), lambda b, h: (b, layer, 0, h, 0, 0)),
        ],
        out_specs=[
            pl.BlockSpec((seq, HEAD_DIM), lambda b, h: (b, h)),
            pl.BlockSpec((None, 2, None, HEAD_DIM, HEAD_DIM), lambda b, h: (b, 0, h, 0, 0)),
        ],
        out_shape=[
            jax.ShapeDtypeStruct((rows, RET_WIDTH), F32),
            jax.ShapeDtypeStruct((n_batch, 2, N_RET_HEADS, HEAD_DIM, HEAD_DIM), F32),
        ],
        compiler_params=_params(2),
        name="retention",
    )(log_gamma.reshape(2 * N_RET_HEADS), proj, proj, proj, proj, s0)


def _alt_sign(n):
    r = lax.broadcasted_iota(jnp.int32, (n, 1), 0)
    return jnp.where((r & 1) == 0, 1.0, -1.0).astype(F32)


def _filter_kernel(z_ref, w1_ref, b1_ref, fr_ref, w2_ref, b2_ref, w3f_ref, w3b_ref, dl_ref,
                   ch_ref, cl_ref, sh_ref, sl_ref, kr_ref, ki_ref, kn_ref, *, seq):
    freq = fr_ref[...]
    hdn = jnp.sin(freq * (_dot3(*_split(z_ref[...]), *_split(w1_ref[...])) + b1_ref[...]))
    hdn = jnp.sin(freq * (_dot3(*_split(hdn), *_split(w2_ref[...])) + b2_ref[...]))
    hdn_hi, hdn_lo = _split(hdn)

    pos = lax.broadcasted_iota(jnp.int32, (seq, 1), 0)
    t = pos.astype(F32) / max(seq - 1, 1)
    decay = jnp.exp(-t * dl_ref[...])
    h_fwd = _dot3(hdn_hi, hdn_lo, *_split(w3f_ref[...])) * decay
    h_bwd = _dot3(hdn_hi, hdn_lo, *_split(w3b_ref[...])) * decay

    h_bwd_shift = jnp.where(pos == 0, 0.0, _shift_rows(h_bwd, down=True))
    alt = _alt_sign(seq)
    kr = _dot3(ch_ref[...], cl_ref[...], *_split(h_fwd + h_bwd_shift)) + alt * h_bwd[seq - 1:seq, :]
    ki = _dot3(sh_ref[...], sl_ref[...], *_split(h_bwd_shift - h_fwd))
    wgt = jnp.where(pos == 0, 1.0, 2.0) * (0.5 / seq)
    kr_ref[...] = kr * wgt
    ki_ref[...] = ki * wgt
    kn_ref[...] = jnp.sum(alt * (h_fwd - h_bwd), axis=0, keepdims=True) * (0.5 / seq)


def _hyena_filters(seq, zfeat, w1, b1, freq, w2, b2, w3, abs_deltas, dft):
    tw = HYENA_TW
    n_w = HYENA_WIDTH // tw
    const = lambda o, wj: (0, 0)
    mat = pl.BlockSpec((seq, seq), const)
    vec = pl.BlockSpec((1, FILTER_PAD), const)
    return pl.pallas_call(
        functools.partial(_filter_kernel, seq=seq),
        grid=(HYENA_ORDER, n_w),
        in_specs=[
            pl.BlockSpec((seq, FILTER_PAD), const),
            pl.BlockSpec((FILTER_PAD, FILTER_PAD), const), vec, vec,
            pl.BlockSpec((FILTER_PAD, FILTER_PAD), const), vec,
            pl.BlockSpec((FILTER_PAD, tw), lambda o, wj: (0, o * 2 * n_w + wj)),
            pl.BlockSpec((FILTER_PAD, tw), lambda o, wj: (0, o * 2 * n_w + n_w + wj)),
            pl.BlockSpec((1, tw), lambda o, wj: (0, wj)),
            mat, mat, mat, mat,
        ],
        out_specs=[
            pl.BlockSpec((None, seq, tw), lambda o, wj: (o, 0, wj)),
            pl.BlockSpec((None, seq, tw), lambda o, wj: (o, 0, wj)),
            pl.BlockSpec((None, 1, tw), lambda o, wj: (o, 0, wj)),
        ],
        out_shape=[
            jax.ShapeDtypeStruct((HYENA_ORDER, seq, HYENA_WIDTH), F32),
            jax.ShapeDtypeStruct((HYENA_ORDER, seq, HYENA_WIDTH), F32),
            jax.ShapeDtypeStruct((HYENA_ORDER, 1, HYENA_WIDTH), F32),
        ],
        compiler_params=_params(2),
        name="hyena_filters",
    )(zfeat, w1, b1, freq, w2, b2, w3, w3, abs_deltas, *dft)


def _dwconv3_seq(x, w, pos, seq):
    prev = jnp.where(pos == 0, 0.0, _shift_rows(x, down=True))
    nxt = jnp.where(pos == seq - 1, 0.0, _shift_rows(x, down=False))
    return prev * w[0:1] + x * w[1:2] + nxt * w[2:3]


def _hyena_kernel(v_ref, x1_ref, x2_ref, cv_ref, c1_ref, c2_ref, bias_ref, kr_ref, ki_ref, kn_ref,
                  ch_ref, cl_ref, sh_ref, sl_ref, o_ref, *, seq):
    pos = lax.broadcasted_iota(jnp.int32, (seq, 1), 0)
    alt = _alt_sign(seq)
    z = _dwconv3_seq(v_ref[...], cv_ref[...], pos, seq)
    gates = (_dwconv3_seq(x1_ref[...], c1_ref[...], pos, seq),
             _dwconv3_seq(x2_ref[...], c2_ref[...], pos, seq))
    c_hi, c_lo, s_hi, s_lo = ch_ref[...], cl_ref[...], sh_ref[...], sl_ref[...]
    for o in range(HYENA_ORDER):
        z_hi, z_lo = _split(z)
        ur = _dot3(c_hi, c_lo, z_hi, z_lo)
        ui = _dot3(s_hi, s_lo, z_hi, z_lo)
        u_nyq = jnp.sum(alt * z, axis=0, keepdims=True)
        kr, ki = kr_ref[o], ki_ref[o]
        yr = ur * kr + ui * ki
        yi = ui * kr - ur * ki
        y = (_dot3(c_hi, c_lo, *_split(yr)) + _dot3(s_hi, s_lo, *_split(yi))
             + alt * (u_nyq * kn_ref[o]))
        z = gates[o] * (y + z * bias_ref[o:o + 1, :])
    o_ref[...] = z


def _hyena(proj, n_batch, seq, hy_conv, hy_bias, spectra, dft):
    rows = proj.shape[0]
    tw = HYENA_TW
    n_w = HYENA_WIDTH // tw
    kr, ki, kn = spectra

    def part(p):
        return pl.BlockSpec((seq, tw), lambda b, wj: (b, (COL_HY + p * HYENA_WIDTH) // tw + wj))

    def conv(p):
        return pl.BlockSpec((3, tw), lambda b, wj: (0, p * n_w + wj))

    mat = pl.BlockSpec((seq, seq), lambda b, wj: (0, 0))
    return pl.pallas_call(
        functools.partial(_hyena_kernel, seq=seq),
        grid=(n_batch, n_w),
        in_specs=[
            part(0), part(1), part(2), conv(0), conv(1), conv(2),
            pl.BlockSpec((HYENA_ORDER, tw), lambda b, wj: (0, wj)),
            pl.BlockSpec((HYENA_ORDER, seq, tw), lambda b, wj: (0, 0, wj)),
            pl.BlockSpec((HYENA_ORDER, seq, tw), lambda b, wj: (0, 0, wj)),
            pl.BlockSpec((HYENA_ORDER, 1, tw), lambda b, wj: (0, 0, wj)),
            mat, mat, mat, mat,
        ],
        out_specs=pl.BlockSpec((seq, tw), lambda b, wj: (b, wj)),
        out_shape=jax.ShapeDtypeStruct((rows, HYENA_WIDTH), F32),
        compiler_params=_params(2),
        name="hyena",
    )(proj, proj, proj, hy_conv, hy_conv, hy_conv, hy_bias, kr, ki, kn, *dft)


def _outproj_kernel(att_ref, ret_ref, hy_ref, x_ref, g_ref, w_ref, lw_ref, lb_ref, o_ref):
    r0, r1 = ATT_WIDTH, ATT_WIDTH + RET_WIDTH
    mix = (_dot(att_ref[...].astype(BF16), w_ref[0:r0, :])
           + _dot(ret_ref[...].astype(BF16), w_ref[r0:r1, :])
           + _dot(hy_ref[...].astype(BF16), w_ref[r1:, :]))
    y = DEEPNORM_ALPHA * x_ref[...] + g_ref[...] * mix
    o_ref[...] = _layer_norm(y, lw_ref[...], lb_ref[...])


def _outproj(att, ret, hyz, x, mod, w_out, layer, ln_w, ln_b):
    rows = x.shape[0]
    tm = OUTPROJ_TM
    tiles_per_mod = rows // mod.shape[0] // tm
    row = lambda width: pl.BlockSpec((tm, width), lambda i: (i, 0))
    vec = pl.BlockSpec((1, D_MODEL), lambda i: (0, 0))
    return pl.pallas_call(
        _outproj_kernel,
        grid=(rows // tm,),
        in_specs=[
            row(ATT_WIDTH), row(RET_WIDTH), row(HYENA_WIDTH), row(D_MODEL),
            pl.BlockSpec((None, 1, D_MODEL), lambda i: (i // tiles_per_mod, 0, 2)),
            pl.BlockSpec((None, D_MODEL, D_MODEL), lambda i: (layer, 0, 0), pipeline_mode=pl.Buffered(1)),
            vec, vec,
        ],
        out_specs=row(D_MODEL),
        out_shape=jax.ShapeDtypeStruct((rows, D_MODEL), F32),
        compiler_params=_params(1),
        name="outproj_ln",
    )(att, ret, hyz, x, mod, w_out, ln_w.reshape(1, D_MODEL), ln_b.reshape(1, D_MODEL))


def _ffn_kernel(x_ref, xp_ref, xn_ref, sh_ref, sc_ref, g_ref, wg_ref, wu_ref, cg_ref, cu_ref, wd_ref,
                lw_ref, lb_ref, o_ref, h_scr, acc_scr, *, seq, tm):
    i = pl.program_id(0)
    j = pl.program_id(1)

    @pl.when(j == 0)
    def _():
        sc = 1.0 + sc_ref[...]
        sh = sh_ref[...]
        h_scr[0:HALO, :] = (xp_ref[...] * sc + sh).astype(BF16)
        h_scr[HALO:HALO + tm, :] = (x_ref[...] * sc + sh).astype(BF16)
        h_scr[HALO + tm:, :] = (xn_ref[...] * sc + sh).astype(BF16)
        acc_scr[...] = jnp.zeros_like(acc_scr)

    pos = (i * tm + lax.broadcasted_iota(jnp.int32, (tm, 1), 0)) & (seq - 1)
    h = h_scr[...]

    def up_conv(w_ref, c_ref):
        up = _dot(h, w_ref[...])
        prev = _shift_rows(up, down=True)[HALO:HALO + tm]
        nxt = _shift_rows(up, down=False)[HALO:HALO + tm]
        c = c_ref[...]
        return (jnp.where(pos == 0, 0.0, prev) * c[0:1] + up[HALO:HALO + tm] * c[1:2]
                + jnp.where(pos == seq - 1, 0.0, nxt) * c[2:3])

    act = _silu(up_conv(wg_ref, cg_ref)) * up_conv(wu_ref, cu_ref)
    acc_scr[...] += _dot(act.astype(BF16), wd_ref[...])

    @pl.when(j == pl.num_programs(1) - 1)
    def _():
        y = DEEPNORM_ALPHA * x_ref[...] + g_ref[...] * acc_scr[...]
        o_ref[...] = _layer_norm(y, lw_ref[...], lb_ref[...])


def _ffn(x, seq, mod, w_up, ffn_conv, w_down, layer, ln_w, ln_b):
    rows = x.shape[0]
    tm, tf = FFN_TM, FFN_TF
    n_f = D_FF // tf
    tiles_per_mod = rows // mod.shape[0] // tm
    halo_per_tile = tm // HALO
    last_halo = rows // HALO - 1

    def mod_spec(chunk):
        return pl.BlockSpec((None, 1, D_MODEL), lambda i, j: (i // tiles_per_mod, 0, chunk))

    vec = pl.BlockSpec((1, D_MODEL), lambda i, j: (0, 0))
    return pl.pallas_call(
        functools.partial(_ffn_kernel, seq=seq, tm=tm),
        grid=(rows // tm, n_f),
        in_specs=[
            pl.BlockSpec((tm, D_MODEL), lambda i, j: (i, 0)),
            pl.BlockSpec((HALO, D_MODEL), lambda i, j: (jnp.maximum(i * halo_per_tile - 1, 0), 0)),
            pl.BlockSpec((HALO, D_MODEL), lambda i, j: (jnp.minimum((i + 1) * halo_per_tile, last_halo), 0)),
            mod_spec(3), mod_spec(4), mod_spec(5),
            pl.BlockSpec((None, D_MODEL, tf), lambda i, j: (layer, 0, j)),
            pl.BlockSpec((None, D_MODEL, tf), lambda i, j: (layer, 0, n_f + j)),
            pl.BlockSpec((None, 3, tf), lambda i, j: (layer, 0, j)),
            pl.BlockSpec((None, 3, tf), lambda i, j: (layer, 0, n_f + j)),
            pl.BlockSpec((None, tf, D_MODEL), lambda i, j: (layer, j, 0)),
            vec, vec,
        ],
        out_specs=pl.BlockSpec((tm, D_MODEL), lambda i, j: (i, 0)),
        out_shape=jax.ShapeDtypeStruct((rows, D_MODEL), F32),
        scratch_shapes=[pltpu.VMEM((tm + 2 * HALO, D_MODEL), BF16), pltpu.VMEM((tm, D_MODEL), F32)],
        compiler_params=_params(2),
        name="ffn_ln",
    )(x, x, x, mod, mod, mod, w_up, w_up, ffn_conv, ffn_conv, w_down,
      ln_w.reshape(1, D_MODEL), ln_b.reshape(1, D_MODEL))


def _rope_tables(n_tokens):
    rows = n_tokens // GRID_W
    row = jnp.repeat(jnp.arange(rows, dtype=F32), GRID_W)
    col = jnp.tile(jnp.arange(GRID_W, dtype=F32), rows)
    n_freq = HEAD_DIM // 4
    inv_freq = ROPE_THETA ** (-jnp.arange(n_freq, dtype=F32) / n_freq)
    ang = jnp.concatenate([row[:, None] * inv_freq[None], col[:, None] * inv_freq[None]], -1)
    cos, sin = jnp.cos(ang), jnp.sin(ang)
    return jnp.concatenate([cos, cos], -1), jnp.concatenate([-sin, sin], -1)


def _dft_tables(seq):
    idx = jnp.arange(seq, dtype=jnp.int32)
    kn = (idx[:, None] * idx[None, :]) % (2 * seq)
    ang = kn.astype(F32) * (math.pi / seq)
    return (*_split(jnp.cos(ang)), *_split(jnp.sin(ang)))


def _filter_features(seq):
    pos = jnp.arange(seq, dtype=F32)
    t = pos / max(seq - 1, 1)
    w = 2.0 * math.pi * pos / seq
    f = jnp.linspace(1e-4, FILTER_BANDS - 1, FILTER_BANDS, dtype=F32)
    ang = w[:, None] * f[None, :]
    z = jnp.concatenate([t[:, None], jnp.cos(ang), -jnp.sin(ang)], -1)
    return jnp.pad(z, ((0, 0), (0, FILTER_PAD - FILTER_EMB)))


def _pad_to(a, shape):
    return jnp.pad(a, [(0, s - d) for d, s in zip(a.shape, shape)])


def kernel(x_prompt, x_sample, cache_k, cache_v, state_ret, c, c_ctx, w_ada, b_ada, w_in, q_norm, k_norm, ret_decay, hy_conv, hf_w1, hf_b1, hf_freq, hf_w2, hf_b2, hf_w3, hy_bias, w_out, ln1_w, ln1_b, w_up, ffn_conv, w_down, ln2_w, ln2_b):
    n_ctx, seq_ctx, _ = x_prompt.shape
    n_lat, seq_lat, _ = x_sample.shape

    cond = _pad_to(jnp.concatenate([c_ctx[None, :], c], 0), (COND_ROWS, D_MODEL))
    mod_all = _ada(cond, w_ada, b_ada)

    w_in_b, w_out_b = w_in.astype(BF16), w_out.astype(BF16)
    w_up_b, w_down_b = w_up.astype(BF16), w_down.astype(BF16)
    log_gamma = jax.nn.log_sigmoid(ret_decay.astype(F32))
    abs_deltas = jnp.abs(jnp.linspace(math.log(HYENA_DECAY_TARGET) / HYENA_DECAY_PCT_MIN,
                                      math.log(HYENA_DECAY_TARGET) / HYENA_DECAY_PCT_MAX,
                                      HYENA_WIDTH, dtype=F32))[None, :]
    ret_zero = jnp.zeros((n_ctx, 1, 2, N_RET_HEADS, HEAD_DIM, HEAD_DIM), F32)

    groups = {
        "ctx": dict(n=n_ctx, seq=seq_ctx, rope=None, dft=_dft_tables(seq_ctx), feat=_filter_features(seq_ctx)),
        "lat": dict(n=n_lat, seq=seq_lat, rope=_rope_tables(seq_lat), dft=_dft_tables(seq_lat),
                    feat=_filter_features(seq_lat)),
    }
    xs = {"ctx": x_prompt.reshape(n_ctx * seq_ctx, D_MODEL), "lat": x_sample.reshape(n_lat * seq_lat, D_MODEL)}
    ks_out, vs_out, ss_out = [], [], []

    for l in range(DEPTH):
        mods = {"ctx": mod_all[l, 0:1, None, :], "lat": mod_all[l, 1:1 + n_lat, None, :]}
        pad2 = (FILTER_PAD, FILTER_PAD)
        filt_w = (_pad_to(hf_w1[l], pad2), _pad_to(hf_b1[l][None, :], (1, FILTER_PAD)),
                  _pad_to(hf_freq[l][None, :], (1, FILTER_PAD)), _pad_to(hf_w2[l], pad2),
                  _pad_to(hf_b2[l][None, :], (1, FILTER_PAD)), _pad_to(hf_w3[l], (FILTER_PAD, hf_w3.shape[-1])))
        for name, g in groups.items():
            n, seq, x, mod = g["n"], g["seq"], xs[name], mods[name]
            is_ctx = name == "ctx"
            proj = _inproj(x, mod, w_in_b, l)
            att_out = _attention(proj, n, seq, q_norm[l], k_norm[l], g["rope"],
                                 None if is_ctx else (cache_k, cache_v), l, emit_cache=is_ctx)
            ret, s_fin = _retention(proj, n, seq, log_gamma[l],
                                    ret_zero if is_ctx else state_ret, 0 if is_ctx else l)
            spectra = _hyena_filters(seq, g["feat"], *filt_w, abs_deltas, g["dft"])
            hyz = _hyena(proj, n, seq, hy_conv[l], hy_bias[l], spectra, g["dft"])
            if is_ctx:
                att, own_k, own_v = att_out
                ks_out.append(own_k)
                vs_out.append(own_v)
                ss_out.append(s_fin)
            else:
                att = att_out[0]
            x = _outproj(att, ret, hyz, x, mod, w_out_b, l, ln1_w[l], ln1_b[l])
            xs[name] = _ffn(x, seq, mod, w_up_b, ffn_conv, w_down_b, l, ln2_w[l], ln2_b[l])

    return (xs["ctx"].reshape(n_ctx, seq_ctx, D_MODEL), xs["lat"].reshape(n_lat, seq_lat, D_MODEL),
            jnp.stack(ks_out, axis=1), jnp.stack(vs_out, axis=1), jnp.stack(ss_out, axis=1))
```

```python
import functools
import math

import jax
import jax.numpy as jnp
from jax import lax
from jax.experimental import pallas as pl
from jax.experimental.pallas import tpu as pltpu

F32 = jnp.float32
BF16 = jnp.bfloat16

D_MODEL = 2048
DEPTH = 2
GRID_W = 64
HEAD_DIM = 128
ATT_WIDTH = D_MODEL // 2
N_ATT_HEADS = ATT_WIDTH // HEAD_DIM
N_KV_HEADS = N_ATT_HEADS // 4
GQA_GROUP = N_ATT_HEADS // N_KV_HEADS
KV_WIDTH = N_KV_HEADS * HEAD_DIM
RET_WIDTH = D_MODEL // 4
N_RET_HEADS = RET_WIDTH // HEAD_DIM
HYENA_WIDTH = D_MODEL // 4
HYENA_ORDER = 2
IN_WIDTH = ATT_WIDTH + 2 * KV_WIDTH + 4 * RET_WIDTH + (HYENA_ORDER + 1) * HYENA_WIDTH
D_FF = 5632
ROPE_THETA = 10000.0
FILTER_BANDS = 16
FILTER_EMB = 1 + 2 * FILTER_BANDS
FILTER_HID = 64
HYENA_DECAY_TARGET = 1e-2
HYENA_DECAY_PCT_MIN = 0.3
HYENA_DECAY_PCT_MAX = 1.5
DEEPNORM_ALPHA = (2 * DEPTH) ** 0.25
EPS = 1e-6

COL_Q = 0
COL_K = ATT_WIDTH
COL_V = COL_K + KV_WIDTH
COL_RQ = COL_V + KV_WIDTH
COL_RK = COL_RQ + RET_WIDTH
COL_RV = COL_RK + RET_WIDTH
COL_RG = COL_RV + RET_WIDTH
COL_HY = COL_RG + RET_WIDTH

LANES = 128
HALO = 16
VMEM_LIMIT = 56 * 1024 * 1024
COND_ROWS = 16

ADA_TN = 1024
INPROJ_TM, INPROJ_TN = 256, 512
OUTPROJ_TM, OUTPROJ_SUB = 512, 256
FFN_TM, FFN_TF = 512, 512
ATT_TQ = 256
FILTER_PAD = 128

NT_DIMS = (((1,), (1,)), ((), ()))
TN_DIMS = (((0,), (0,)), ((), ()))


def _params(n_grid):
    return pltpu.CompilerParams(dimension_semantics=("arbitrary",) * n_grid,
                                vmem_limit_bytes=VMEM_LIMIT)


def _dot(a, b):
    return jnp.dot(a, b, preferred_element_type=F32)


def _split(a):
    hi = a.astype(BF16)
    lo = (a - hi.astype(F32)).astype(BF16)
    return hi, lo


def _dot3(a_hi, a_lo, b_hi, b_lo):
    return _dot(a_hi, b_hi) + _dot(a_lo, b_hi) + _dot(a_hi, b_lo)


def _silu(x):
    return x * jax.nn.sigmoid(x)


def _layer_norm(y, w, b):
    yc = y - jnp.mean(y, -1, keepdims=True)
    var = jnp.mean(yc * yc, -1, keepdims=True)
    return yc * lax.rsqrt(var + EPS) * w + b


def _shift_rows(x, down):
    n = x.shape[0]
    return pltpu.roll(x, 1 if down else n - 1, axis=0)


def _ada_kernel(cond_ref, w_ref, b_ref, o_ref):
    a = _silu(cond_ref[...]).astype(BF16)
    o_ref[...] = _dot(a, w_ref[...].astype(BF16)) + b_ref[...]


def _ada(cond, w_ada, b_ada):
    n_out = w_ada.shape[-1]
    return pl.pallas_call(
        _ada_kernel,
        grid=(DEPTH, n_out // ADA_TN),
        in_specs=[
            pl.BlockSpec((COND_ROWS, D_MODEL), lambda l, j: (0, 0)),
            pl.BlockSpec((None, D_MODEL, ADA_TN), lambda l, j: (l, 0, j)),
            pl.BlockSpec((None, 1, ADA_TN), lambda l, j: (l, 0, j)),
        ],
        out_specs=pl.BlockSpec((None, COND_ROWS, ADA_TN), lambda l, j: (l, 0, j)),
        out_shape=jax.ShapeDtypeStruct((DEPTH, COND_ROWS, n_out), F32),
        compiler_params=_params(2),
        name="ada",
    )(cond, w_ada, b_ada.reshape(DEPTH, 1, n_out))


def _inproj_kernel(x_ref, sh_ref, sc_ref, w_ref, o_ref):
    h = (x_ref[...] * (1.0 + sc_ref[...]) + sh_ref[...]).astype(BF16)
    for n0 in range(0, IN_WIDTH, INPROJ_TN):
        o_ref[:, n0:n0 + INPROJ_TN] = _dot(h, w_ref[:, n0:n0 + INPROJ_TN])


def _inproj(x, mod, w_in, layer):
    rows = x.shape[0]
    tm = INPROJ_TM
    tiles_per_mod = rows // mod.shape[0] // tm
    return pl.pallas_call(
        _inproj_kernel,
        grid=(rows // tm,),
        in_specs=[
            pl.BlockSpec((tm, D_MODEL), lambda i: (i, 0)),
            pl.BlockSpec((None, 1, D_MODEL), lambda i: (i // tiles_per_mod, 0, 0)),
            pl.BlockSpec((None, 1, D_MODEL), lambda i: (i // tiles_per_mod, 0, 1)),
            pl.BlockSpec((None, D_MODEL, IN_WIDTH), lambda i: (layer, 0, 0), pipeline_mode=pl.Buffered(1)),
        ],
        out_specs=pl.BlockSpec((tm, IN_WIDTH), lambda i: (i, 0)),
        out_shape=jax.ShapeDtypeStruct((rows, IN_WIDTH), F32),
        compiler_params=_params(1),
        name="inproj",
    )(x, mod, mod, w_in)


def _rms_norm(x, w):
    return x * lax.rsqrt(jnp.mean(x * x, -1, keepdims=True) + EPS) * w


def _rope(x, cos_full, sin_signed):
    return x * cos_full + pltpu.roll(x, HEAD_DIM // 2, axis=1) * sin_signed


def _attn_kernel(*refs, rope, has_ctx, emit_cache):
    refs = list(refs)
    q_ref, k_ref, v_ref, qw_ref, kw_ref = refs[:5]
    del refs[:5]
    if rope:
        qcos_ref, qsin_ref, kcos_ref, ksin_ref = refs[:4]
        del refs[:4]
    if has_ctx:
        ck_ref, cv_ref = refs[:2]
        del refs[:2]
    o_ref = refs.pop(0)
    if emit_cache:
        ownk_ref, ownv_ref = refs[:2]
        del refs[:2]
    k_scr, v_scr = refs[:2]
    if has_ctx:
        ck_scr, cv_scr = refs[2:]

    @pl.when(pl.program_id(2) == 0)
    def _():
        kn = _rms_norm(k_ref[...], kw_ref[...])
        if emit_cache:
            ownk_ref[...] = kn
            ownv_ref[...] = v_ref[...]
        if rope:
            kn = _rope(kn, kcos_ref[...], ksin_ref[...])
        k_scr[...] = kn.astype(BF16)
        v_scr[...] = v_ref[...].astype(BF16)
        if has_ctx:
            ck_scr[...] = ck_ref[...].astype(BF16)
            cv_scr[...] = cv_ref[...].astype(BF16)

    for g in range(GQA_GROUP):
        qn = _rms_norm(q_ref[:, g * HEAD_DIM:(g + 1) * HEAD_DIM], qw_ref[...])
        if rope:
            qn = _rope(qn, qcos_ref[...], qsin_ref[...])
        qb = (qn * (HEAD_DIM ** -0.5)).astype(BF16)
        s = lax.dot_general(qb, k_scr[...], NT_DIMS, preferred_element_type=F32)
        m = jnp.max(s, -1, keepdims=True)
        if has_ctx:
            sc = lax.dot_general(qb, ck_scr[...], NT_DIMS, preferred_element_type=F32)
            m = jnp.maximum(m, jnp.max(sc, -1, keepdims=True))
        p = jnp.exp(s - m)
        den = jnp.sum(p, -1, keepdims=True)
        o = _dot(p.astype(BF16), v_scr[...])
        if has_ctx:
            pc = jnp.exp(sc - m)
            den = den + jnp.sum(pc, -1, keepdims=True)
            o = o + _dot(pc.astype(BF16), cv_scr[...])
        o_ref[:, g * HEAD_DIM:(g + 1) * HEAD_DIM] = (o / den).astype(o_ref.dtype)


def _attention(proj, n_batch, seq, q_norm, k_norm, rope_tabs, ctx_kv, layer, emit_cache):
    rows = proj.shape[0]
    tq = min(ATT_TQ, seq)
    nq = seq // tq
    group_w = GQA_GROUP * HEAD_DIM
    rope = rope_tabs is not None
    has_ctx = ctx_kv is not None

    in_specs = [
        pl.BlockSpec((tq, group_w), lambda b, kv, qi: (b * nq + qi, COL_Q // group_w + kv)),
        pl.BlockSpec((seq, HEAD_DIM), lambda b, kv, qi: (b, COL_K // HEAD_DIM + kv)),
        pl.BlockSpec((seq, HEAD_DIM), lambda b, kv, qi: (b, COL_V // HEAD_DIM + kv)),
        pl.BlockSpec((1, HEAD_DIM), lambda b, kv, qi: (0, 0)),
        pl.BlockSpec((1, HEAD_DIM), lambda b, kv, qi: (0, 0)),
    ]
    args = [proj, proj, proj, q_norm.reshape(1, HEAD_DIM), k_norm.reshape(1, HEAD_DIM)]
    if rope:
        cos_full, sin_signed = rope_tabs
        in_specs += [
            pl.BlockSpec((tq, HEAD_DIM), lambda b, kv, qi: (qi, 0)),
            pl.BlockSpec((tq, HEAD_DIM), lambda b, kv, qi: (qi, 0)),
            pl.BlockSpec((seq, HEAD_DIM), lambda b, kv, qi: (0, 0)),
            pl.BlockSpec((seq, HEAD_DIM), lambda b, kv, qi: (0, 0)),
        ]
        args += [cos_full, sin_signed, cos_full, sin_signed]
    if has_ctx:
        ck, cv = ctx_kv
        past = ck.shape[3]
        ctx_spec = pl.BlockSpec((None, None, None, past, HEAD_DIM), lambda b, kv, qi: (b, layer, kv, 0, 0))
        in_specs += [ctx_spec, ctx_spec]
        args += [ck, cv]

    out_specs = [pl.BlockSpec((tq, group_w), lambda b, kv, qi: (b * nq + qi, kv))]
    out_shape = [jax.ShapeDtypeStruct((rows, ATT_WIDTH), BF16)]
    if emit_cache:
        own_spec = pl.BlockSpec((None, None, seq, HEAD_DIM), lambda b, kv, qi: (b, kv, 0, 0))
        out_specs += [own_spec, own_spec]
        out_shape += [jax.ShapeDtypeStruct((n_batch, N_KV_HEADS, seq, HEAD_DIM), F32)] * 2

    return pl.pallas_call(
        functools.partial(_attn_kernel, rope=rope, has_ctx=has_ctx, emit_cache=emit_cache),
        grid=(n_batch, N_KV_HEADS, nq),
        in_specs=in_specs,
        out_specs=out_specs,
        out_shape=out_shape,
        scratch_shapes=[pltpu.VMEM((n_keys, HEAD_DIM), BF16)
                        for n_keys in [seq, seq] + ([past, past] if has_ctx else [])],
        compiler_params=_params(3),
        name="attention",
    )(*args)


def _retention_kernel(lg_ref, q_ref, k_ref, v_ref, g_ref, s0_ref, o_ref, sf_ref, decay_scr, *, seq):
    h = pl.program_id(0)
    lg_f = lg_ref[h]
    lg_b = lg_ref[N_RET_HEADS + h]

    @pl.when(pl.program_id(1) == 0)
    def _():
        i = lax.broadcasted_iota(jnp.int32, (seq, seq), 0)
        j = lax.broadcasted_iota(jnp.int32, (seq, seq), 1)
        d = (i - j).astype(F32)
        decay = jnp.exp(jnp.where(d >= 0, lg_f, lg_b) * jnp.abs(d))
        decay_scr[...] = jnp.where(d == 0, 2.0, decay)

    q = q_ref[...]
    k = k_ref[...] * (HEAD_DIM ** -0.5)
    vb = v_ref[...].astype(BF16)
    a = lax.dot_general(q.astype(BF16), k.astype(BF16), NT_DIMS, preferred_element_type=F32)
    o = _dot((a * decay_scr[...]).astype(BF16), vb)

    pos = lax.broadcasted_iota(jnp.int32, (seq, 1), 0).astype(F32)
    s0_f = s0_ref[0]
    s0_b = s0_ref[1]
    o = o + _dot((q * jnp.exp(lg_f * (pos + 1.0))).astype(BF16), s0_f.astype(BF16))
    o = o + _dot((q * jnp.exp(lg_b * (seq - pos))).astype(BF16), s0_b.astype(BF16))

    kf = (k * jnp.exp(lg_f * (seq - 1.0 - pos))).astype(BF16)
    kb = (k * jnp.exp(lg_b * pos)).astype(BF16)
    ones = jnp.ones((1, HEAD_DIM), F32)
    sf_ref[0] = jnp.exp(ones * (lg_f * seq)) * s0_f + lax.dot_general(kf, vb, TN_DIMS, preferred_element_type=F32)
    sf_ref[1] = jnp.exp(ones * (lg_b * seq)) * s0_b + lax.dot_general(kb, vb, TN_DIMS, preferred_element_type=F32)

    oc = o - jnp.mean(o, -1, keepdims=True)
    var = jnp.mean(oc * oc, -1, keepdims=True)
    o_ref[...] = (_silu(g_ref[...]) * (oc * lax.rsqrt(var + EPS))).astype(o_ref.dtype)


def _retention(proj, n_batch, seq, log_gamma, s0, layer):
    rows = proj.shape[0]

    def col(offset):
        return pl.BlockSpec((seq, HEAD_DIM), lambda h, b: (b, offset // HEAD_DIM + h))

    return pl.pallas_call(
        functools.partial(_retention_kernel, seq=seq),
        grid=(N_RET_HEADS, n_batch),
        in_specs=[
            pl.BlockSpec(memory_space=pltpu.SMEM),
            col(COL_RQ), col(COL_RK), col(COL_RV), col(COL_RG),
            pl.BlockSpec((None, None, 2, None, HEAD_DIM, HEAD_DIM), lambda h, b: (b, layer, 0, h, 0, 0)),
        ],
        out_specs=[
            pl.BlockSpec((seq, HEAD_DIM), lambda h, b: (b, h)),
            pl.BlockSpec((None, 2, None, HEAD_DIM, HEAD_DIM), lambda h, b: (b, 0, h, 0, 0)),
        ],
        out_shape=[
            jax.ShapeDtypeStruct((rows, RET_WIDTH), BF16),
            jax.ShapeDtypeStruct((n_batch, 2, N_RET_HEADS, HEAD_DIM, HEAD_DIM), F32),
        ],
        scratch_shapes=[pltpu.VMEM((seq, seq), F32)],
        compiler_params=_params(2),
        name="retention",
    )(log_gamma.reshape(2 * N_RET_HEADS), proj, proj, proj, proj, s0)


def _alt_sign(n):
    r = lax.broadcasted_iota(jnp.int32, (n, 1), 0)
    return jnp.where((r & 1) == 0, 1.0, -1.0).astype(F32)


def _filter_kernel(z_ref, w1_ref, b1_ref, fr_ref, w2_ref, b2_ref, w3f_ref, w3b_ref, dl_ref,
                   ch_ref, cl_ref, sh_ref, sl_ref, kr_ref, ki_ref, kn_ref, *, seq):
    freq = fr_ref[...]
    hdn = jnp.sin(freq * (_dot3(*_split(z_ref[...]), *_split(w1_ref[...])) + b1_ref[...]))
    hdn = jnp.sin(freq * (_dot3(*_split(hdn), *_split(w2_ref[...])) + b2_ref[...]))
    hdn_hi, hdn_lo = _split(hdn)

    pos = lax.broadcasted_iota(jnp.int32, (seq, 1), 0)
    t = pos.astype(F32) / max(seq - 1, 1)
    decay = jnp.exp(-t * dl_ref[...])
    h_fwd = _dot3(hdn_hi, hdn_lo, *_split(w3f_ref[...])) * decay
    h_bwd = _dot3(hdn_hi, hdn_lo, *_split(w3b_ref[...])) * decay

    h_bwd_shift = jnp.where(pos == 0, 0.0, _shift_rows(h_bwd, down=True))
    alt = _alt_sign(seq)
    kr = _dot3(ch_ref[...], cl_ref[...], *_split(h_fwd + h_bwd_shift)) + alt * h_bwd[seq - 1:seq, :]
    ki = _dot3(sh_ref[...], sl_ref[...], *_split(h_bwd_shift - h_fwd))
    wgt = jnp.where(pos == 0, 1.0, 2.0) * (0.5 / seq)
    kr_ref[...] = kr * wgt
    ki_ref[...] = ki * wgt
    kn_ref[...] = jnp.sum(alt * (h_fwd - h_bwd), axis=0, keepdims=True) * (0.5 / seq)


def _hyena_filters(seq, zfeat, w1, b1, freq, w2, b2, w3, abs_deltas, dft):
    w = HYENA_WIDTH
    const = lambda o: (0, 0)
    once = pl.Buffered(1)
    mat = pl.BlockSpec((seq, seq), const, pipeline_mode=once)
    vec = pl.BlockSpec((1, FILTER_PAD), const)
    return pl.pallas_call(
        functools.partial(_filter_kernel, seq=seq),
        grid=(HYENA_ORDER,),
        in_specs=[
            pl.BlockSpec((seq, FILTER_PAD), const),
            pl.BlockSpec((FILTER_PAD, FILTER_PAD), const), vec, vec,
            pl.BlockSpec((FILTER_PAD, FILTER_PAD), const), vec,
            pl.BlockSpec((FILTER_PAD, w), lambda o: (0, 2 * o)),
            pl.BlockSpec((FILTER_PAD, w), lambda o: (0, 2 * o + 1)),
            pl.BlockSpec((1, w), const),
            mat, mat, mat, mat,
        ],
        out_specs=[
            pl.BlockSpec((None, seq, w), lambda o: (o, 0, 0)),
            pl.BlockSpec((None, seq, w), lambda o: (o, 0, 0)),
            pl.BlockSpec((None, 1, w), lambda o: (o, 0, 0)),
        ],
        out_shape=[
            jax.ShapeDtypeStruct((HYENA_ORDER, seq, HYENA_WIDTH), F32),
            jax.ShapeDtypeStruct((HYENA_ORDER, seq, HYENA_WIDTH), F32),
            jax.ShapeDtypeStruct((HYENA_ORDER, 1, HYENA_WIDTH), F32),
        ],
        compiler_params=_params(1),
        name="hyena_filters",
    )(zfeat, w1, b1, freq, w2, b2, w3, w3, abs_deltas, *dft)


def _dwconv3_seq(x, w, pos, seq):
    prev = jnp.where(pos == 0, 0.0, _shift_rows(x, down=True))
    nxt = jnp.where(pos == seq - 1, 0.0, _shift_rows(x, down=False))
    return prev * w[0:1] + x * w[1:2] + nxt * w[2:3]


def _hyena_kernel(v_ref, x1_ref, x2_ref, cv_ref, c1_ref, c2_ref, bias_ref, kr_ref, ki_ref, kn_ref,
                  ch_ref, cl_ref, sh_ref, sl_ref, o_ref, *, seq):
    pos = lax.broadcasted_iota(jnp.int32, (seq, 1), 0)
    alt = _alt_sign(seq)

    def dft(hi_ref, lo_ref, a):
        a_hi, a_lo = _split(a)
        return _dot(hi_ref[...], a_hi) + _dot(lo_ref[...], a_hi) + _dot(hi_ref[...], a_lo)

    z = _dwconv3_seq(v_ref[...], cv_ref[...], pos, seq)
    for o, (x_ref, c_ref) in enumerate(((x1_ref, c1_ref), (x2_ref, c2_ref))):
        ur = dft(ch_ref, cl_ref, z)
        ui = dft(sh_ref, sl_ref, z)
        u_nyq = jnp.sum(alt * z, axis=0, keepdims=True)
        kr, ki = kr_ref[o], ki_ref[o]
        yr = ur * kr + ui * ki
        yi = ui * kr - ur * ki
        y = dft(ch_ref, cl_ref, yr) + dft(sh_ref, sl_ref, yi) + alt * (u_nyq * kn_ref[o])
        z = _dwconv3_seq(x_ref[...], c_ref[...], pos, seq) * (y + z * bias_ref[o:o + 1, :])
    o_ref[...] = z.astype(o_ref.dtype)


def _hyena(proj, n_batch, seq, hy_conv, hy_bias, spectra, dft):
    rows = proj.shape[0]
    w = HYENA_WIDTH
    kr, ki, kn = spectra
    once = pl.Buffered(1)

    def part(p):
        return pl.BlockSpec((seq, w), lambda b: (b, COL_HY // w + p))

    def conv(p):
        return pl.BlockSpec((3, w), lambda b: (0, p), pipeline_mode=once)

    mat = pl.BlockSpec((seq, seq), lambda b: (0, 0), pipeline_mode=once)
    spec = pl.BlockSpec((HYENA_ORDER, seq, w), lambda b: (0, 0, 0), pipeline_mode=once)
    return pl.pallas_call(
        functools.partial(_hyena_kernel, seq=seq),
        grid=(n_batch,),
        in_specs=[
            part(0), part(1), part(2), conv(0), conv(1), conv(2),
            pl.BlockSpec((HYENA_ORDER, w), lambda b: (0, 0), pipeline_mode=once),
            spec, spec,
            pl.BlockSpec((HYENA_ORDER, 1, w), lambda b: (0, 0, 0), pipeline_mode=once),
            mat, mat, mat, mat,
        ],
        out_specs=pl.BlockSpec((seq, w), lambda b: (b, 0)),
        out_shape=jax.ShapeDtypeStruct((rows, w), BF16),
        compiler_params=_params(1),
        name="hyena",
    )(proj, proj, proj, hy_conv, hy_conv, hy_conv, hy_bias, kr, ki, kn, *dft)


def _outproj_kernel(att_ref, ret_ref, hy_ref, x_ref, g_ref, w_ref, lw_ref, lb_ref, o_ref):
    r0, r1 = ATT_WIDTH, ATT_WIDTH + RET_WIDTH
    for t0 in range(0, x_ref.shape[0], OUTPROJ_SUB):
        rows = slice(t0, t0 + OUTPROJ_SUB)
        mix = (_dot(att_ref[rows, :], w_ref[0:r0, :]) + _dot(ret_ref[rows, :], w_ref[r0:r1, :])
               + _dot(hy_ref[rows, :], w_ref[r1:, :]))
        y = DEEPNORM_ALPHA * x_ref[rows, :] + g_ref[...] * mix
        o_ref[rows, :] = _layer_norm(y, lw_ref[...], lb_ref[...])


def _outproj(att, ret, hyz, x, mod, w_out, layer, ln_w, ln_b):
    rows = x.shape[0]
    tm = OUTPROJ_TM
    tiles_per_mod = rows // mod.shape[0] // tm
    row = lambda width: pl.BlockSpec((tm, width), lambda i: (i, 0))
    vec = pl.BlockSpec((1, D_MODEL), lambda i: (0, 0))
    return pl.pallas_call(
        _outproj_kernel,
        grid=(rows // tm,),
        in_specs=[
            row(ATT_WIDTH), row(RET_WIDTH), row(HYENA_WIDTH), row(D_MODEL),
            pl.BlockSpec((None, 1, D_MODEL), lambda i: (i // tiles_per_mod, 0, 2)),
            pl.BlockSpec((None, D_MODEL, D_MODEL), lambda i: (layer, 0, 0), pipeline_mode=pl.Buffered(1)),
            vec, vec,
        ],
        out_specs=row(D_MODEL),
        out_shape=jax.ShapeDtypeStruct((rows, D_MODEL), F32),
        compiler_params=_params(1),
        name="outproj_ln",
    )(att, ret, hyz, x, mod, w_out, ln_w.reshape(1, D_MODEL), ln_b.reshape(1, D_MODEL))


def _ffn_kernel(x_ref, xp_ref, xn_ref, sh_ref, sc_ref, g_ref, wg_ref, wu_ref, cg_ref, cu_ref, wd_ref,
                lw_ref, lb_ref, o_ref, h_scr, acc_scr, *, seq, tm):
    i = pl.program_id(0)
    j = pl.program_id(1)

    @pl.when(j == 0)
    def _():
        sc = 1.0 + sc_ref[...]
        sh = sh_ref[...]
        h_scr[0:HALO, :] = (xp_ref[...] * sc + sh).astype(BF16)
        h_scr[HALO:HALO + tm, :] = (x_ref[...] * sc + sh).astype(BF16)
        h_scr[HALO + tm:, :] = (xn_ref[...] * sc + sh).astype(BF16)
        acc_scr[...] = jnp.zeros_like(acc_scr)

    pos = (i * tm + lax.broadcasted_iota(jnp.int32, (tm, 1), 0)) & (seq - 1)

    def up_conv(w_ref, c_ref):
        up = _dot(h_scr[...], w_ref[...])
        prev = _shift_rows(up, down=True)[HALO:HALO + tm]
        nxt = _shift_rows(up, down=False)[HALO:HALO + tm]
        c = c_ref[...]
        return (jnp.where(pos == 0, 0.0, prev) * c[0:1] + up[HALO:HALO + tm] * c[1:2]
                + jnp.where(pos == seq - 1, 0.0, nxt) * c[2:3])

    act = _silu(up_conv(wg_ref, cg_ref)) * up_conv(wu_ref, cu_ref)
    acc_scr[...] += _dot(act.astype(BF16), wd_ref[...])

    @pl.when(j == pl.num_programs(1) - 1)
    def _():
        y = DEEPNORM_ALPHA * x_ref[...] + g_ref[...] * acc_scr[...]
        o_ref[...] = _layer_norm(y, lw_ref[...], lb_ref[...])


def _ffn(x, seq, mod, w_up, ffn_conv, w_down, layer, ln_w, ln_b):
    rows = x.shape[0]
    tm, tf = FFN_TM, FFN_TF
    n_f = D_FF // tf
    tiles_per_mod = rows // mod.shape[0] // tm
    halo_per_tile = tm // HALO
    last_halo = rows // HALO - 1

    def mod_spec(chunk):
        return pl.BlockSpec((None, 1, D_MODEL), lambda i, j: (i // tiles_per_mod, 0, chunk))

    vec = pl.BlockSpec((1, D_MODEL), lambda i, j: (0, 0))
    return pl.pallas_call(
        functools.partial(_ffn_kernel, seq=seq, tm=tm),
        grid=(rows // tm, n_f),
        in_specs=[
            pl.BlockSpec((tm, D_MODEL), lambda i, j: (i, 0)),
            pl.BlockSpec((HALO, D_MODEL), lambda i, j: (jnp.maximum(i * halo_per_tile - 1, 0), 0)),
            pl.BlockSpec((HALO, D_MODEL), lambda i, j: (jnp.minimum((i + 1) * halo_per_tile, last_halo), 0)),
            mod_spec(3), mod_spec(4), mod_spec(5),
            pl.BlockSpec((None, D_MODEL, tf), lambda i, j: (layer, 0, j)),
            pl.BlockSpec((None, D_MODEL, tf), lambda i, j: (layer, 0, n_f + j)),
            pl.BlockSpec((None, 3, tf), lambda i, j: (layer, 0, j)),
            pl.BlockSpec((None, 3, tf), lambda i, j: (layer, 0, n_f + j)),
            pl.BlockSpec((None, tf, D_MODEL), lambda i, j: (layer, j, 0)),
            vec, vec,
        ],
        out_specs=pl.BlockSpec((tm, D_MODEL), lambda i, j: (i, 0)),
        out_shape=jax.ShapeDtypeStruct((rows, D_MODEL), F32),
        scratch_shapes=[pltpu.VMEM((tm + 2 * HALO, D_MODEL), BF16), pltpu.VMEM((tm, D_MODEL), F32)],
        compiler_params=_params(2),
        name="ffn_ln",
    )(x, x, x, mod, mod, mod, w_up, w_up, ffn_conv, ffn_conv, w_down,
      ln_w.reshape(1, D_MODEL), ln_b.reshape(1, D_MODEL))


def _rope_tables(n_tokens):
    rows = n_tokens // GRID_W
    row = jnp.repeat(jnp.arange(rows, dtype=F32), GRID_W)
    col = jnp.tile(jnp.arange(GRID_W, dtype=F32), rows)
    n_freq = HEAD_DIM // 4
    inv_freq = ROPE_THETA ** (-jnp.arange(n_freq, dtype=F32) / n_freq)
    ang = jnp.concatenate([row[:, None] * inv_freq[None], col[:, None] * inv_freq[None]], -1)
    cos, sin = jnp.cos(ang), jnp.sin(ang)
    return jnp.concatenate([cos, cos], -1), jnp.concatenate([-sin, sin], -1)


def _dft_tables(seq):
    idx = jnp.arange(seq, dtype=jnp.int32)
    kn = (idx[:, None] * idx[None, :]) % (2 * seq)
    ang = kn.astype(F32) * (math.pi / seq)
    return (*_split(jnp.cos(ang)), *_split(jnp.sin(ang)))


def _filter_features(seq):
    pos = jnp.arange(seq, dtype=F32)
    t = pos / max(seq - 1, 1)
    w = 2.0 * math.pi * pos / seq
    f = jnp.linspace(1e-4, FILTER_BANDS - 1, FILTER_BANDS, dtype=F32)
    ang = w[:, None] * f[None, :]
    z = jnp.concatenate([t[:, None], jnp.cos(ang), -jnp.sin(ang)], -1)
    return jnp.pad(z, ((0, 0), (0, FILTER_PAD - FILTER_EMB)))


def _pad_to(a, shape):
    return jnp.pad(a, [(0, s - d) for d, s in zip(a.shape, shape)])


def kernel(x_prompt, x_sample, cache_k, cache_v, state_ret, c, c_ctx, w_ada, b_ada, w_in, q_norm, k_norm, ret_decay, hy_conv, hf_w1, hf_b1, hf_freq, hf_w2, hf_b2, hf_w3, hy_bias, w_out, ln1_w, ln1_b, w_up, ffn_conv, w_down, ln2_w, ln2_b):
    n_ctx, seq_ctx, _ = x_prompt.shape
    n_lat, seq_lat, _ = x_sample.shape

    cond = _pad_to(jnp.concatenate([c_ctx[None, :], c], 0), (COND_ROWS, D_MODEL))
    mod_all = _ada(cond, w_ada, b_ada)

    w_in_b, w_out_b = w_in.astype(BF16), w_out.astype(BF16)
    w_up_b, w_down_b = w_up.astype(BF16), w_down.astype(BF16)
    log_gamma = jax.nn.log_sigmoid(ret_decay.astype(F32))
    abs_deltas = jnp.abs(jnp.linspace(math.log(HYENA_DECAY_TARGET) / HYENA_DECAY_PCT_MIN,
                                      math.log(HYENA_DECAY_TARGET) / HYENA_DECAY_PCT_MAX,
                                      HYENA_WIDTH, dtype=F32))[None, :]
    ret_zero = jnp.zeros((n_ctx, 1, 2, N_RET_HEADS, HEAD_DIM, HEAD_DIM), F32)

    groups = {
        "ctx": dict(n=n_ctx, seq=seq_ctx, rope=None, dft=_dft_tables(seq_ctx), feat=_filter_features(seq_ctx)),
        "lat": dict(n=n_lat, seq=seq_lat, rope=_rope_tables(seq_lat), dft=_dft_tables(seq_lat),
                    feat=_filter_features(seq_lat)),
    }
    xs = {"ctx": x_prompt.reshape(n_ctx * seq_ctx, D_MODEL), "lat": x_sample.reshape(n_lat * seq_lat, D_MODEL)}
    ks_out, vs_out, ss_out = [], [], []

    for l in range(DEPTH):
        mods = {"ctx": mod_all[l, 0:1, None, :], "lat": mod_all[l, 1:1 + n_lat, None, :]}
        pad2 = (FILTER_PAD, FILTER_PAD)
        filt_w = (_pad_to(hf_w1[l], pad2), _pad_to(hf_b1[l][None, :], (1, FILTER_PAD)),
                  _pad_to(hf_freq[l][None, :], (1, FILTER_PAD)), _pad_to(hf_w2[l], pad2),
                  _pad_to(hf_b2[l][None, :], (1, FILTER_PAD)), _pad_to(hf_w3[l], (FILTER_PAD, hf_w3.shape[-1])))
        for name, g in groups.items():
            n, seq, x, mod = g["n"], g["seq"], xs[name], mods[name]
            is_ctx = name == "ctx"
            proj = _inproj(x, mod, w_in_b, l)
            att_out = _attention(proj, n, seq, q_norm[l], k_norm[l], g["rope"],
                                 None if is_ctx else (cache_k, cache_v), l, emit_cache=is_ctx)
            ret, s_fin = _retention(proj, n, seq, log_gamma[l],
                                    ret_zero if is_ctx else state_ret, 0 if is_ctx else l)
            spectra = _hyena_filters(seq, g["feat"], *filt_w, abs_deltas, g["dft"])
            hyz = _hyena(proj, n, seq, hy_conv[l], hy_bias[l], spectra, g["dft"])
            if is_ctx:
                att, own_k, own_v = att_out
                ks_out.append(own_k)
                vs_out.append(own_v)
                ss_out.append(s_fin)
            else:
                att = att_out[0]
            x = _outproj(att, ret, hyz, x, mod, w_out_b, l, ln1_w[l], ln1_b[l])
            xs[name] = _ffn(x, seq, mod, w_up_b, ffn_conv, w_down_b, l, ln2_w[l], ln2_b[l])

    return (xs["ctx"].reshape(n_ctx, seq_ctx, D_MODEL), xs["lat"].reshape(n_lat, seq_lat, D_MODEL),
            jnp.stack(ks_out, axis=1), jnp.stack(vs_out, axis=1), jnp.stack(ss_out, axis=1))
```

```python
import functools
import math

import jax
import jax.numpy as jnp
from jax import lax
from jax.experimental import pallas as pl
from jax.experimental.pallas import tpu as pltpu

F32 = jnp.float32
BF16 = jnp.bfloat16

D_MODEL = 2048
DEPTH = 2
GRID_W = 64
HEAD_DIM = 128
ATT_WIDTH = D_MODEL // 2
N_ATT_HEADS = ATT_WIDTH // HEAD_DIM
N_KV_HEADS = N_ATT_HEADS // 4
GQA_GROUP = N_ATT_HEADS // N_KV_HEADS
KV_WIDTH = N_KV_HEADS * HEAD_DIM
RET_WIDTH = D_MODEL // 4
N_RET_HEADS = RET_WIDTH // HEAD_DIM
HYENA_WIDTH = D_MODEL // 4
HYENA_ORDER = 2
IN_WIDTH = ATT_WIDTH + 2 * KV_WIDTH + 4 * RET_WIDTH + (HYENA_ORDER + 1) * HYENA_WIDTH
D_FF = 5632
ROPE_THETA = 10000.0
FILTER_BANDS = 16
FILTER_EMB = 1 + 2 * FILTER_BANDS
FILTER_HID = 64
HYENA_DECAY_TARGET = 1e-2
HYENA_DECAY_PCT_MIN = 0.3
HYENA_DECAY_PCT_MAX = 1.5
DEEPNORM_ALPHA = (2 * DEPTH) ** 0.25
EPS = 1e-6

COL_Q = 0
COL_K = ATT_WIDTH
COL_V = COL_K + KV_WIDTH
COL_RQ = COL_V + KV_WIDTH
COL_RK = COL_RQ + RET_WIDTH
COL_RV = COL_RK + RET_WIDTH
COL_RG = COL_RV + RET_WIDTH
COL_HY = COL_RG + RET_WIDTH

LANES = 128
HALO = 16
VMEM_LIMIT = 56 * 1024 * 1024
COND_ROWS = 16

ADA_TN = 1024
INPROJ_TM, INPROJ_TN = 256, 512
OUTPROJ_TM, OUTPROJ_SUB = 512, 256
FFN_TM, FFN_TF = 512, 512
ATT_TQ = 256
FILTER_PAD = 128

NT_DIMS = (((1,), (1,)), ((), ()))
TN_DIMS = (((0,), (0,)), ((), ()))


def _params(n_grid):
    return pltpu.CompilerParams(dimension_semantics=("arbitrary",) * n_grid,
                                vmem_limit_bytes=VMEM_LIMIT)


def _dot(a, b):
    return jnp.dot(a, b, preferred_element_type=F32)


def _split(a):
    hi = a.astype(BF16)
    lo = (a - hi.astype(F32)).astype(BF16)
    return hi, lo


def _dot3(a_hi, a_lo, b_hi, b_lo):
    return _dot(a_hi, b_hi) + _dot(a_lo, b_hi) + _dot(a_hi, b_lo)


def _silu(x):
    return x * jax.nn.sigmoid(x)


def _layer_norm(y, w, b):
    yc = y - jnp.mean(y, -1, keepdims=True)
    var = jnp.mean(yc * yc, -1, keepdims=True)
    return yc * lax.rsqrt(var + EPS) * w + b


def _shift_rows(x, down):
    n = x.shape[0]
    return pltpu.roll(x, 1 if down else n - 1, axis=0)


def _ada_kernel(cond_ref, w_ref, b_ref, o_ref):
    a = _silu(cond_ref[...]).astype(BF16)
    o_ref[...] = _dot(a, w_ref[...].astype(BF16)) + b_ref[...]


def _ada(cond, w_ada, b_ada):
    n_out = w_ada.shape[-1]
    return pl.pallas_call(
        _ada_kernel,
        grid=(DEPTH, n_out // ADA_TN),
        in_specs=[
            pl.BlockSpec((COND_ROWS, D_MODEL), lambda l, j: (0, 0)),
            pl.BlockSpec((None, D_MODEL, ADA_TN), lambda l, j: (l, 0, j)),
            pl.BlockSpec((None, 1, ADA_TN), lambda l, j: (l, 0, j)),
        ],
        out_specs=pl.BlockSpec((None, COND_ROWS, ADA_TN), lambda l, j: (l, 0, j)),
        out_shape=jax.ShapeDtypeStruct((DEPTH, COND_ROWS, n_out), F32),
        compiler_params=_params(2),
        name="ada",
    )(cond, w_ada, b_ada.reshape(DEPTH, 1, n_out))


def _inproj_kernel(x_ref, sh_ref, sc_ref, w_ref, o_ref):
    h = (x_ref[...] * (1.0 + sc_ref[...]) + sh_ref[...]).astype(BF16)
    for n0 in range(0, IN_WIDTH, INPROJ_TN):
        o_ref[:, n0:n0 + INPROJ_TN] = _dot(h, w_ref[:, n0:n0 + INPROJ_TN])


def _inproj(x, mod, w_in, layer):
    rows = x.shape[0]
    tm = INPROJ_TM
    tiles_per_mod = rows // mod.shape[0] // tm
    return pl.pallas_call(
        _inproj_kernel,
        grid=(rows // tm,),
        in_specs=[
            pl.BlockSpec((tm, D_MODEL), lambda i: (i, 0)),
            pl.BlockSpec((None, 1, D_MODEL), lambda i: (i // tiles_per_mod, 0, 0)),
            pl.BlockSpec((None, 1, D_MODEL), lambda i: (i // tiles_per_mod, 0, 1)),
            pl.BlockSpec((None, D_MODEL, IN_WIDTH), lambda i: (layer, 0, 0), pipeline_mode=pl.Buffered(1)),
        ],
        out_specs=pl.BlockSpec((tm, IN_WIDTH), lambda i: (i, 0)),
        out_shape=jax.ShapeDtypeStruct((rows, IN_WIDTH), F32),
        compiler_params=_params(1),
        name="inproj",
    )(x, mod, mod, w_in)


def _rms_norm(x, w):
    return x * lax.rsqrt(jnp.mean(x * x, -1, keepdims=True) + EPS) * w


def _rope(x, cos_full, sin_signed):
    return x * cos_full + pltpu.roll(x, HEAD_DIM // 2, axis=1) * sin_signed


def _attn_kernel(*refs, rope, has_ctx, emit_cache):
    refs = list(refs)
    q_ref, k_ref, v_ref, qw_ref, kw_ref = refs[:5]
    del refs[:5]
    if rope:
        qcos_ref, qsin_ref, kcos_ref, ksin_ref = refs[:4]
        del refs[:4]
    if has_ctx:
        ck_ref, cv_ref = refs[:2]
        del refs[:2]
    o_ref = refs.pop(0)
    if emit_cache:
        ownk_ref, ownv_ref = refs[:2]
        del refs[:2]
    k_scr, v_scr = refs
    seq = k_ref.shape[0]

    @pl.when(pl.program_id(2) == 0)
    def _():
        kn = _rms_norm(k_ref[...], kw_ref[...])
        if emit_cache:
            ownk_ref[...] = kn
            ownv_ref[...] = v_ref[...]
        if rope:
            kn = _rope(kn, kcos_ref[...], ksin_ref[...])
        k_scr[0:seq, :] = kn.astype(BF16)
        v_scr[0:seq, :] = v_ref[...].astype(BF16)
        if has_ctx:
            k_scr[seq:, :] = ck_ref[...].astype(BF16)
            v_scr[seq:, :] = cv_ref[...].astype(BF16)

    for g in range(GQA_GROUP):
        qn = _rms_norm(q_ref[:, g * HEAD_DIM:(g + 1) * HEAD_DIM], qw_ref[...])
        if rope:
            qn = _rope(qn, qcos_ref[...], qsin_ref[...])
        qb = (qn * (HEAD_DIM ** -0.5)).astype(BF16)
        s = lax.dot_general(qb, k_scr[...], NT_DIMS, preferred_element_type=F32)
        p = jnp.exp(s - jnp.max(s, -1, keepdims=True))
        den = jnp.sum(p, -1, keepdims=True)
        o = _dot(p.astype(BF16), v_scr[...])
        o_ref[:, g * HEAD_DIM:(g + 1) * HEAD_DIM] = (o / den).astype(o_ref.dtype)


def _attention(proj, n_batch, seq, q_norm, k_norm, rope_tabs, ctx_kv, layer, emit_cache):
    rows = proj.shape[0]
    tq = min(ATT_TQ, seq)
    nq = seq // tq
    group_w = GQA_GROUP * HEAD_DIM
    rope = rope_tabs is not None
    has_ctx = ctx_kv is not None

    in_specs = [
        pl.BlockSpec((tq, group_w), lambda b, kv, qi: (b * nq + qi, COL_Q // group_w + kv)),
        pl.BlockSpec((seq, HEAD_DIM), lambda b, kv, qi: (b, COL_K // HEAD_DIM + kv)),
        pl.BlockSpec((seq, HEAD_DIM), lambda b, kv, qi: (b, COL_V // HEAD_DIM + kv)),
        pl.BlockSpec((1, HEAD_DIM), lambda b, kv, qi: (0, 0)),
        pl.BlockSpec((1, HEAD_DIM), lambda b, kv, qi: (0, 0)),
    ]
    args = [proj, proj, proj, q_norm.reshape(1, HEAD_DIM), k_norm.reshape(1, HEAD_DIM)]
    if rope:
        cos_full, sin_signed = rope_tabs
        in_specs += [
            pl.BlockSpec((tq, HEAD_DIM), lambda b, kv, qi: (qi, 0)),
            pl.BlockSpec((tq, HEAD_DIM), lambda b, kv, qi: (qi, 0)),
            pl.BlockSpec((seq, HEAD_DIM), lambda b, kv, qi: (0, 0)),
            pl.BlockSpec((seq, HEAD_DIM), lambda b, kv, qi: (0, 0)),
        ]
        args += [cos_full, sin_signed, cos_full, sin_signed]
    if has_ctx:
        ck, cv = ctx_kv
        past = ck.shape[3]
        ctx_spec = pl.BlockSpec((None, None, None, past, HEAD_DIM), lambda b, kv, qi: (b, layer, kv, 0, 0))
        in_specs += [ctx_spec, ctx_spec]
        args += [ck, cv]

    out_specs = [pl.BlockSpec((tq, group_w), lambda b, kv, qi: (b * nq + qi, kv))]
    out_shape = [jax.ShapeDtypeStruct((rows, ATT_WIDTH), BF16)]
    if emit_cache:
        own_spec = pl.BlockSpec((None, None, seq, HEAD_DIM), lambda b, kv, qi: (b, kv, 0, 0))
        out_specs += [own_spec, own_spec]
        out_shape += [jax.ShapeDtypeStruct((n_batch, N_KV_HEADS, seq, HEAD_DIM), F32)] * 2

    return pl.pallas_call(
        functools.partial(_attn_kernel, rope=rope, has_ctx=has_ctx, emit_cache=emit_cache),
        grid=(n_batch, N_KV_HEADS, nq),
        in_specs=in_specs,
        out_specs=out_specs,
        out_shape=out_shape,
        scratch_shapes=[pltpu.VMEM((seq + (past if has_ctx else 0), HEAD_DIM), BF16)] * 2,
        compiler_params=_params(3),
        name="attention",
    )(*args)


def _retention_kernel(lg_ref, q_ref, k_ref, v_ref, g_ref, s0_ref, o_ref, sf_ref, decay_scr, *, seq):
    h = pl.program_id(0)
    lg_f = lg_ref[h]
    lg_b = lg_ref[N_RET_HEADS + h]

    @pl.when(pl.program_id(1) == 0)
    def _():
        i = lax.broadcasted_iota(jnp.int32, (seq, seq), 0)
        j = lax.broadcasted_iota(jnp.int32, (seq, seq), 1)
        d = (i - j).astype(F32)
        decay = jnp.exp(jnp.where(d >= 0, lg_f, lg_b) * jnp.abs(d))
        decay_scr[...] = jnp.where(d == 0, 2.0, decay)

    q = q_ref[...]
    k = k_ref[...] * (HEAD_DIM ** -0.5)
    vb = v_ref[...].astype(BF16)
    a = lax.dot_general(q.astype(BF16), k.astype(BF16), NT_DIMS, preferred_element_type=F32)
    o = _dot((a * decay_scr[...]).astype(BF16), vb)

    pos = lax.broadcasted_iota(jnp.int32, (seq, 1), 0).astype(F32)
    s0_f = s0_ref[0]
    s0_b = s0_ref[1]
    o = o + _dot((q * jnp.exp(lg_f * (pos + 1.0))).astype(BF16), s0_f.astype(BF16))
    o = o + _dot((q * jnp.exp(lg_b * (seq - pos))).astype(BF16), s0_b.astype(BF16))

    kf = (k * jnp.exp(lg_f * (seq - 1.0 - pos))).astype(BF16)
    kb = (k * jnp.exp(lg_b * pos)).astype(BF16)
    ones = jnp.ones((1, HEAD_DIM), F32)
    sf_ref[0] = jnp.exp(ones * (lg_f * seq)) * s0_f + lax.dot_general(kf, vb, TN_DIMS, preferred_element_type=F32)
    sf_ref[1] = jnp.exp(ones * (lg_b * seq)) * s0_b + lax.dot_general(kb, vb, TN_DIMS, preferred_element_type=F32)

    oc = o - jnp.mean(o, -1, keepdims=True)
    var = jnp.mean(oc * oc, -1, keepdims=True)
    o_ref[...] = (_silu(g_ref[...]) * (oc * lax.rsqrt(var + EPS))).astype(o_ref.dtype)


def _retention(proj, n_batch, seq, log_gamma, s0, layer):
    rows = proj.shape[0]

    def col(offset):
        return pl.BlockSpec((seq, HEAD_DIM), lambda h, b: (b, offset // HEAD_DIM + h))

    return pl.pallas_call(
        functools.partial(_retention_kernel, seq=seq),
        grid=(N_RET_HEADS, n_batch),
        in_specs=[
            pl.BlockSpec(memory_space=pltpu.SMEM),
            col(COL_RQ), col(COL_RK), col(COL_RV), col(COL_RG),
            pl.BlockSpec((None, None, 2, None, HEAD_DIM, HEAD_DIM), lambda h, b: (b, layer, 0, h, 0, 0)),
        ],
        out_specs=[
            pl.BlockSpec((seq, HEAD_DIM), lambda h, b: (b, h)),
            pl.BlockSpec((None, 2, None, HEAD_DIM, HEAD_DIM), lambda h, b: (b, 0, h, 0, 0)),
        ],
        out_shape=[
            jax.ShapeDtypeStruct((rows, RET_WIDTH), BF16),
            jax.ShapeDtypeStruct((n_batch, 2, N_RET_HEADS, HEAD_DIM, HEAD_DIM), F32),
        ],
        scratch_shapes=[pltpu.VMEM((seq, seq), F32)],
        compiler_params=_params(2),
        name="retention",
    )(log_gamma.reshape(2 * N_RET_HEADS), proj, proj, proj, proj, s0)


def _alt_sign(n):
    r = lax.broadcasted_iota(jnp.int32, (n, 1), 0)
    return jnp.where((r & 1) == 0, 1.0, -1.0).astype(F32)


def _filter_kernel(z_ref, w1_ref, b1_ref, fr_ref, w2_ref, b2_ref, w3f_ref, w3b_ref, dl_ref,
                   ch_ref, cl_ref, sh_ref, sl_ref, kr_ref, ki_ref, kn_ref, *, seq):
    freq = fr_ref[...]
    hdn = jnp.sin(freq * (_dot3(*_split(z_ref[...]), *_split(w1_ref[...])) + b1_ref[...]))
    hdn = jnp.sin(freq * (_dot3(*_split(hdn), *_split(w2_ref[...])) + b2_ref[...]))
    hdn_hi, hdn_lo = _split(hdn)

    pos = lax.broadcasted_iota(jnp.int32, (seq, 1), 0)
    t = pos.astype(F32) / max(seq - 1, 1)
    decay = jnp.exp(-t * dl_ref[...])
    h_fwd = _dot3(hdn_hi, hdn_lo, *_split(w3f_ref[...])) * decay
    h_bwd = _dot3(hdn_hi, hdn_lo, *_split(w3b_ref[...])) * decay

    h_bwd_shift = jnp.where(pos == 0, 0.0, _shift_rows(h_bwd, down=True))
    alt = _alt_sign(seq)
    kr = _dot3(ch_ref[...], cl_ref[...], *_split(h_fwd + h_bwd_shift)) + alt * h_bwd[seq - 1:seq, :]
    ki = _dot3(sh_ref[...], sl_ref[...], *_split(h_bwd_shift - h_fwd))
    wgt = jnp.where(pos == 0, 1.0, 2.0) * (0.5 / seq)
    kr_ref[...] = kr * wgt
    ki_ref[...] = ki * wgt
    kn_ref[...] = jnp.sum(alt * (h_fwd - h_bwd), axis=0, keepdims=True) * (0.5 / seq)


def _hyena_filters(seq, zfeat, w1, b1, freq, w2, b2, w3, abs_deltas, dft):
    w = HYENA_WIDTH
    const = lambda o: (0, 0)
    once = pl.Buffered(1)
    mat = pl.BlockSpec((seq, seq), const, pipeline_mode=once)
    vec = pl.BlockSpec((1, FILTER_PAD), const)
    return pl.pallas_call(
        functools.partial(_filter_kernel, seq=seq),
        grid=(HYENA_ORDER,),
        in_specs=[
            pl.BlockSpec((seq, FILTER_PAD), const),
            pl.BlockSpec((FILTER_PAD, FILTER_PAD), const), vec, vec,
            pl.BlockSpec((FILTER_PAD, FILTER_PAD), const), vec,
            pl.BlockSpec((FILTER_PAD, w), lambda o: (0, 2 * o)),
            pl.BlockSpec((FILTER_PAD, w), lambda o: (0, 2 * o + 1)),
            pl.BlockSpec((1, w), const),
            mat, mat, mat, mat,
        ],
        out_specs=[
            pl.BlockSpec((None, seq, w), lambda o: (o, 0, 0)),
            pl.BlockSpec((None, seq, w), lambda o: (o, 0, 0)),
            pl.BlockSpec((None, 1, w), lambda o: (o, 0, 0)),
        ],
        out_shape=[
            jax.ShapeDtypeStruct((HYENA_ORDER, seq, HYENA_WIDTH), F32),
            jax.ShapeDtypeStruct((HYENA_ORDER, seq, HYENA_WIDTH), F32),
            jax.ShapeDtypeStruct((HYENA_ORDER, 1, HYENA_WIDTH), F32),
        ],
        compiler_params=_params(1),
        name="hyena_filters",
    )(zfeat, w1, b1, freq, w2, b2, w3, w3, abs_deltas, *dft)


def _dwconv3_seq(x, w, pos, seq):
    prev = jnp.where(pos == 0, 0.0, _shift_rows(x, down=True))
    nxt = jnp.where(pos == seq - 1, 0.0, _shift_rows(x, down=False))
    return prev * w[0:1] + x * w[1:2] + nxt * w[2:3]


def _hyena_kernel(v_ref, x1_ref, x2_ref, cv_ref, c1_ref, c2_ref, bias_ref, kr_ref, ki_ref, kn_ref,
                  cos_ref, sin_ref, o_ref, *, seq):
    pos = lax.broadcasted_iota(jnp.int32, (seq, 1), 0)
    alt = _alt_sign(seq)

    def dft(m_ref, a):
        return _dot(m_ref[...], a.astype(BF16))

    z = _dwconv3_seq(v_ref[...], cv_ref[...], pos, seq)
    for o, (x_ref, c_ref) in enumerate(((x1_ref, c1_ref), (x2_ref, c2_ref))):
        ur = dft(cos_ref, z)
        ui = dft(sin_ref, z)
        u_nyq = jnp.sum(alt * z, axis=0, keepdims=True)
        kr, ki = kr_ref[o], ki_ref[o]
        yr = ur * kr + ui * ki
        yi = ui * kr - ur * ki
        y = dft(cos_ref, yr) + dft(sin_ref, yi) + alt * (u_nyq * kn_ref[o])
        z = _dwconv3_seq(x_ref[...], c_ref[...], pos, seq) * (y + z * bias_ref[o:o + 1, :])
    o_ref[...] = z.astype(o_ref.dtype)


def _hyena(proj, n_batch, seq, hy_conv, hy_bias, spectra, dft):
    rows = proj.shape[0]
    w = HYENA_WIDTH
    kr, ki, kn = spectra
    once = pl.Buffered(1)

    def part(p):
        return pl.BlockSpec((seq, w), lambda b: (b, COL_HY // w + p))

    def conv(p):
        return pl.BlockSpec((3, w), lambda b: (0, p), pipeline_mode=once)

    mat = pl.BlockSpec((seq, seq), lambda b: (0, 0), pipeline_mode=once)
    spec = pl.BlockSpec((HYENA_ORDER, seq, w), lambda b: (0, 0, 0), pipeline_mode=once)
    return pl.pallas_call(
        functools.partial(_hyena_kernel, seq=seq),
        grid=(n_batch,),
        in_specs=[
            part(0), part(1), part(2), conv(0), conv(1), conv(2),
            pl.BlockSpec((HYENA_ORDER, w), lambda b: (0, 0), pipeline_mode=once),
            spec, spec,
            pl.BlockSpec((HYENA_ORDER, 1, w), lambda b: (0, 0, 0), pipeline_mode=once),
            mat, mat,
        ],
        out_specs=pl.BlockSpec((seq, w), lambda b: (b, 0)),
        out_shape=jax.ShapeDtypeStruct((rows, w), BF16),
        compiler_params=_params(1),
        name="hyena",
    )(proj, proj, proj, hy_conv, hy_conv, hy_conv, hy_bias, kr, ki, kn, dft[0], dft[2])


def _outproj_kernel(att_ref, ret_ref, hy_ref, x_ref, g_ref, w_ref, lw_ref, lb_ref, o_ref):
    r0, r1 = ATT_WIDTH, ATT_WIDTH + RET_WIDTH
    for t0 in range(0, x_ref.shape[0], OUTPROJ_SUB):
        rows = slice(t0, t0 + OUTPROJ_SUB)
        mix = (_dot(att_ref[rows, :], w_ref[0:r0, :]) + _dot(ret_ref[rows, :], w_ref[r0:r1, :])
               + _dot(hy_ref[rows, :], w_ref[r1:, :]))
        y = DEEPNORM_ALPHA * x_ref[rows, :] + g_ref[...] * mix
        o_ref[rows, :] = _layer_norm(y, lw_ref[...], lb_ref[...])


def _outproj(att, ret, hyz, x, mod, w_out, layer, ln_w, ln_b):
    rows = x.shape[0]
    tm = OUTPROJ_TM
    tiles_per_mod = rows // mod.shape[0] // tm
    row = lambda width: pl.BlockSpec((tm, width), lambda i: (i, 0))
    vec = pl.BlockSpec((1, D_MODEL), lambda i: (0, 0))
    return pl.pallas_call(
        _outproj_kernel,
        grid=(rows // tm,),
        in_specs=[
            row(ATT_WIDTH), row(RET_WIDTH), row(HYENA_WIDTH), row(D_MODEL),
            pl.BlockSpec((None, 1, D_MODEL), lambda i: (i // tiles_per_mod, 0, 2)),
            pl.BlockSpec((None, D_MODEL, D_MODEL), lambda i: (layer, 0, 0), pipeline_mode=pl.Buffered(1)),
            vec, vec,
        ],
        out_specs=row(D_MODEL),
        out_shape=jax.ShapeDtypeStruct((rows, D_MODEL), F32),
        compiler_params=_params(1),
        name="outproj_ln",
    )(att, ret, hyz, x, mod, w_out, ln_w.reshape(1, D_MODEL), ln_b.reshape(1, D_MODEL))


def _ffn_kernel(x_ref, xp_ref, xn_ref, sh_ref, sc_ref, g_ref, wg_ref, wu_ref, cg_ref, cu_ref, wd_ref,
                lw_ref, lb_ref, o_ref, h_scr, acc_scr, act0_scr, act1_scr, *, seq, tm, n_f):
    i = pl.program_id(0)
    j = pl.program_id(1)
    acts = (act0_scr, act1_scr)

    pos = (i * tm + lax.broadcasted_iota(jnp.int32, (tm, 1), 0)) & (seq - 1)

    def conv(up, c_ref):
        prev = _shift_rows(up, down=True)[HALO:HALO + tm]
        nxt = _shift_rows(up, down=False)[HALO:HALO + tm]
        c = c_ref[...]
        return (jnp.where(pos == 0, 0.0, prev) * c[0:1] + up[HALO:HALO + tm] * c[1:2]
                + jnp.where(pos == seq - 1, 0.0, nxt) * c[2:3])

    def step(read_scr, write_scr):
        if write_scr is not None:
            up_g = _dot(h_scr[...], wg_ref[...])
            up_u = _dot(h_scr[...], wu_ref[...])
        if read_scr is not None:
            acc_scr[...] += _dot(read_scr[...], wd_ref[...])
        if write_scr is not None:
            write_scr[...] = (_silu(conv(up_g, cg_ref)) * conv(up_u, cu_ref)).astype(BF16)

    @pl.when(j == 0)
    def _():
        sc = 1.0 + sc_ref[...]
        sh = sh_ref[...]
        h_scr[0:HALO, :] = (xp_ref[...] * sc + sh).astype(BF16)
        h_scr[HALO:HALO + tm, :] = (x_ref[...] * sc + sh).astype(BF16)
        h_scr[HALO + tm:, :] = (xn_ref[...] * sc + sh).astype(BF16)
        acc_scr[...] = jnp.zeros_like(acc_scr)
        step(None, acts[0])

    for parity in range(2):
        @pl.when((j > 0) & (j < n_f) & ((j & 1) == parity))
        def _():
            step(acts[1 - parity], acts[parity])

    @pl.when(j == n_f)
    def _():
        step(acts[(n_f - 1) % 2], None)
        y = DEEPNORM_ALPHA * x_ref[...] + g_ref[...] * acc_scr[...]
        o_ref[...] = _layer_norm(y, lw_ref[...], lb_ref[...])


def _ffn(x, seq, mod, w_up, ffn_conv, w_down, layer, ln_w, ln_b):
    rows = x.shape[0]
    tm, tf = FFN_TM, FFN_TF
    n_f = D_FF // tf
    tiles_per_mod = rows // mod.shape[0] // tm
    halo_per_tile = tm // HALO
    last_halo = rows // HALO - 1

    def mod_spec(chunk):
        return pl.BlockSpec((None, 1, D_MODEL), lambda i, j: (i // tiles_per_mod, 0, chunk))

    def up_tile(j):
        return jnp.minimum(j, n_f - 1)

    vec = pl.BlockSpec((1, D_MODEL), lambda i, j: (0, 0))
    return pl.pallas_call(
        functools.partial(_ffn_kernel, seq=seq, tm=tm, n_f=n_f),
        grid=(rows // tm, n_f + 1),
        in_specs=[
            pl.BlockSpec((tm, D_MODEL), lambda i, j: (i, 0)),
            pl.BlockSpec((HALO, D_MODEL), lambda i, j: (jnp.maximum(i * halo_per_tile - 1, 0), 0)),
            pl.BlockSpec((HALO, D_MODEL), lambda i, j: (jnp.minimum((i + 1) * halo_per_tile, last_halo), 0)),
            mod_spec(3), mod_spec(4), mod_spec(5),
            pl.BlockSpec((None, D_MODEL, tf), lambda i, j: (layer, 0, up_tile(j))),
            pl.BlockSpec((None, D_MODEL, tf), lambda i, j: (layer, 0, n_f + up_tile(j))),
            pl.BlockSpec((None, 3, tf), lambda i, j: (layer, 0, up_tile(j))),
            pl.BlockSpec((None, 3, tf), lambda i, j: (layer, 0, n_f + up_tile(j))),
            pl.BlockSpec((None, tf, D_MODEL), lambda i, j: (layer, jnp.maximum(j - 1, 0), 0)),
            vec, vec,
        ],
        out_specs=pl.BlockSpec((tm, D_MODEL), lambda i, j: (i, 0)),
        out_shape=jax.ShapeDtypeStruct((rows, D_MODEL), F32),
        scratch_shapes=[pltpu.VMEM((tm + 2 * HALO, D_MODEL), BF16), pltpu.VMEM((tm, D_MODEL), F32),
                        pltpu.VMEM((tm, tf), BF16), pltpu.VMEM((tm, tf), BF16)],
        compiler_params=_params(2),
        name="ffn_ln",
    )(x, x, x, mod, mod, mod, w_up, w_up, ffn_conv, ffn_conv, w_down,
      ln_w.reshape(1, D_MODEL), ln_b.reshape(1, D_MODEL))


def _rope_tables(n_tokens):
    rows = n_tokens // GRID_W
    row = jnp.repeat(jnp.arange(rows, dtype=F32), GRID_W)
    col = jnp.tile(jnp.arange(GRID_W, dtype=F32), rows)
    n_freq = HEAD_DIM // 4
    inv_freq = ROPE_THETA ** (-jnp.arange(n_freq, dtype=F32) / n_freq)
    ang = jnp.concatenate([row[:, None] * inv_freq[None], col[:, None] * inv_freq[None]], -1)
    cos, sin = jnp.cos(ang), jnp.sin(ang)
    return jnp.concatenate([cos, cos], -1), jnp.concatenate([-sin, sin], -1)


def _dft_tables(seq):
    idx = jnp.arange(seq, dtype=jnp.int32)
    kn = (idx[:, None] * idx[None, :]) % (2 * seq)
    ang = kn.astype(F32) * (math.pi / seq)
    return (*_split(jnp.cos(ang)), *_split(jnp.sin(ang)))


def _filter_features(seq):
    pos = jnp.arange(seq, dtype=F32)
    t = pos / max(seq - 1, 1)
    w = 2.0 * math.pi * pos / seq
    f = jnp.linspace(1e-4, FILTER_BANDS - 1, FILTER_BANDS, dtype=F32)
    ang = w[:, None] * f[None, :]
    z = jnp.concatenate([t[:, None], jnp.cos(ang), -jnp.sin(ang)], -1)
    return jnp.pad(z, ((0, 0), (0, FILTER_PAD - FILTER_EMB)))


def _pad_to(a, shape):
    return jnp.pad(a, [(0, s - d) for d, s in zip(a.shape, shape)])


def kernel(x_prompt, x_sample, cache_k, cache_v, state_ret, c, c_ctx, w_ada, b_ada, w_in, q_norm, k_norm, ret_decay, hy_conv, hf_w1, hf_b1, hf_freq, hf_w2, hf_b2, hf_w3, hy_bias, w_out, ln1_w, ln1_b, w_up, ffn_conv, w_down, ln2_w, ln2_b):
    n_ctx, seq_ctx, _ = x_prompt.shape
    n_lat, seq_lat, _ = x_sample.shape

    cond = _pad_to(jnp.concatenate([c_ctx[None, :], c], 0), (COND_ROWS, D_MODEL))
    mod_all = _ada(cond, w_ada, b_ada)

    w_in_b, w_out_b = w_in.astype(BF16), w_out.astype(BF16)
    w_up_b, w_down_b = w_up.astype(BF16), w_down.astype(BF16)
    log_gamma = jax.nn.log_sigmoid(ret_decay.astype(F32))
    abs_deltas = jnp.abs(jnp.linspace(math.log(HYENA_DECAY_TARGET) / HYENA_DECAY_PCT_MIN,
                                      math.log(HYENA_DECAY_TARGET) / HYENA_DECAY_PCT_MAX,
                                      HYENA_WIDTH, dtype=F32))[None, :]
    ret_zero = jnp.zeros((n_ctx, 1, 2, N_RET_HEADS, HEAD_DIM, HEAD_DIM), F32)

    groups = {
        "ctx": dict(n=n_ctx, seq=seq_ctx, rope=None, dft=_dft_tables(seq_ctx), feat=_filter_features(seq_ctx)),
        "lat": dict(n=n_lat, seq=seq_lat, rope=_rope_tables(seq_lat), dft=_dft_tables(seq_lat),
                    feat=_filter_features(seq_lat)),
    }
    xs = {"ctx": x_prompt.reshape(n_ctx * seq_ctx, D_MODEL), "lat": x_sample.reshape(n_lat * seq_lat, D_MODEL)}
    ks_out, vs_out, ss_out = [], [], []

    for l in range(DEPTH):
        mods = {"ctx": mod_all[l, 0:1, None, :], "lat": mod_all[l, 1:1 + n_lat, None, :]}
        pad2 = (FILTER_PAD, FILTER_PAD)
        filt_w = (_pad_to(hf_w1[l], pad2), _pad_to(hf_b1[l][None, :], (1, FILTER_PAD)),
                  _pad_to(hf_freq[l][None, :], (1, FILTER_PAD)), _pad_to(hf_w2[l], pad2),
                  _pad_to(hf_b2[l][None, :], (1, FILTER_PAD)), _pad_to(hf_w3[l], (FILTER_PAD, hf_w3.shape[-1])))
        for name, g in groups.items():
            n, seq, x, mod = g["n"], g["seq"], xs[name], mods[name]
            is_ctx = name == "ctx"
            proj = _inproj(x, mod, w_in_b, l)
            att_out = _attention(proj, n, seq, q_norm[l], k_norm[l], g["rope"],
                                 None if is_ctx else (cache_k, cache_v), l, emit_cache=is_ctx)
            ret, s_fin = _retention(proj, n, seq, log_gamma[l],
                                    ret_zero if is_ctx else state_ret, 0 if is_ctx else l)
            spectra = _hyena_filters(seq, g["feat"], *filt_w, abs_deltas, g["dft"])
            hyz = _hyena(proj, n, seq, hy_conv[l], hy_bias[l], spectra, g["dft"])
            if is_ctx:
                att, own_k, own_v = att_out
                ks_out.append(own_k)
                vs_out.append(own_v)
                ss_out.append(s_fin)
            else:
                att = att_out[0]
            x = _outproj(att, ret, hyz, x, mod, w_out_b, l, ln1_w[l], ln1_b[l])
            xs[name] = _ffn(x, seq, mod, w_up_b, ffn_conv, w_down_b, l, ln2_w[l], ln2_b[l])

    return (xs["ctx"].reshape(n_ctx, seq_ctx, D_MODEL), xs["lat"].reshape(n_lat, seq_lat, D_MODEL),
            jnp.stack(ks_out, axis=1), jnp.stack(vs_out, axis=1), jnp.stack(ss_out, axis=1))
```

```python
import functools
import math

import numpy as np
import jax
import jax.numpy as jnp
from jax import lax
from jax.experimental import pallas as pl
from jax.experimental.pallas import tpu as pltpu

F32 = jnp.float32
BF16 = jnp.bfloat16

D_MODEL = 2048
DEPTH = 2
GRID_W = 64
HEAD_DIM = 128
ATT_WIDTH = D_MODEL // 2
N_ATT_HEADS = ATT_WIDTH // HEAD_DIM
N_KV_HEADS = N_ATT_HEADS // 4
GQA_GROUP = N_ATT_HEADS // N_KV_HEADS
KV_WIDTH = N_KV_HEADS * HEAD_DIM
RET_WIDTH = D_MODEL // 4
N_RET_HEADS = RET_WIDTH // HEAD_DIM
HYENA_WIDTH = D_MODEL // 4
HYENA_ORDER = 2
IN_WIDTH = ATT_WIDTH + 2 * KV_WIDTH + 4 * RET_WIDTH + (HYENA_ORDER + 1) * HYENA_WIDTH
D_FF = 5632
ROPE_THETA = 10000.0
FILTER_BANDS = 16
FILTER_EMB = 1 + 2 * FILTER_BANDS
FILTER_HID = 64
HYENA_DECAY_TARGET = 1e-2
HYENA_DECAY_PCT_MIN = 0.3
HYENA_DECAY_PCT_MAX = 1.5
DEEPNORM_ALPHA = (2 * DEPTH) ** 0.25
EPS = 1e-6

COL_Q = 0
COL_K = ATT_WIDTH
COL_V = COL_K + KV_WIDTH
COL_RQ = COL_V + KV_WIDTH
COL_RK = COL_RQ + RET_WIDTH
COL_RV = COL_RK + RET_WIDTH
COL_RG = COL_RV + RET_WIDTH
COL_HY = COL_RG + RET_WIDTH

LANES = 128
HALO = 16
VMEM_LIMIT = 56 * 1024 * 1024
COND_ROWS = 16

ADA_TN = 1024
INPROJ_TM, INPROJ_TN = 256, 512
OUTPROJ_TM, OUTPROJ_SUB = 512, 256
FFN_TM, FFN_TF = 512, 512
ATT_TQ = 256
RET_ROWS = 1024
FILTER_PAD = 128

NT_DIMS = (((1,), (1,)), ((), ()))
TN_DIMS = (((0,), (0,)), ((), ()))


def _params(n_grid):
    return pltpu.CompilerParams(dimension_semantics=("arbitrary",) * n_grid,
                                vmem_limit_bytes=VMEM_LIMIT)


def _dot(a, b):
    return jnp.dot(a, b, preferred_element_type=F32)


def _split(a):
    hi = a.astype(BF16)
    lo = (a - hi.astype(F32)).astype(BF16)
    return hi, lo


def _dot3(a_hi, a_lo, b_hi, b_lo):
    return _dot(a_hi, b_hi) + _dot(a_lo, b_hi) + _dot(a_hi, b_lo)


def _silu(x):
    return x * jax.nn.sigmoid(x)


def _layer_norm(y, w, b):
    yc = y - jnp.mean(y, -1, keepdims=True)
    var = jnp.mean(yc * yc, -1, keepdims=True)
    return yc * lax.rsqrt(var + EPS) * w + b


def _shift_rows(x, down):
    n = x.shape[0]
    return pltpu.roll(x, 1 if down else n - 1, axis=0)


def _ada_kernel(cond_ref, w_ref, b_ref, o_ref):
    a = _silu(cond_ref[...]).astype(BF16)
    o_ref[...] = _dot(a, w_ref[...].astype(BF16)) + b_ref[...]


def _ada(cond, w_ada, b_ada):
    n_out = w_ada.shape[-1]
    return pl.pallas_call(
        _ada_kernel,
        grid=(DEPTH, n_out // ADA_TN),
        in_specs=[
            pl.BlockSpec((COND_ROWS, D_MODEL), lambda l, j: (0, 0)),
            pl.BlockSpec((None, D_MODEL, ADA_TN), lambda l, j: (l, 0, j)),
            pl.BlockSpec((None, 1, ADA_TN), lambda l, j: (l, 0, j)),
        ],
        out_specs=pl.BlockSpec((None, COND_ROWS, ADA_TN), lambda l, j: (l, 0, j)),
        out_shape=jax.ShapeDtypeStruct((DEPTH, COND_ROWS, n_out), F32),
        compiler_params=_params(2),
        name="ada",
    )(cond, w_ada, b_ada.reshape(DEPTH, 1, n_out))


def _inproj_kernel(x_ref, sh_ref, sc_ref, w_ref, o_ref):
    h = (x_ref[...] * (1.0 + sc_ref[...]) + sh_ref[...]).astype(BF16)
    for n0 in range(0, IN_WIDTH, INPROJ_TN):
        o_ref[:, n0:n0 + INPROJ_TN] = _dot(h, w_ref[:, n0:n0 + INPROJ_TN])


def _inproj(x, mod, w_in, layer):
    rows = x.shape[0]
    tm = INPROJ_TM
    tiles_per_mod = rows // mod.shape[0] // tm
    return pl.pallas_call(
        _inproj_kernel,
        grid=(rows // tm,),
        in_specs=[
            pl.BlockSpec((tm, D_MODEL), lambda i: (i, 0)),
            pl.BlockSpec((None, 1, D_MODEL), lambda i: (i // tiles_per_mod, 0, 0)),
            pl.BlockSpec((None, 1, D_MODEL), lambda i: (i // tiles_per_mod, 0, 1)),
            pl.BlockSpec((None, D_MODEL, IN_WIDTH), lambda i: (layer, 0, 0), pipeline_mode=pl.Buffered(1)),
        ],
        out_specs=pl.BlockSpec((tm, IN_WIDTH), lambda i: (i, 0)),
        out_shape=jax.ShapeDtypeStruct((rows, IN_WIDTH), F32),
        compiler_params=_params(1),
        name="inproj",
    )(x, mod, mod, w_in)


def _rms_norm(x, w):
    return x * lax.rsqrt(jnp.mean(x * x, -1, keepdims=True) + EPS) * w


def _rope(x, cos_full, sin_signed):
    return x * cos_full + pltpu.roll(x, HEAD_DIM // 2, axis=1) * sin_signed


def _attn_kernel(*refs, rope, has_ctx, emit_cache):
    refs = list(refs)
    q_ref, k_ref, v_ref, qw_ref, kw_ref = refs[:5]
    del refs[:5]
    if rope:
        qcos_ref, qsin_ref, kcos_ref, ksin_ref = refs[:4]
        del refs[:4]
    if has_ctx:
        ck_ref, cv_ref = refs[:2]
        del refs[:2]
    o_ref = refs.pop(0)
    if emit_cache:
        ownk_ref, ownv_ref = refs[:2]
        del refs[:2]
    k_scr, v_scr = refs
    seq = k_ref.shape[0]

    @pl.when(pl.program_id(2) == 0)
    def _():
        kn = _rms_norm(k_ref[...], kw_ref[...])
        if emit_cache:
            ownk_ref[...] = kn
            ownv_ref[...] = v_ref[...]
        if rope:
            kn = _rope(kn, kcos_ref[...], ksin_ref[...])
        k_scr[0:seq, :] = kn.astype(BF16)
        v_scr[0:seq, :] = v_ref[...].astype(BF16)
        if has_ctx:
            k_scr[seq:, :] = ck_ref[...].astype(BF16)
            v_scr[seq:, :] = cv_ref[...].astype(BF16)

    for g in range(GQA_GROUP):
        qn = _rms_norm(q_ref[:, g * HEAD_DIM:(g + 1) * HEAD_DIM], qw_ref[...])
        if rope:
            qn = _rope(qn, qcos_ref[...], qsin_ref[...])
        qb = (qn * (HEAD_DIM ** -0.5)).astype(BF16)
        s = lax.dot_general(qb, k_scr[...], NT_DIMS, preferred_element_type=F32)
        p = jnp.exp(s - jnp.max(s, -1, keepdims=True))
        den = jnp.sum(p, -1, keepdims=True)
        o = _dot(p.astype(BF16), v_scr[...])
        o_ref[:, g * HEAD_DIM:(g + 1) * HEAD_DIM] = (o / den).astype(o_ref.dtype)


def _attention(proj, n_batch, seq, q_norm, k_norm, rope_tabs, ctx_kv, layer, emit_cache):
    rows = proj.shape[0]
    tq = min(ATT_TQ, seq)
    nq = seq // tq
    group_w = GQA_GROUP * HEAD_DIM
    rope = rope_tabs is not None
    has_ctx = ctx_kv is not None

    in_specs = [
        pl.BlockSpec((tq, group_w), lambda b, kv, qi: (b * nq + qi, COL_Q // group_w + kv)),
        pl.BlockSpec((seq, HEAD_DIM), lambda b, kv, qi: (b, COL_K // HEAD_DIM + kv)),
        pl.BlockSpec((seq, HEAD_DIM), lambda b, kv, qi: (b, COL_V // HEAD_DIM + kv)),
        pl.BlockSpec((1, HEAD_DIM), lambda b, kv, qi: (0, 0)),
        pl.BlockSpec((1, HEAD_DIM), lambda b, kv, qi: (0, 0)),
    ]
    args = [proj, proj, proj, q_norm.reshape(1, HEAD_DIM), k_norm.reshape(1, HEAD_DIM)]
    if rope:
        cos_full, sin_signed = rope_tabs
        in_specs += [
            pl.BlockSpec((tq, HEAD_DIM), lambda b, kv, qi: (qi, 0)),
            pl.BlockSpec((tq, HEAD_DIM), lambda b, kv, qi: (qi, 0)),
            pl.BlockSpec((seq, HEAD_DIM), lambda b, kv, qi: (0, 0)),
            pl.BlockSpec((seq, HEAD_DIM), lambda b, kv, qi: (0, 0)),
        ]
        args += [cos_full, sin_signed, cos_full, sin_signed]
    if has_ctx:
        ck, cv = ctx_kv
        past = ck.shape[3]
        ctx_spec = pl.BlockSpec((None, None, None, past, HEAD_DIM), lambda b, kv, qi: (b, layer, kv, 0, 0))
        in_specs += [ctx_spec, ctx_spec]
        args += [ck, cv]

    out_specs = [pl.BlockSpec((tq, group_w), lambda b, kv, qi: (b * nq + qi, kv))]
    out_shape = [jax.ShapeDtypeStruct((rows, ATT_WIDTH), BF16)]
    if emit_cache:
        own_spec = pl.BlockSpec((None, None, seq, HEAD_DIM), lambda b, kv, qi: (b, kv, 0, 0))
        out_specs += [own_spec, own_spec]
        out_shape += [jax.ShapeDtypeStruct((n_batch, N_KV_HEADS, seq, HEAD_DIM), F32)] * 2

    return pl.pallas_call(
        functools.partial(_attn_kernel, rope=rope, has_ctx=has_ctx, emit_cache=emit_cache),
        grid=(n_batch, N_KV_HEADS, nq),
        in_specs=in_specs,
        out_specs=out_specs,
        out_shape=out_shape,
        scratch_shapes=[pltpu.VMEM((seq + (past if has_ctx else 0), HEAD_DIM), BF16)] * 2,
        compiler_params=_params(3),
        name="attention",
    )(*args)


def _retention_kernel(lg_ref, q_ref, k_ref, v_ref, g_ref, s0_ref, o_ref, sf_ref, decay_scr, *, seq, n_seq):
    h = pl.program_id(0)
    lg_f = lg_ref[h]
    lg_b = lg_ref[N_RET_HEADS + h]

    @pl.when(pl.program_id(1) == 0)
    def _():
        i = lax.broadcasted_iota(jnp.int32, (seq, seq), 0)
        j = lax.broadcasted_iota(jnp.int32, (seq, seq), 1)
        d = (i - j).astype(F32)
        decay = jnp.exp(jnp.where(d >= 0, lg_f, lg_b) * jnp.abs(d))
        decay_scr[...] = jnp.where(d == 0, 2.0, decay)

    pos = lax.broadcasted_iota(jnp.int32, (seq, 1), 0).astype(F32)
    q_decay_f, q_decay_b = jnp.exp(lg_f * (pos + 1.0)), jnp.exp(lg_b * (seq - pos))
    k_decay_f, k_decay_b = jnp.exp(lg_f * (seq - 1.0 - pos)), jnp.exp(lg_b * pos)
    ones = jnp.ones((1, HEAD_DIM), F32)
    s_decay_f, s_decay_b = jnp.exp(ones * (lg_f * seq)), jnp.exp(ones * (lg_b * seq))

    for s in range(n_seq):
        rows = slice(s * seq, (s + 1) * seq)
        q = q_ref[rows, :]
        k = k_ref[rows, :] * (HEAD_DIM ** -0.5)
        vb = v_ref[rows, :].astype(BF16)
        a = lax.dot_general(q.astype(BF16), k.astype(BF16), NT_DIMS, preferred_element_type=F32)
        o = _dot((a * decay_scr[...]).astype(BF16), vb)
        s0_f = s0_ref[s, 0]
        s0_b = s0_ref[s, 1]
        o = o + _dot((q * q_decay_f).astype(BF16), s0_f.astype(BF16))
        o = o + _dot((q * q_decay_b).astype(BF16), s0_b.astype(BF16))

        kf = (k * k_decay_f).astype(BF16)
        kb = (k * k_decay_b).astype(BF16)
        sf_ref[s, 0] = s_decay_f * s0_f + lax.dot_general(kf, vb, TN_DIMS, preferred_element_type=F32)
        sf_ref[s, 1] = s_decay_b * s0_b + lax.dot_general(kb, vb, TN_DIMS, preferred_element_type=F32)

        oc = o - jnp.mean(o, -1, keepdims=True)
        var = jnp.mean(oc * oc, -1, keepdims=True)
        o_ref[rows, :] = (_silu(g_ref[rows, :]) * (oc * lax.rsqrt(var + EPS))).astype(o_ref.dtype)


def _retention(proj, n_batch, seq, log_gamma, s0, layer):
    rows = proj.shape[0]
    n_seq = max(1, RET_ROWS // seq)
    hd = HEAD_DIM

    def col(offset):
        return pl.BlockSpec((n_seq * seq, hd), lambda h, b: (b, offset // hd + h))

    return pl.pallas_call(
        functools.partial(_retention_kernel, seq=seq, n_seq=n_seq),
        grid=(N_RET_HEADS, n_batch // n_seq),
        in_specs=[
            pl.BlockSpec(memory_space=pltpu.SMEM),
            col(COL_RQ), col(COL_RK), col(COL_RV), col(COL_RG),
            pl.BlockSpec((n_seq, None, 2, None, hd, hd), lambda h, b: (b, layer, 0, h, 0, 0)),
        ],
        out_specs=[
            pl.BlockSpec((n_seq * seq, hd), lambda h, b: (b, h)),
            pl.BlockSpec((n_seq, 2, None, hd, hd), lambda h, b: (b, 0, h, 0, 0)),
        ],
        out_shape=[
            jax.ShapeDtypeStruct((rows, RET_WIDTH), BF16),
            jax.ShapeDtypeStruct((n_batch, 2, N_RET_HEADS, hd, hd), F32),
        ],
        scratch_shapes=[pltpu.VMEM((seq, seq), F32)],
        compiler_params=_params(2),
        name="retention",
    )(log_gamma.reshape(2 * N_RET_HEADS), proj, proj, proj, proj, s0)


def _alt_sign(n):
    r = lax.broadcasted_iota(jnp.int32, (n, 1), 0)
    return jnp.where((r & 1) == 0, 1.0, -1.0).astype(F32)


def _filter_kernel(z_ref, w1_ref, b1_ref, fr_ref, w2_ref, b2_ref, w3f_ref, w3b_ref, dl_ref,
                   ch_ref, cl_ref, sh_ref, sl_ref, kr_ref, ki_ref, kn_ref, *, seq):
    freq = fr_ref[...]
    hdn = jnp.sin(freq * (_dot3(*_split(z_ref[...]), *_split(w1_ref[...])) + b1_ref[...]))
    hdn = jnp.sin(freq * (_dot3(*_split(hdn), *_split(w2_ref[...])) + b2_ref[...]))
    hdn_hi, hdn_lo = _split(hdn)

    pos = lax.broadcasted_iota(jnp.int32, (seq, 1), 0)
    t = pos.astype(F32) / max(seq - 1, 1)
    decay = jnp.exp(-t * dl_ref[...])
    h_fwd = _dot3(hdn_hi, hdn_lo, *_split(w3f_ref[...])) * decay
    h_bwd = _dot3(hdn_hi, hdn_lo, *_split(w3b_ref[...])) * decay

    h_bwd_shift = jnp.where(pos == 0, 0.0, _shift_rows(h_bwd, down=True))
    alt = _alt_sign(seq)
    kr = _dot3(ch_ref[...], cl_ref[...], *_split(h_fwd + h_bwd_shift)) + alt * h_bwd[seq - 1:seq, :]
    ki = _dot3(sh_ref[...], sl_ref[...], *_split(h_bwd_shift - h_fwd))
    wgt = jnp.where(pos == 0, 1.0, 2.0) * (0.5 / seq)
    kr_ref[...] = kr * wgt
    ki_ref[...] = ki * wgt
    kn_ref[...] = jnp.sum(alt * (h_fwd - h_bwd), axis=0, keepdims=True) * (0.5 / seq)


def _hyena_filters(seq, zfeat, w1, b1, freq, w2, b2, w3, abs_deltas, dft):
    w = HYENA_WIDTH
    const = lambda o: (0, 0)
    once = pl.Buffered(1)
    mat = pl.BlockSpec((seq, seq), const, pipeline_mode=once)
    vec = pl.BlockSpec((1, FILTER_PAD), const)
    return pl.pallas_call(
        functools.partial(_filter_kernel, seq=seq),
        grid=(HYENA_ORDER,),
        in_specs=[
            pl.BlockSpec((seq, FILTER_PAD), const),
            pl.BlockSpec((FILTER_PAD, FILTER_PAD), const), vec, vec,
            pl.BlockSpec((FILTER_PAD, FILTER_PAD), const), vec,
            pl.BlockSpec((FILTER_PAD, w), lambda o: (0, 2 * o)),
            pl.BlockSpec((FILTER_PAD, w), lambda o: (0, 2 * o + 1)),
            pl.BlockSpec((1, w), const),
            mat, mat, mat, mat,
        ],
        out_specs=[
            pl.BlockSpec((None, seq, w), lambda o: (o, 0, 0)),
            pl.BlockSpec((None, seq, w), lambda o: (o, 0, 0)),
            pl.BlockSpec((None, 1, w), lambda o: (o, 0, 0)),
        ],
        out_shape=[
            jax.ShapeDtypeStruct((HYENA_ORDER, seq, HYENA_WIDTH), F32),
            jax.ShapeDtypeStruct((HYENA_ORDER, seq, HYENA_WIDTH), F32),
            jax.ShapeDtypeStruct((HYENA_ORDER, 1, HYENA_WIDTH), F32),
        ],
        compiler_params=_params(1),
        name="hyena_filters",
    )(zfeat, w1, b1, freq, w2, b2, w3, w3, abs_deltas, *dft)


def _dwconv3_seq(x, w, pos, seq):
    prev = jnp.where(pos == 0, 0.0, _shift_rows(x, down=True))
    nxt = jnp.where(pos == seq - 1, 0.0, _shift_rows(x, down=False))
    return prev * w[0:1] + x * w[1:2] + nxt * w[2:3]


def _hyena_kernel(v_ref, x1_ref, x2_ref, cv_ref, c1_ref, c2_ref, bias_ref, kr_ref, ki_ref, kn_ref,
                  cos_ref, sin_ref, o_ref, *, seq):
    pos = lax.broadcasted_iota(jnp.int32, (seq, 1), 0)
    alt = _alt_sign(seq)

    def dft(m_ref, a):
        return _dot(m_ref[...], a.astype(BF16))

    z = _dwconv3_seq(v_ref[...], cv_ref[...], pos, seq)
    for o, (x_ref, c_ref) in enumerate(((x1_ref, c1_ref), (x2_ref, c2_ref))):
        ur = dft(cos_ref, z)
        ui = dft(sin_ref, z)
        u_nyq = jnp.sum(alt * z, axis=0, keepdims=True)
        kr, ki = kr_ref[o], ki_ref[o]
        yr = ur * kr + ui * ki
        yi = ui * kr - ur * ki
        y = dft(cos_ref, yr) + dft(sin_ref, yi) + alt * (u_nyq * kn_ref[o])
        z = _dwconv3_seq(x_ref[...], c_ref[...], pos, seq) * (y + z * bias_ref[o:o + 1, :])
    o_ref[...] = z.astype(o_ref.dtype)


def _hyena(proj, n_batch, seq, hy_conv, hy_bias, spectra, dft):
    rows = proj.shape[0]
    w = HYENA_WIDTH
    kr, ki, kn = spectra
    once = pl.Buffered(1)

    def part(p):
        return pl.BlockSpec((seq, w), lambda b: (b, COL_HY // w + p))

    def conv(p):
        return pl.BlockSpec((3, w), lambda b: (0, p), pipeline_mode=once)

    mat = pl.BlockSpec((seq, seq), lambda b: (0, 0), pipeline_mode=once)
    spec = pl.BlockSpec((HYENA_ORDER, seq, w), lambda b: (0, 0, 0), pipeline_mode=once)
    return pl.pallas_call(
        functools.partial(_hyena_kernel, seq=seq),
        grid=(n_batch,),
        in_specs=[
            part(0), part(1), part(2), conv(0), conv(1), conv(2),
            pl.BlockSpec((HYENA_ORDER, w), lambda b: (0, 0), pipeline_mode=once),
            spec, spec,
            pl.BlockSpec((HYENA_ORDER, 1, w), lambda b: (0, 0, 0), pipeline_mode=once),
            mat, mat,
        ],
        out_specs=pl.BlockSpec((seq, w), lambda b: (b, 0)),
        out_shape=jax.ShapeDtypeStruct((rows, w), BF16),
        compiler_params=_params(1),
        name="hyena",
    )(proj, proj, proj, hy_conv, hy_conv, hy_conv, hy_bias, kr, ki, kn, dft[0], dft[2])


def _outproj_kernel(att_ref, ret_ref, hy_ref, x_ref, g_ref, w_ref, lw_ref, lb_ref, o_ref):
    r0, r1 = ATT_WIDTH, ATT_WIDTH + RET_WIDTH
    for t0 in range(0, x_ref.shape[0], OUTPROJ_SUB):
        rows = slice(t0, t0 + OUTPROJ_SUB)
        mix = (_dot(att_ref[rows, :], w_ref[0:r0, :]) + _dot(ret_ref[rows, :], w_ref[r0:r1, :])
               + _dot(hy_ref[rows, :], w_ref[r1:, :]))
        y = DEEPNORM_ALPHA * x_ref[rows, :] + g_ref[...] * mix
        o_ref[rows, :] = _layer_norm(y, lw_ref[...], lb_ref[...])


def _outproj(att, ret, hyz, x, mod, w_out, layer, ln_w, ln_b):
    rows = x.shape[0]
    tm = OUTPROJ_TM
    tiles_per_mod = rows // mod.shape[0] // tm
    row = lambda width: pl.BlockSpec((tm, width), lambda i: (i, 0))
    vec = pl.BlockSpec((1, D_MODEL), lambda i: (0, 0))
    return pl.pallas_call(
        _outproj_kernel,
        grid=(rows // tm,),
        in_specs=[
            row(ATT_WIDTH), row(RET_WIDTH), row(HYENA_WIDTH), row(D_MODEL),
            pl.BlockSpec((None, 1, D_MODEL), lambda i: (i // tiles_per_mod, 0, 2)),
            pl.BlockSpec((None, D_MODEL, D_MODEL), lambda i: (layer, 0, 0), pipeline_mode=pl.Buffered(1)),
            vec, vec,
        ],
        out_specs=row(D_MODEL),
        out_shape=jax.ShapeDtypeStruct((rows, D_MODEL), F32),
        compiler_params=_params(1),
        name="outproj_ln",
    )(att, ret, hyz, x, mod, w_out, ln_w.reshape(1, D_MODEL), ln_b.reshape(1, D_MODEL))


def _ffn_kernel(x_ref, xp_ref, xn_ref, sh_ref, sc_ref, g_ref, wup_ref, cup_ref, wd_ref,
                lw_ref, lb_ref, o_ref, h_scr, acc_scr, *, seq, tm):
    i = pl.program_id(0)
    j = pl.program_id(1)
    tf = wd_ref.shape[0]

    @pl.when(j == 0)
    def _():
        sc = 1.0 + sc_ref[...]
        sh = sh_ref[...]
        h_scr[0:HALO, :] = (xp_ref[...] * sc + sh).astype(BF16)
        h_scr[HALO:HALO + tm, :] = (x_ref[...] * sc + sh).astype(BF16)
        h_scr[HALO + tm:, :] = (xn_ref[...] * sc + sh).astype(BF16)
        acc_scr[...] = jnp.zeros_like(acc_scr)

    pos = (i * tm + lax.broadcasted_iota(jnp.int32, (tm, 1), 0)) & (seq - 1)

    up = _dot(h_scr[...], wup_ref[...])
    prev = _shift_rows(up, down=True)[HALO:HALO + tm]
    nxt = _shift_rows(up, down=False)[HALO:HALO + tm]
    c = cup_ref[...]
    up = (jnp.where(pos == 0, 0.0, prev) * c[0:1] + up[HALO:HALO + tm] * c[1:2]
          + jnp.where(pos == seq - 1, 0.0, nxt) * c[2:3])
    act = _silu(up[:, :tf]) * up[:, tf:]
    acc_scr[...] += _dot(act.astype(BF16), wd_ref[...])

    @pl.when(j == pl.num_programs(1) - 1)
    def _():
        y = DEEPNORM_ALPHA * x_ref[...] + g_ref[...] * acc_scr[...]
        o_ref[...] = _layer_norm(y, lw_ref[...], lb_ref[...])


def _tile_gate_up(w):
    depth, k, _ = w.shape
    n_f = D_FF // FFN_TF
    return w.reshape(depth, k, 2, n_f, FFN_TF).transpose(0, 3, 1, 2, 4).reshape(depth, n_f, k, 2 * FFN_TF)


def _ffn(x, seq, mod, w_up, ffn_conv, w_down, layer, ln_w, ln_b):
    rows = x.shape[0]
    tm, tf = FFN_TM, FFN_TF
    n_f = D_FF // tf
    tiles_per_mod = rows // mod.shape[0] // tm
    halo_per_tile = tm // HALO
    last_halo = rows // HALO - 1

    def mod_spec(chunk):
        return pl.BlockSpec((None, 1, D_MODEL), lambda i, j: (i // tiles_per_mod, 0, chunk))

    vec = pl.BlockSpec((1, D_MODEL), lambda i, j: (0, 0))
    return pl.pallas_call(
        functools.partial(_ffn_kernel, seq=seq, tm=tm),
        grid=(rows // tm, n_f),
        in_specs=[
            pl.BlockSpec((tm, D_MODEL), lambda i, j: (i, 0)),
            pl.BlockSpec((HALO, D_MODEL), lambda i, j: (jnp.maximum(i * halo_per_tile - 1, 0), 0)),
            pl.BlockSpec((HALO, D_MODEL), lambda i, j: (jnp.minimum((i + 1) * halo_per_tile, last_halo), 0)),
            mod_spec(3), mod_spec(4), mod_spec(5),
            pl.BlockSpec((None, None, D_MODEL, 2 * tf), lambda i, j: (layer, j, 0, 0)),
            pl.BlockSpec((None, None, 3, 2 * tf), lambda i, j: (layer, j, 0, 0)),
            pl.BlockSpec((None, tf, D_MODEL), lambda i, j: (layer, j, 0)),
            vec, vec,
        ],
        out_specs=pl.BlockSpec((tm, D_MODEL), lambda i, j: (i, 0)),
        out_shape=jax.ShapeDtypeStruct((rows, D_MODEL), F32),
        scratch_shapes=[pltpu.VMEM((tm + 2 * HALO, D_MODEL), BF16), pltpu.VMEM((tm, D_MODEL), F32)],
        compiler_params=_params(2),
        name="ffn_ln",
    )(x, x, x, mod, mod, mod, w_up, ffn_conv, w_down, ln_w.reshape(1, D_MODEL), ln_b.reshape(1, D_MODEL))


def _rope_tables(n_tokens):
    t = np.arange(n_tokens)
    n_freq = HEAD_DIM // 4
    inv_freq = ROPE_THETA ** (-np.arange(n_freq) / n_freq)
    ang = np.concatenate([(t // GRID_W)[:, None] * inv_freq[None], (t % GRID_W)[:, None] * inv_freq[None]], -1)
    cos, sin = np.cos(ang), np.sin(ang)
    return (jnp.asarray(np.concatenate([cos, cos], -1), F32), jnp.asarray(np.concatenate([-sin, sin], -1), F32))


def _dft_tables(seq):
    idx = np.arange(seq)
    ang = ((idx[:, None] * idx[None, :]) % (2 * seq)) * (math.pi / seq)
    return (*_split(jnp.asarray(np.cos(ang), F32)), *_split(jnp.asarray(np.sin(ang), F32)))


def _filter_features(seq):
    pos = np.arange(seq)
    t = pos / max(seq - 1, 1)
    ang = (2.0 * math.pi * pos / seq)[:, None] * np.linspace(1e-4, FILTER_BANDS - 1, FILTER_BANDS)[None, :]
    z = np.concatenate([t[:, None], np.cos(ang), -np.sin(ang)], -1)
    return jnp.asarray(np.pad(z, ((0, 0), (0, FILTER_PAD - FILTER_EMB))), F32)


def _pad_to(a, shape):
    return jnp.pad(a, [(0, s - d) for d, s in zip(a.shape, shape)])


def kernel(x_prompt, x_sample, cache_k, cache_v, state_ret, c, c_ctx, w_ada, b_ada, w_in, q_norm, k_norm, ret_decay, hy_conv, hf_w1, hf_b1, hf_freq, hf_w2, hf_b2, hf_w3, hy_bias, w_out, ln1_w, ln1_b, w_up, ffn_conv, w_down, ln2_w, ln2_b):
    n_ctx, seq_ctx, _ = x_prompt.shape
    n_lat, seq_lat, _ = x_sample.shape

    cond = _pad_to(jnp.concatenate([c_ctx[None, :], c], 0), (COND_ROWS, D_MODEL))
    mod_all = _ada(cond, w_ada, b_ada)

    w_in_b, w_out_b = w_in.astype(BF16), w_out.astype(BF16)
    w_up_b, w_down_b = _tile_gate_up(w_up.astype(BF16)), w_down.astype(BF16)
    ffn_conv_t = _tile_gate_up(ffn_conv)
    log_gamma = jax.nn.log_sigmoid(ret_decay.astype(F32))
    abs_deltas = jnp.asarray(np.abs(np.linspace(math.log(HYENA_DECAY_TARGET) / HYENA_DECAY_PCT_MIN,
                                                math.log(HYENA_DECAY_TARGET) / HYENA_DECAY_PCT_MAX,
                                                HYENA_WIDTH))[None, :], F32)
    ret_zero = jnp.zeros((n_ctx, 1, 2, N_RET_HEADS, HEAD_DIM, HEAD_DIM), F32)

    groups = {
        "ctx": dict(n=n_ctx, seq=seq_ctx, rope=None, dft=_dft_tables(seq_ctx), feat=_filter_features(seq_ctx)),
        "lat": dict(n=n_lat, seq=seq_lat, rope=_rope_tables(seq_lat), dft=_dft_tables(seq_lat),
                    feat=_filter_features(seq_lat)),
    }
    xs = {"ctx": x_prompt.reshape(n_ctx * seq_ctx, D_MODEL), "lat": x_sample.reshape(n_lat * seq_lat, D_MODEL)}
    ks_out, vs_out, ss_out = [], [], []

    for l in range(DEPTH):
        mods = {"ctx": mod_all[l, 0:1, None, :], "lat": mod_all[l, 1:1 + n_lat, None, :]}
        pad2 = (FILTER_PAD, FILTER_PAD)
        filt_w = (_pad_to(hf_w1[l], pad2), _pad_to(hf_b1[l][None, :], (1, FILTER_PAD)),
                  _pad_to(hf_freq[l][None, :], (1, FILTER_PAD)), _pad_to(hf_w2[l], pad2),
                  _pad_to(hf_b2[l][None, :], (1, FILTER_PAD)), _pad_to(hf_w3[l], (FILTER_PAD, hf_w3.shape[-1])))
        for name, g in groups.items():
            n, seq, x, mod = g["n"], g["seq"], xs[name], mods[name]
            is_ctx = name == "ctx"
            proj = _inproj(x, mod, w_in_b, l)
            att_out = _attention(proj, n, seq, q_norm[l], k_norm[l], g["rope"],
                                 None if is_ctx else (cache_k, cache_v), l, emit_cache=is_ctx)
            ret, s_fin = _retention(proj, n, seq, log_gamma[l],
                                    ret_zero if is_ctx else state_ret, 0 if is_ctx else l)
            spectra = _hyena_filters(seq, g["feat"], *filt_w, abs_deltas, g["dft"])
            hyz = _hyena(proj, n, seq, hy_conv[l], hy_bias[l], spectra, g["dft"])
            if is_ctx:
                att, own_k, own_v = att_out
                ks_out.append(own_k)
                vs_out.append(own_v)
                ss_out.append(s_fin)
            else:
                att = att_out[0]
            x = _outproj(att, ret, hyz, x, mod, w_out_b, l, ln1_w[l], ln1_b[l])
            xs[name] = _ffn(x, seq, mod, w_up_b, ffn_conv_t, w_down_b, l, ln2_w[l], ln2_b[l])

    return (xs["ctx"].reshape(n_ctx, seq_ctx, D_MODEL), xs["lat"].reshape(n_lat, seq_lat, D_MODEL),
            jnp.stack(ks_out, axis=1), jnp.stack(vs_out, axis=1), jnp.stack(ss_out, axis=1))
```

```python
import functools
import math

import numpy as np
import jax
import jax.numpy as jnp
from jax import lax
from jax.experimental import pallas as pl
from jax.experimental.pallas import tpu as pltpu

F32 = jnp.float32
BF16 = jnp.bfloat16

D_MODEL = 2048
DEPTH = 2
GRID_W = 64
HEAD_DIM = 128
ATT_WIDTH = D_MODEL // 2
N_ATT_HEADS = ATT_WIDTH // HEAD_DIM
N_KV_HEADS = N_ATT_HEADS // 4
GQA_GROUP = N_ATT_HEADS // N_KV_HEADS
KV_WIDTH = N_KV_HEADS * HEAD_DIM
RET_WIDTH = D_MODEL // 4
N_RET_HEADS = RET_WIDTH // HEAD_DIM
HYENA_WIDTH = D_MODEL // 4
HYENA_ORDER = 2
IN_WIDTH = ATT_WIDTH + 2 * KV_WIDTH + 4 * RET_WIDTH + (HYENA_ORDER + 1) * HYENA_WIDTH
D_FF = 5632
ROPE_THETA = 10000.0
FILTER_BANDS = 16
FILTER_EMB = 1 + 2 * FILTER_BANDS
FILTER_HID = 64
HYENA_DECAY_TARGET = 1e-2
HYENA_DECAY_PCT_MIN = 0.3
HYENA_DECAY_PCT_MAX = 1.5
DEEPNORM_ALPHA = (2 * DEPTH) ** 0.25
EPS = 1e-6

COL_Q = 0
COL_K = ATT_WIDTH
COL_V = COL_K + KV_WIDTH
COL_RQ = COL_V + KV_WIDTH
COL_RK = COL_RQ + RET_WIDTH
COL_RV = COL_RK + RET_WIDTH
COL_RG = COL_RV + RET_WIDTH
COL_HY = COL_RG + RET_WIDTH

LANES = 128
HALO = 16
VMEM_LIMIT = 60 * 1024 * 1024
COND_ROWS = 16

ADA_TN = 1024
INPROJ_TM, INPROJ_TN = 256, 512
OUTPROJ_TM, OUTPROJ_SUB = 512, 256
FFN_TM, FFN_TF = 1024, 512
ATT_TQ = 256
RET_ROWS = 1024
FILTER_PAD = 128

NT_DIMS = (((1,), (1,)), ((), ()))
TN_DIMS = (((0,), (0,)), ((), ()))


def _params(n_grid):
    return pltpu.CompilerParams(dimension_semantics=("arbitrary",) * n_grid,
                                vmem_limit_bytes=VMEM_LIMIT)


def _dot(a, b):
    return jnp.dot(a, b, preferred_element_type=F32)


def _split(a):
    hi = a.astype(BF16)
    lo = (a - hi.astype(F32)).astype(BF16)
    return hi, lo


def _dot3(a_hi, a_lo, b_hi, b_lo):
    return _dot(a_hi, b_hi) + _dot(a_lo, b_hi) + _dot(a_hi, b_lo)


def _silu(x):
    return x * jax.nn.sigmoid(x)


def _layer_norm(y, w, b):
    yc = y - jnp.mean(y, -1, keepdims=True)
    var = jnp.mean(yc * yc, -1, keepdims=True)
    return yc * lax.rsqrt(var + EPS) * w + b


def _shift_rows(x, down):
    n = x.shape[0]
    return pltpu.roll(x, 1 if down else n - 1, axis=0)


def _ada_kernel(cond_ref, w_ref, b_ref, o_ref):
    a = _silu(cond_ref[...]).astype(BF16)
    o_ref[...] = _dot(a, w_ref[...].astype(BF16)) + b_ref[...]


def _ada(cond, w_ada, b_ada):
    n_out = w_ada.shape[-1]
    return pl.pallas_call(
        _ada_kernel,
        grid=(DEPTH, n_out // ADA_TN),
        in_specs=[
            pl.BlockSpec((COND_ROWS, D_MODEL), lambda l, j: (0, 0)),
            pl.BlockSpec((None, D_MODEL, ADA_TN), lambda l, j: (l, 0, j)),
            pl.BlockSpec((None, 1, ADA_TN), lambda l, j: (l, 0, j)),
        ],
        out_specs=pl.BlockSpec((None, COND_ROWS, ADA_TN), lambda l, j: (l, 0, j)),
        out_shape=jax.ShapeDtypeStruct((DEPTH, COND_ROWS, n_out), F32),
        compiler_params=_params(2),
        name="ada",
    )(cond, w_ada, b_ada.reshape(DEPTH, 1, n_out))


def _inproj_kernel(x_ref, sh_ref, sc_ref, w_ref, o_ref):
    h = (x_ref[...] * (1.0 + sc_ref[...]) + sh_ref[...]).astype(BF16)
    for n0 in range(0, IN_WIDTH, INPROJ_TN):
        o_ref[:, n0:n0 + INPROJ_TN] = _dot(h, w_ref[:, n0:n0 + INPROJ_TN])


def _inproj(x, mod, w_in, layer):
    rows = x.shape[0]
    tm = INPROJ_TM
    tiles_per_mod = rows // mod.shape[0] // tm
    return pl.pallas_call(
        _inproj_kernel,
        grid=(rows // tm,),
        in_specs=[
            pl.BlockSpec((tm, D_MODEL), lambda i: (i, 0)),
            pl.BlockSpec((None, 1, D_MODEL), lambda i: (i // tiles_per_mod, 0, 0)),
            pl.BlockSpec((None, 1, D_MODEL), lambda i: (i // tiles_per_mod, 0, 1)),
            pl.BlockSpec((None, D_MODEL, IN_WIDTH), lambda i: (layer, 0, 0), pipeline_mode=pl.Buffered(1)),
        ],
        out_specs=pl.BlockSpec((tm, IN_WIDTH), lambda i: (i, 0)),
        out_shape=jax.ShapeDtypeStruct((rows, IN_WIDTH), F32),
        compiler_params=_params(1),
        name="inproj",
    )(x, mod, mod, w_in)


def _rms_norm(x, w):
    return x * lax.rsqrt(jnp.mean(x * x, -1, keepdims=True) + EPS) * w


def _rope(x, cos_full, sin_signed):
    return x * cos_full + pltpu.roll(x, HEAD_DIM // 2, axis=1) * sin_signed


def _attn_kernel(*refs, rope, has_ctx, emit_cache):
    refs = list(refs)
    q_ref, k_ref, v_ref, qw_ref, kw_ref = refs[:5]
    del refs[:5]
    if rope:
        qcos_ref, qsin_ref, kcos_ref, ksin_ref = refs[:4]
        del refs[:4]
    if has_ctx:
        ck_ref, cv_ref = refs[:2]
        del refs[:2]
    o_ref = refs.pop(0)
    if emit_cache:
        ownk_ref, ownv_ref = refs[:2]
        del refs[:2]
    k_scr, v_scr = refs
    seq = k_ref.shape[0]

    @pl.when(pl.program_id(2) == 0)
    def _():
        kn = _rms_norm(k_ref[...], kw_ref[...])
        if emit_cache:
            ownk_ref[...] = kn
            ownv_ref[...] = v_ref[...]
        if rope:
            kn = _rope(kn, kcos_ref[...], ksin_ref[...])
        k_scr[0:seq, :] = kn.astype(BF16)
        v_scr[0:seq, :] = v_ref[...].astype(BF16)
        if has_ctx:
            k_scr[seq:, :] = ck_ref[...].astype(BF16)
            v_scr[seq:, :] = cv_ref[...].astype(BF16)

    for g in range(GQA_GROUP):
        qn = _rms_norm(q_ref[:, g * HEAD_DIM:(g + 1) * HEAD_DIM], qw_ref[...])
        if rope:
            qn = _rope(qn, qcos_ref[...], qsin_ref[...])
        qb = (qn * (HEAD_DIM ** -0.5)).astype(BF16)
        s = lax.dot_general(qb, k_scr[...], NT_DIMS, preferred_element_type=F32)
        p = jnp.exp(s - jnp.max(s, -1, keepdims=True))
        den = jnp.sum(p, -1, keepdims=True)
        o = _dot(p.astype(BF16), v_scr[...])
        o_ref[:, g * HEAD_DIM:(g + 1) * HEAD_DIM] = (o / den).astype(o_ref.dtype)


def _attention(proj, n_batch, seq, q_norm, k_norm, rope_tabs, ctx_kv, layer, emit_cache):
    rows = proj.shape[0]
    tq = min(ATT_TQ, seq)
    nq = seq // tq
    group_w = GQA_GROUP * HEAD_DIM
    rope = rope_tabs is not None
    has_ctx = ctx_kv is not None

    in_specs = [
        pl.BlockSpec((tq, group_w), lambda b, kv, qi: (b * nq + qi, COL_Q // group_w + kv)),
        pl.BlockSpec((seq, HEAD_DIM), lambda b, kv, qi: (b, COL_K // HEAD_DIM + kv)),
        pl.BlockSpec((seq, HEAD_DIM), lambda b, kv, qi: (b, COL_V // HEAD_DIM + kv)),
        pl.BlockSpec((1, HEAD_DIM), lambda b, kv, qi: (0, 0)),
        pl.BlockSpec((1, HEAD_DIM), lambda b, kv, qi: (0, 0)),
    ]
    args = [proj, proj, proj, q_norm.reshape(1, HEAD_DIM), k_norm.reshape(1, HEAD_DIM)]
    if rope:
        cos_full, sin_signed = rope_tabs
        in_specs += [
            pl.BlockSpec((tq, HEAD_DIM), lambda b, kv, qi: (qi, 0)),
            pl.BlockSpec((tq, HEAD_DIM), lambda b, kv, qi: (qi, 0)),
            pl.BlockSpec((seq, HEAD_DIM), lambda b, kv, qi: (0, 0)),
            pl.BlockSpec((seq, HEAD_DIM), lambda b, kv, qi: (0, 0)),
        ]
        args += [cos_full, sin_signed, cos_full, sin_signed]
    if has_ctx:
        ck, cv = ctx_kv
        past = ck.shape[3]
        ctx_spec = pl.BlockSpec((None, None, None, past, HEAD_DIM), lambda b, kv, qi: (b, layer, kv, 0, 0))
        in_specs += [ctx_spec, ctx_spec]
        args += [ck, cv]

    out_specs = [pl.BlockSpec((tq, group_w), lambda b, kv, qi: (b * nq + qi, kv))]
    out_shape = [jax.ShapeDtypeStruct((rows, ATT_WIDTH), BF16)]
    if emit_cache:
        own_spec = pl.BlockSpec((None, None, seq, HEAD_DIM), lambda b, kv, qi: (b, kv, 0, 0))
        out_specs += [own_spec, own_spec]
        out_shape += [jax.ShapeDtypeStruct((n_batch, N_KV_HEADS, seq, HEAD_DIM), F32)] * 2

    return pl.pallas_call(
        functools.partial(_attn_kernel, rope=rope, has_ctx=has_ctx, emit_cache=emit_cache),
        grid=(n_batch, N_KV_HEADS, nq),
        in_specs=in_specs,
        out_specs=out_specs,
        out_shape=out_shape,
        scratch_shapes=[pltpu.VMEM((seq + (past if has_ctx else 0), HEAD_DIM), BF16)] * 2,
        compiler_params=_params(3),
        name="attention",
    )(*args)


def _retention_kernel(lg_ref, q_ref, k_ref, v_ref, g_ref, s0_ref, o_ref, sf_ref, decay_scr, *, seq, n_seq):
    h = pl.program_id(0)
    lg_f = lg_ref[h]
    lg_b = lg_ref[N_RET_HEADS + h]

    @pl.when(pl.program_id(1) == 0)
    def _():
        i = lax.broadcasted_iota(jnp.int32, (seq, seq), 0)
        j = lax.broadcasted_iota(jnp.int32, (seq, seq), 1)
        d = (i - j).astype(F32)
        decay = jnp.exp(jnp.where(d >= 0, lg_f, lg_b) * jnp.abs(d))
        decay_scr[...] = jnp.where(d == 0, 2.0, decay)

    pos = lax.broadcasted_iota(jnp.int32, (seq, 1), 0).astype(F32)
    q_decay_f, q_decay_b = jnp.exp(lg_f * (pos + 1.0)), jnp.exp(lg_b * (seq - pos))
    k_decay_f, k_decay_b = jnp.exp(lg_f * (seq - 1.0 - pos)), jnp.exp(lg_b * pos)
    ones = jnp.ones((1, HEAD_DIM), F32)
    s_decay_f, s_decay_b = jnp.exp(ones * (lg_f * seq)), jnp.exp(ones * (lg_b * seq))

    for s in range(n_seq):
        rows = slice(s * seq, (s + 1) * seq)
        q = q_ref[rows, :]
        k = k_ref[rows, :] * (HEAD_DIM ** -0.5)
        vb = v_ref[rows, :].astype(BF16)
        a = lax.dot_general(q.astype(BF16), k.astype(BF16), NT_DIMS, preferred_element_type=F32)
        o = _dot((a * decay_scr[...]).astype(BF16), vb)
        s0_f = s0_ref[s, 0]
        s0_b = s0_ref[s, 1]
        o = o + _dot((q * q_decay_f).astype(BF16), s0_f.astype(BF16))
        o = o + _dot((q * q_decay_b).astype(BF16), s0_b.astype(BF16))

        kf = (k * k_decay_f).astype(BF16)
        kb = (k * k_decay_b).astype(BF16)
        sf_ref[s, 0] = s_decay_f * s0_f + lax.dot_general(kf, vb, TN_DIMS, preferred_element_type=F32)
        sf_ref[s, 1] = s_decay_b * s0_b + lax.dot_general(kb, vb, TN_DIMS, preferred_element_type=F32)

        oc = o - jnp.mean(o, -1, keepdims=True)
        var = jnp.mean(oc * oc, -1, keepdims=True)
        o_ref[rows, :] = (_silu(g_ref[rows, :]) * (oc * lax.rsqrt(var + EPS))).astype(o_ref.dtype)


def _retention(proj, n_batch, seq, log_gamma, s0, layer):
    rows = proj.shape[0]
    n_seq = max(1, RET_ROWS // seq)
    hd = HEAD_DIM

    def col(offset):
        return pl.BlockSpec((n_seq * seq, hd), lambda h, b: (b, offset // hd + h))

    return pl.pallas_call(
        functools.partial(_retention_kernel, seq=seq, n_seq=n_seq),
        grid=(N_RET_HEADS, n_batch // n_seq),
        in_specs=[
            pl.BlockSpec(memory_space=pltpu.SMEM),
            col(COL_RQ), col(COL_RK), col(COL_RV), col(COL_RG),
            pl.BlockSpec((n_seq, None, 2, None, hd, hd), lambda h, b: (b, layer, 0, h, 0, 0)),
        ],
        out_specs=[
            pl.BlockSpec((n_seq * seq, hd), lambda h, b: (b, h)),
            pl.BlockSpec((n_seq, 2, None, hd, hd), lambda h, b: (b, 0, h, 0, 0)),
        ],
        out_shape=[
            jax.ShapeDtypeStruct((rows, RET_WIDTH), BF16),
            jax.ShapeDtypeStruct((n_batch, 2, N_RET_HEADS, hd, hd), F32),
        ],
        scratch_shapes=[pltpu.VMEM((seq, seq), F32)],
        compiler_params=_params(2),
        name="retention",
    )(log_gamma.reshape(2 * N_RET_HEADS), proj, proj, proj, proj, s0)


def _alt_sign(n):
    r = lax.broadcasted_iota(jnp.int32, (n, 1), 0)
    return jnp.where((r & 1) == 0, 1.0, -1.0).astype(F32)


def _filter_kernel(z_ref, w1_ref, b1_ref, fr_ref, w2_ref, b2_ref, w3f_ref, w3b_ref, dl_ref,
                   ch_ref, cl_ref, sh_ref, sl_ref, kr_ref, ki_ref, kn_ref, *, seq):
    freq = fr_ref[...]
    hdn = jnp.sin(freq * (_dot3(*_split(z_ref[...]), *_split(w1_ref[...])) + b1_ref[...]))
    hdn = jnp.sin(freq * (_dot3(*_split(hdn), *_split(w2_ref[...])) + b2_ref[...]))
    hdn_hi, hdn_lo = _split(hdn)

    pos = lax.broadcasted_iota(jnp.int32, (seq, 1), 0)
    t = pos.astype(F32) / max(seq - 1, 1)
    decay = jnp.exp(-t * dl_ref[...])
    h_fwd = _dot3(hdn_hi, hdn_lo, *_split(w3f_ref[...])) * decay
    h_bwd = _dot3(hdn_hi, hdn_lo, *_split(w3b_ref[...])) * decay

    h_bwd_shift = jnp.where(pos == 0, 0.0, _shift_rows(h_bwd, down=True))
    alt = _alt_sign(seq)
    kr = _dot3(ch_ref[...], cl_ref[...], *_split(h_fwd + h_bwd_shift)) + alt * h_bwd[seq - 1:seq, :]
    ki = _dot3(sh_ref[...], sl_ref[...], *_split(h_bwd_shift - h_fwd))
    wgt = jnp.where(pos == 0, 1.0, 2.0) * (0.5 / seq)
    kr_ref[...] = kr * wgt
    ki_ref[...] = ki * wgt
    kn_ref[...] = jnp.sum(alt * (h_fwd - h_bwd), axis=0, keepdims=True) * (0.5 / seq)


def _hyena_filters(seq, zfeat, w1, b1, freq, w2, b2, w3, abs_deltas, dft):
    w = HYENA_WIDTH
    const = lambda o: (0, 0)
    once = pl.Buffered(1)
    mat = pl.BlockSpec((seq, seq), const, pipeline_mode=once)
    vec = pl.BlockSpec((1, FILTER_PAD), const)
    return pl.pallas_call(
        functools.partial(_filter_kernel, seq=seq),
        grid=(HYENA_ORDER,),
        in_specs=[
            pl.BlockSpec((seq, FILTER_PAD), const),
            pl.BlockSpec((FILTER_PAD, FILTER_PAD), const), vec, vec,
            pl.BlockSpec((FILTER_PAD, FILTER_PAD), const), vec,
            pl.BlockSpec((FILTER_PAD, w), lambda o: (0, 2 * o)),
            pl.BlockSpec((FILTER_PAD, w), lambda o: (0, 2 * o + 1)),
            pl.BlockSpec((1, w), const),
            mat, mat, mat, mat,
        ],
        out_specs=[
            pl.BlockSpec((None, seq, w), lambda o: (o, 0, 0)),
            pl.BlockSpec((None, seq, w), lambda o: (o, 0, 0)),
            pl.BlockSpec((None, 1, w), lambda o: (o, 0, 0)),
        ],
        out_shape=[
            jax.ShapeDtypeStruct((HYENA_ORDER, seq, HYENA_WIDTH), F32),
            jax.ShapeDtypeStruct((HYENA_ORDER, seq, HYENA_WIDTH), F32),
            jax.ShapeDtypeStruct((HYENA_ORDER, 1, HYENA_WIDTH), F32),
        ],
        compiler_params=_params(1),
        name="hyena_filters",
    )(zfeat, w1, b1, freq, w2, b2, w3, w3, abs_deltas, *dft)


def _dwconv3_seq(x, w, pos, seq):
    prev = jnp.where(pos == 0, 0.0, _shift_rows(x, down=True))
    nxt = jnp.where(pos == seq - 1, 0.0, _shift_rows(x, down=False))
    return prev * w[0:1] + x * w[1:2] + nxt * w[2:3]


def _hyena_kernel(v_ref, x1_ref, x2_ref, cv_ref, c1_ref, c2_ref, bias_ref, kr_ref, ki_ref, kn_ref,
                  cos_ref, sin_ref, o_ref, *, seq):
    pos = lax.broadcasted_iota(jnp.int32, (seq, 1), 0)
    alt = _alt_sign(seq)

    def dft(m_ref, a):
        return _dot(m_ref[...], a.astype(BF16))

    z = _dwconv3_seq(v_ref[...], cv_ref[...], pos, seq)
    for o, (x_ref, c_ref) in enumerate(((x1_ref, c1_ref), (x2_ref, c2_ref))):
        ur = dft(cos_ref, z)
        ui = dft(sin_ref, z)
        u_nyq = jnp.sum(alt * z, axis=0, keepdims=True)
        kr, ki = kr_ref[o], ki_ref[o]
        yr = ur * kr + ui * ki
        yi = ui * kr - ur * ki
        y = dft(cos_ref, yr) + dft(sin_ref, yi) + alt * (u_nyq * kn_ref[o])
        z = _dwconv3_seq(x_ref[...], c_ref[...], pos, seq) * (y + z * bias_ref[o:o + 1, :])
    o_ref[...] = z.astype(o_ref.dtype)


def _hyena(proj, n_batch, seq, hy_conv, hy_bias, spectra, dft):
    rows = proj.shape[0]
    w = HYENA_WIDTH
    kr, ki, kn = spectra
    once = pl.Buffered(1)

    def part(p):
        return pl.BlockSpec((seq, w), lambda b: (b, COL_HY // w + p))

    def conv(p):
        return pl.BlockSpec((3, w), lambda b: (0, p), pipeline_mode=once)

    mat = pl.BlockSpec((seq, seq), lambda b: (0, 0), pipeline_mode=once)
    spec = pl.BlockSpec((HYENA_ORDER, seq, w), lambda b: (0, 0, 0), pipeline_mode=once)
    return pl.pallas_call(
        functools.partial(_hyena_kernel, seq=seq),
        grid=(n_batch,),
        in_specs=[
            part(0), part(1), part(2), conv(0), conv(1), conv(2),
            pl.BlockSpec((HYENA_ORDER, w), lambda b: (0, 0), pipeline_mode=once),
            spec, spec,
            pl.BlockSpec((HYENA_ORDER, 1, w), lambda b: (0, 0, 0), pipeline_mode=once),
            mat, mat,
        ],
        out_specs=pl.BlockSpec((seq, w), lambda b: (b, 0)),
        out_shape=jax.ShapeDtypeStruct((rows, w), BF16),
        compiler_params=_params(1),
        name="hyena",
    )(proj, proj, proj, hy_conv, hy_conv, hy_conv, hy_bias, kr, ki, kn, dft[0], dft[2])


def _outproj_kernel(att_ref, ret_ref, hy_ref, x_ref, g_ref, w_ref, lw_ref, lb_ref, o_ref):
    r0, r1 = ATT_WIDTH, ATT_WIDTH + RET_WIDTH
    for t0 in range(0, x_ref.shape[0], OUTPROJ_SUB):
        rows = slice(t0, t0 + OUTPROJ_SUB)
        mix = (_dot(att_ref[rows, :], w_ref[0:r0, :]) + _dot(ret_ref[rows, :], w_ref[r0:r1, :])
               + _dot(hy_ref[rows, :], w_ref[r1:, :]))
        y = DEEPNORM_ALPHA * x_ref[rows, :] + g_ref[...] * mix
        o_ref[rows, :] = _layer_norm(y, lw_ref[...], lb_ref[...])


def _outproj(att, ret, hyz, x, mod, w_out, layer, ln_w, ln_b):
    rows = x.shape[0]
    tm = OUTPROJ_TM
    tiles_per_mod = rows // mod.shape[0] // tm
    row = lambda width: pl.BlockSpec((tm, width), lambda i: (i, 0))
    vec = pl.BlockSpec((1, D_MODEL), lambda i: (0, 0))
    return pl.pallas_call(
        _outproj_kernel,
        grid=(rows // tm,),
        in_specs=[
            row(ATT_WIDTH), row(RET_WIDTH), row(HYENA_WIDTH), row(D_MODEL),
            pl.BlockSpec((None, 1, D_MODEL), lambda i: (i // tiles_per_mod, 0, 2)),
            pl.BlockSpec((None, D_MODEL, D_MODEL), lambda i: (layer, 0, 0), pipeline_mode=pl.Buffered(1)),
            vec, vec,
        ],
        out_specs=row(D_MODEL),
        out_shape=jax.ShapeDtypeStruct((rows, D_MODEL), F32),
        compiler_params=_params(1),
        name="outproj_ln",
    )(att, ret, hyz, x, mod, w_out, ln_w.reshape(1, D_MODEL), ln_b.reshape(1, D_MODEL))


def _ffn_kernel(x_ref, xp_ref, xn_ref, sh_ref, sc_ref, g_ref, wg_ref, wu_ref, cg_ref, cu_ref, wd_ref,
                lw_ref, lb_ref, o_ref, h_scr, *, seq, tm):
    i = pl.program_id(0)
    j = pl.program_id(1)

    @pl.when(j == 0)
    def _():
        sc = 1.0 + sc_ref[...]
        sh = sh_ref[...]
        h_scr[0:HALO, :] = (xp_ref[...] * sc + sh).astype(BF16)
        h_scr[HALO:HALO + tm, :] = (x_ref[...] * sc + sh).astype(BF16)
        h_scr[HALO + tm:, :] = (xn_ref[...] * sc + sh).astype(BF16)
        o_ref[...] = jnp.zeros_like(o_ref)

    pos = (i * tm + lax.broadcasted_iota(jnp.int32, (tm, 1), 0)) & (seq - 1)

    def up_conv(w_ref, c_ref):
        up = _dot(h_scr[...], w_ref[...])
        prev = _shift_rows(up, down=True)[HALO:HALO + tm]
        nxt = _shift_rows(up, down=False)[HALO:HALO + tm]
        c = c_ref[...]
        return (jnp.where(pos == 0, 0.0, prev) * c[0:1] + up[HALO:HALO + tm] * c[1:2]
                + jnp.where(pos == seq - 1, 0.0, nxt) * c[2:3])

    act = _silu(up_conv(wg_ref, cg_ref)) * up_conv(wu_ref, cu_ref)
    o_ref[...] += _dot(act.astype(BF16), wd_ref[...])

    @pl.when(j == pl.num_programs(1) - 1)
    def _():
        y = DEEPNORM_ALPHA * x_ref[...] + g_ref[...] * o_ref[...]
        o_ref[...] = _layer_norm(y, lw_ref[...], lb_ref[...])


def _ffn(x, seq, mod, w_up, ffn_conv, w_down, layer, ln_w, ln_b):
    rows = x.shape[0]
    tm, tf = FFN_TM, FFN_TF
    n_f = D_FF // tf
    tiles_per_mod = rows // mod.shape[0] // tm
    halo_per_tile = tm // HALO
    last_halo = rows // HALO - 1

    def mod_spec(chunk):
        return pl.BlockSpec((None, 1, D_MODEL), lambda i, j: (i // tiles_per_mod, 0, chunk))

    vec = pl.BlockSpec((1, D_MODEL), lambda i, j: (0, 0))
    return pl.pallas_call(
        functools.partial(_ffn_kernel, seq=seq, tm=tm),
        grid=(rows // tm, n_f),
        in_specs=[
            pl.BlockSpec((tm, D_MODEL), lambda i, j: (i, 0), pipeline_mode=pl.Buffered(1)),
            pl.BlockSpec((HALO, D_MODEL), lambda i, j: (jnp.maximum(i * halo_per_tile - 1, 0), 0)),
            pl.BlockSpec((HALO, D_MODEL), lambda i, j: (jnp.minimum((i + 1) * halo_per_tile, last_halo), 0)),
            mod_spec(3), mod_spec(4), mod_spec(5),
            pl.BlockSpec((None, D_MODEL, tf), lambda i, j: (layer, 0, j)),
            pl.BlockSpec((None, D_MODEL, tf), lambda i, j: (layer, 0, n_f + j)),
            pl.BlockSpec((None, 3, tf), lambda i, j: (layer, 0, j)),
            pl.BlockSpec((None, 3, tf), lambda i, j: (layer, 0, n_f + j)),
            pl.BlockSpec((None, tf, D_MODEL), lambda i, j: (layer, j, 0)),
            vec, vec,
        ],
        out_specs=pl.BlockSpec((tm, D_MODEL), lambda i, j: (i, 0)),
        out_shape=jax.ShapeDtypeStruct((rows, D_MODEL), F32),
        scratch_shapes=[pltpu.VMEM((tm + 2 * HALO, D_MODEL), BF16)],
        compiler_params=_params(2),
        name="ffn_ln",
    )(x, x, x, mod, mod, mod, w_up, w_up, ffn_conv, ffn_conv, w_down,
      ln_w.reshape(1, D_MODEL), ln_b.reshape(1, D_MODEL))


def _rope_tables(n_tokens):
    t = np.arange(n_tokens)
    n_freq = HEAD_DIM // 4
    inv_freq = ROPE_THETA ** (-np.arange(n_freq) / n_freq)
    ang = np.concatenate([(t // GRID_W)[:, None] * inv_freq[None], (t % GRID_W)[:, None] * inv_freq[None]], -1)
    cos, sin = np.cos(ang), np.sin(ang)
    return (jnp.asarray(np.concatenate([cos, cos], -1), F32), jnp.asarray(np.concatenate([-sin, sin], -1), F32))


def _dft_tables(seq):
    idx = np.arange(seq)
    ang = ((idx[:, None] * idx[None, :]) % (2 * seq)) * (math.pi / seq)
    return (*_split(jnp.asarray(np.cos(ang), F32)), *_split(jnp.asarray(np.sin(ang), F32)))


def _filter_features(seq):
    pos = np.arange(seq)
    t = pos / max(seq - 1, 1)
    ang = (2.0 * math.pi * pos / seq)[:, None] * np.linspace(1e-4, FILTER_BANDS - 1, FILTER_BANDS)[None, :]
    z = np.concatenate([t[:, None], np.cos(ang), -np.sin(ang)], -1)
    return jnp.asarray(np.pad(z, ((0, 0), (0, FILTER_PAD - FILTER_EMB))), F32)


def _pad_to(a, shape):
    return jnp.pad(a, [(0, s - d) for d, s in zip(a.shape, shape)])


def kernel(x_prompt, x_sample, cache_k, cache_v, state_ret, c, c_ctx, w_ada, b_ada, w_in, q_norm, k_norm, ret_decay, hy_conv, hf_w1, hf_b1, hf_freq, hf_w2, hf_b2, hf_w3, hy_bias, w_out, ln1_w, ln1_b, w_up, ffn_conv, w_down, ln2_w, ln2_b):
    n_ctx, seq_ctx, _ = x_prompt.shape
    n_lat, seq_lat, _ = x_sample.shape

    cond = _pad_to(jnp.concatenate([c_ctx[None, :], c], 0), (COND_ROWS, D_MODEL))
    mod_all = _ada(cond, w_ada, b_ada)

    w_in_b, w_out_b = w_in.astype(BF16), w_out.astype(BF16)
    w_up_b, w_down_b = w_up.astype(BF16), w_down.astype(BF16)
    log_gamma = jax.nn.log_sigmoid(ret_decay.astype(F32))
    abs_deltas = jnp.asarray(np.abs(np.linspace(math.log(HYENA_DECAY_TARGET) / HYENA_DECAY_PCT_MIN,
                                                math.log(HYENA_DECAY_TARGET) / HYENA_DECAY_PCT_MAX,
                                                HYENA_WIDTH))[None, :], F32)
    ret_zero = jnp.zeros((n_ctx, 1, 2, N_RET_HEADS, HEAD_DIM, HEAD_DIM), F32)

    groups = {
        "ctx": dict(n=n_ctx, seq=seq_ctx, rope=None, dft=_dft_tables(seq_ctx), feat=_filter_features(seq_ctx)),
        "lat": dict(n=n_lat, seq=seq_lat, rope=_rope_tables(seq_lat), dft=_dft_tables(seq_lat),
                    feat=_filter_features(seq_lat)),
    }
    xs = {"ctx": x_prompt.reshape(n_ctx * seq_ctx, D_MODEL), "lat": x_sample.reshape(n_lat * seq_lat, D_MODEL)}
    ks_out, vs_out, ss_out = [], [], []

    for l in range(DEPTH):
        mods = {"ctx": mod_all[l, 0:1, None, :], "lat": mod_all[l, 1:1 + n_lat, None, :]}
        pad2 = (FILTER_PAD, FILTER_PAD)
        filt_w = (_pad_to(hf_w1[l], pad2), _pad_to(hf_b1[l][None, :], (1, FILTER_PAD)),
                  _pad_to(hf_freq[l][None, :], (1, FILTER_PAD)), _pad_to(hf_w2[l], pad2),
                  _pad_to(hf_b2[l][None, :], (1, FILTER_PAD)), _pad_to(hf_w3[l], (FILTER_PAD, hf_w3.shape[-1])))
        for name, g in groups.items():
            n, seq, x, mod = g["n"], g["seq"], xs[name], mods[name]
            is_ctx = name == "ctx"
            proj = _inproj(x, mod, w_in_b, l)
            att_out = _attention(proj, n, seq, q_norm[l], k_norm[l], g["rope"],
                                 None if is_ctx else (cache_k, cache_v), l, emit_cache=is_ctx)
            ret, s_fin = _retention(proj, n, seq, log_gamma[l],
                                    ret_zero if is_ctx else state_ret, 0 if is_ctx else l)
            spectra = _hyena_filters(seq, g["feat"], *filt_w, abs_deltas, g["dft"])
            hyz = _hyena(proj, n, seq, hy_conv[l], hy_bias[l], spectra, g["dft"])
            if is_ctx:
                att, own_k, own_v = att_out
                ks_out.append(own_k)
                vs_out.append(own_v)
                ss_out.append(s_fin)
            else:
                att = att_out[0]
            x = _outproj(att, ret, hyz, x, mod, w_out_b, l, ln1_w[l], ln1_b[l])
            xs[name] = _ffn(x, seq, mod, w_up_b, ffn_conv, w_down_b, l, ln2_w[l], ln2_b[l])

    return (xs["ctx"].reshape(n_ctx, seq_ctx, D_MODEL), xs["lat"].reshape(n_lat, seq_lat, D_MODEL),
            jnp.stack(ks_out, axis=1), jnp.stack(vs_out, axis=1), jnp.stack(ss_out, axis=1))
```

```python
import functools
import math

import numpy as np
import jax
import jax.numpy as jnp
from jax import lax
from jax.experimental import pallas as pl
from jax.experimental.pallas import tpu as pltpu

F32 = jnp.float32
BF16 = jnp.bfloat16

D_MODEL = 2048
DEPTH = 2
GRID_W = 64
HEAD_DIM = 128
ATT_WIDTH = D_MODEL // 2
N_ATT_HEADS = ATT_WIDTH // HEAD_DIM
N_KV_HEADS = N_ATT_HEADS // 4
GQA_GROUP = N_ATT_HEADS // N_KV_HEADS
KV_WIDTH = N_KV_HEADS * HEAD_DIM
RET_WIDTH = D_MODEL // 4
N_RET_HEADS = RET_WIDTH // HEAD_DIM
HYENA_WIDTH = D_MODEL // 4
HYENA_ORDER = 2
IN_WIDTH = ATT_WIDTH + 2 * KV_WIDTH + 4 * RET_WIDTH + (HYENA_ORDER + 1) * HYENA_WIDTH
D_FF = 5632
ROPE_THETA = 10000.0
FILTER_BANDS = 16
FILTER_EMB = 1 + 2 * FILTER_BANDS
FILTER_HID = 64
HYENA_DECAY_TARGET = 1e-2
HYENA_DECAY_PCT_MIN = 0.3
HYENA_DECAY_PCT_MAX = 1.5
DEEPNORM_ALPHA = (2 * DEPTH) ** 0.25
EPS = 1e-6

COL_Q = 0
COL_K = ATT_WIDTH
COL_V = COL_K + KV_WIDTH
COL_RQ = COL_V + KV_WIDTH
COL_RK = COL_RQ + RET_WIDTH
COL_RV = COL_RK + RET_WIDTH
COL_RG = COL_RV + RET_WIDTH
COL_HY = COL_RG + RET_WIDTH

LANES = 128
HALO = 16
VMEM_LIMIT = 60 * 1024 * 1024
COND_ROWS = 16

ADA_TN = 1024
INPROJ_TM, INPROJ_TN = 256, 512
OUTPROJ_TM, OUTPROJ_SUB = 512, 256
FFN_TM, FFN_TF = 1024, 512
ATT_TQ, ATT_CHAIN = 512, 256
ATT_ROWS = 1024
RET_ROWS = 1024
HYENA_ROWS = 1024
FILTER_PAD = 128

NT_DIMS = (((1,), (1,)), ((), ()))
TN_DIMS = (((0,), (0,)), ((), ()))


def _params(n_grid):
    return pltpu.CompilerParams(dimension_semantics=("arbitrary",) * n_grid,
                                vmem_limit_bytes=VMEM_LIMIT)


def _dot(a, b):
    return jnp.dot(a, b, preferred_element_type=F32)


def _split(a):
    hi = a.astype(BF16)
    lo = (a - hi.astype(F32)).astype(BF16)
    return hi, lo


def _dot3(a_hi, a_lo, b_hi, b_lo):
    return _dot(a_hi, b_hi) + _dot(a_lo, b_hi) + _dot(a_hi, b_lo)


def _silu(x):
    return x * jax.nn.sigmoid(x)


def _layer_norm(y, w, b):
    yc = y - jnp.mean(y, -1, keepdims=True)
    var = jnp.mean(yc * yc, -1, keepdims=True)
    return yc * lax.rsqrt(var + EPS) * w + b


def _shift_rows(x, down):
    n = x.shape[0]
    return pltpu.roll(x, 1 if down else n - 1, axis=0)


def _ada_kernel(cond_ref, w_ref, b_ref, o_ref):
    a = _silu(cond_ref[...]).astype(BF16)
    o_ref[...] = _dot(a, w_ref[...].astype(BF16)) + b_ref[...]


def _ada(cond, w_ada, b_ada):
    n_out = w_ada.shape[-1]
    return pl.pallas_call(
        _ada_kernel,
        grid=(DEPTH, n_out // ADA_TN),
        in_specs=[
            pl.BlockSpec((COND_ROWS, D_MODEL), lambda l, j: (0, 0)),
            pl.BlockSpec((None, D_MODEL, ADA_TN), lambda l, j: (l, 0, j)),
            pl.BlockSpec((None, 1, ADA_TN), lambda l, j: (l, 0, j)),
        ],
        out_specs=pl.BlockSpec((None, COND_ROWS, ADA_TN), lambda l, j: (l, 0, j)),
        out_shape=jax.ShapeDtypeStruct((DEPTH, COND_ROWS, n_out), F32),
        compiler_params=_params(2),
        name="ada",
    )(cond, w_ada, b_ada.reshape(DEPTH, 1, n_out))


def _inproj_kernel(x_ref, sh_ref, sc_ref, w_ref, o_ref):
    h = (x_ref[...] * (1.0 + sc_ref[...]) + sh_ref[...]).astype(BF16)
    for n0 in range(0, IN_WIDTH, INPROJ_TN):
        o_ref[:, n0:n0 + INPROJ_TN] = _dot(h, w_ref[:, n0:n0 + INPROJ_TN])


def _inproj(x, mod, w_in, layer):
    rows = x.shape[0]
    tm = INPROJ_TM
    tiles_per_mod = rows // mod.shape[0] // tm
    return pl.pallas_call(
        _inproj_kernel,
        grid=(rows // tm,),
        in_specs=[
            pl.BlockSpec((tm, D_MODEL), lambda i: (i, 0)),
            pl.BlockSpec((None, 1, D_MODEL), lambda i: (i // tiles_per_mod, 0, 0)),
            pl.BlockSpec((None, 1, D_MODEL), lambda i: (i // tiles_per_mod, 0, 1)),
            pl.BlockSpec((None, D_MODEL, IN_WIDTH), lambda i: (layer, 0, 0), pipeline_mode=pl.Buffered(1)),
        ],
        out_specs=pl.BlockSpec((tm, IN_WIDTH), lambda i: (i, 0)),
        out_shape=jax.ShapeDtypeStruct((rows, IN_WIDTH), F32),
        compiler_params=_params(1),
        name="inproj",
    )(x, mod, mod, w_in)


def _rms_norm(x, w):
    return x * lax.rsqrt(jnp.mean(x * x, -1, keepdims=True) + EPS) * w


def _rope(x, cos_full, sin_signed):
    return x * cos_full + pltpu.roll(x, HEAD_DIM // 2, axis=1) * sin_signed


def _attn_kernel(*refs, seq, n_seq, rope, has_ctx, emit_cache):
    refs = list(refs)
    q_ref, k_ref, v_ref, qw_ref, kw_ref = refs[:5]
    del refs[:5]
    if rope:
        qcos_ref, qsin_ref, kcos_ref, ksin_ref = refs[:4]
        del refs[:4]
    if has_ctx:
        ck_ref, cv_ref = refs[:2]
        del refs[:2]
    o_ref = refs.pop(0)
    if emit_cache:
        ownk_ref, ownv_ref = refs[:2]
        del refs[:2]
    k_scr, v_scr = refs
    own = n_seq * seq

    @pl.when(pl.program_id(2) == 0)
    def _():
        kn = _rms_norm(k_ref[...], kw_ref[...])
        if emit_cache:
            for s in range(n_seq):
                ownk_ref[s] = kn[s * seq:(s + 1) * seq]
                ownv_ref[s] = v_ref[s * seq:(s + 1) * seq, :]
        if rope:
            kn = _rope(kn, kcos_ref[...], ksin_ref[...])
        k_scr[0:own, :] = kn.astype(BF16)
        v_scr[0:own, :] = v_ref[...].astype(BF16)
        if has_ctx:
            k_scr[own:, :] = ck_ref[...].astype(BF16)
            v_scr[own:, :] = cv_ref[...].astype(BF16)

    tq = q_ref.shape[0] // n_seq
    for s in range(n_seq):
        keys = slice(s * seq, (s + 1) * seq) if n_seq > 1 else slice(None)
        for g in range(GQA_GROUP):
            cols = slice(g * HEAD_DIM, (g + 1) * HEAD_DIM)
            for r0 in range(s * tq, (s + 1) * tq, ATT_CHAIN):
                rows = slice(r0, r0 + ATT_CHAIN)
                qn = _rms_norm(q_ref[rows, cols], qw_ref[...])
                if rope:
                    qn = _rope(qn, qcos_ref[rows, :], qsin_ref[rows, :])
                qb = (qn * (HEAD_DIM ** -0.5)).astype(BF16)
                sc = lax.dot_general(qb, k_scr[keys, :], NT_DIMS, preferred_element_type=F32)
                p = jnp.exp(sc - jnp.max(sc, -1, keepdims=True))
                den = jnp.sum(p, -1, keepdims=True)
                o = _dot(p.astype(BF16), v_scr[keys, :])
                o_ref[rows, cols] = (o / den).astype(o_ref.dtype)


def _attention(proj, n_batch, seq, q_norm, k_norm, rope_tabs, ctx_kv, layer, emit_cache):
    rows = proj.shape[0]
    tq = min(ATT_TQ, seq)
    nq = seq // tq
    group_w = GQA_GROUP * HEAD_DIM
    rope = rope_tabs is not None
    has_ctx = ctx_kv is not None
    n_seq = max(1, ATT_ROWS // seq) if not (rope or has_ctx) else 1

    in_specs = [
        pl.BlockSpec((n_seq * tq, group_w), lambda b, kv, qi: (b * nq + qi, COL_Q // group_w + kv)),
        pl.BlockSpec((n_seq * seq, HEAD_DIM), lambda b, kv, qi: (b, COL_K // HEAD_DIM + kv)),
        pl.BlockSpec((n_seq * seq, HEAD_DIM), lambda b, kv, qi: (b, COL_V // HEAD_DIM + kv)),
        pl.BlockSpec((1, HEAD_DIM), lambda b, kv, qi: (0, 0)),
        pl.BlockSpec((1, HEAD_DIM), lambda b, kv, qi: (0, 0)),
    ]
    args = [proj, proj, proj, q_norm.reshape(1, HEAD_DIM), k_norm.reshape(1, HEAD_DIM)]
    if rope:
        cos_full, sin_signed = rope_tabs
        in_specs += [
            pl.BlockSpec((tq, HEAD_DIM), lambda b, kv, qi: (qi, 0)),
            pl.BlockSpec((tq, HEAD_DIM), lambda b, kv, qi: (qi, 0)),
            pl.BlockSpec((seq, HEAD_DIM), lambda b, kv, qi: (0, 0)),
            pl.BlockSpec((seq, HEAD_DIM), lambda b, kv, qi: (0, 0)),
        ]
        args += [cos_full, sin_signed, cos_full, sin_signed]
    if has_ctx:
        ck, cv = ctx_kv
        past = ck.shape[3]
        ctx_spec = pl.BlockSpec((None, None, None, past, HEAD_DIM), lambda b, kv, qi: (b, layer, kv, 0, 0))
        in_specs += [ctx_spec, ctx_spec]
        args += [ck, cv]

    out_specs = [pl.BlockSpec((n_seq * tq, group_w), lambda b, kv, qi: (b * nq + qi, kv))]
    out_shape = [jax.ShapeDtypeStruct((rows, ATT_WIDTH), BF16)]
    if emit_cache:
        own_spec = pl.BlockSpec((n_seq, None, seq, HEAD_DIM), lambda b, kv, qi: (b, kv, 0, 0))
        out_specs += [own_spec, own_spec]
        out_shape += [jax.ShapeDtypeStruct((n_batch, N_KV_HEADS, seq, HEAD_DIM), F32)] * 2

    return pl.pallas_call(
        functools.partial(_attn_kernel, seq=seq, n_seq=n_seq, rope=rope, has_ctx=has_ctx, emit_cache=emit_cache),
        grid=(n_batch // n_seq, N_KV_HEADS, nq),
        in_specs=in_specs,
        out_specs=out_specs,
        out_shape=out_shape,
        scratch_shapes=[pltpu.VMEM((n_seq * seq + (past if has_ctx else 0), HEAD_DIM), BF16)] * 2,
        compiler_params=_params(3),
        name="attention",
    )(*args)


def _retention_kernel(lg_ref, q_ref, k_ref, v_ref, g_ref, s0_ref, o_ref, sf_ref, decay_scr, *, seq, n_seq):
    h = pl.program_id(0)
    lg_f = lg_ref[h]
    lg_b = lg_ref[N_RET_HEADS + h]

    @pl.when(pl.program_id(1) == 0)
    def _():
        i = lax.broadcasted_iota(jnp.int32, (seq, seq), 0)
        j = lax.broadcasted_iota(jnp.int32, (seq, seq), 1)
        d = (i - j).astype(F32)
        decay = jnp.exp(jnp.where(d >= 0, lg_f, lg_b) * jnp.abs(d))
        decay_scr[...] = jnp.where(d == 0, 2.0, decay)

    pos = lax.broadcasted_iota(jnp.int32, (seq, 1), 0).astype(F32)
    q_decay_f, q_decay_b = jnp.exp(lg_f * (pos + 1.0)), jnp.exp(lg_b * (seq - pos))
    k_decay_f, k_decay_b = jnp.exp(lg_f * (seq - 1.0 - pos)), jnp.exp(lg_b * pos)
    ones = jnp.ones((1, HEAD_DIM), F32)
    s_decay_f, s_decay_b = jnp.exp(ones * (lg_f * seq)), jnp.exp(ones * (lg_b * seq))

    for s in range(n_seq):
        rows = slice(s * seq, (s + 1) * seq)
        q = q_ref[rows, :]
        k = k_ref[rows, :] * (HEAD_DIM ** -0.5)
        vb = v_ref[rows, :].astype(BF16)
        a = lax.dot_general(q.astype(BF16), k.astype(BF16), NT_DIMS, preferred_element_type=F32)
        o = _dot((a * decay_scr[...]).astype(BF16), vb)
        s0_f = s0_ref[s, 0]
        s0_b = s0_ref[s, 1]
        o = o + _dot((q * q_decay_f).astype(BF16), s0_f.astype(BF16))
        o = o + _dot((q * q_decay_b).astype(BF16), s0_b.astype(BF16))

        kf = (k * k_decay_f).astype(BF16)
        kb = (k * k_decay_b).astype(BF16)
        sf_ref[s, 0] = s_decay_f * s0_f + lax.dot_general(kf, vb, TN_DIMS, preferred_element_type=F32)
        sf_ref[s, 1] = s_decay_b * s0_b + lax.dot_general(kb, vb, TN_DIMS, preferred_element_type=F32)

        oc = o - jnp.mean(o, -1, keepdims=True)
        var = jnp.mean(oc * oc, -1, keepdims=True)
        o_ref[rows, :] = (_silu(g_ref[rows, :]) * (oc * lax.rsqrt(var + EPS))).astype(o_ref.dtype)


def _retention(proj, n_batch, seq, log_gamma, s0, layer):
    rows = proj.shape[0]
    n_seq = max(1, RET_ROWS // seq)
    hd = HEAD_DIM

    def col(offset):
        return pl.BlockSpec((n_seq * seq, hd), lambda h, b: (b, offset // hd + h))

    return pl.pallas_call(
        functools.partial(_retention_kernel, seq=seq, n_seq=n_seq),
        grid=(N_RET_HEADS, n_batch // n_seq),
        in_specs=[
            pl.BlockSpec(memory_space=pltpu.SMEM),
            col(COL_RQ), col(COL_RK), col(COL_RV), col(COL_RG),
            pl.BlockSpec((n_seq, None, 2, None, hd, hd), lambda h, b: (b, layer, 0, h, 0, 0)),
        ],
        out_specs=[
            pl.BlockSpec((n_seq * seq, hd), lambda h, b: (b, h)),
            pl.BlockSpec((n_seq, 2, None, hd, hd), lambda h, b: (b, 0, h, 0, 0)),
        ],
        out_shape=[
            jax.ShapeDtypeStruct((rows, RET_WIDTH), BF16),
            jax.ShapeDtypeStruct((n_batch, 2, N_RET_HEADS, hd, hd), F32),
        ],
        scratch_shapes=[pltpu.VMEM((seq, seq), F32)],
        compiler_params=_params(2),
        name="retention",
    )(log_gamma.reshape(2 * N_RET_HEADS), proj, proj, proj, proj, s0)


def _alt_sign(n):
    r = lax.broadcasted_iota(jnp.int32, (n, 1), 0)
    return jnp.where((r & 1) == 0, 1.0, -1.0).astype(F32)


def _filter_kernel(z_ref, w1_ref, b1_ref, fr_ref, w2_ref, b2_ref, w3f_ref, w3b_ref, dl_ref,
                   ch_ref, cl_ref, sh_ref, sl_ref, kr_ref, ki_ref, kn_ref, *, seq):
    freq = fr_ref[...]
    hdn = jnp.sin(freq * (_dot3(*_split(z_ref[...]), *_split(w1_ref[...])) + b1_ref[...]))
    hdn = jnp.sin(freq * (_dot3(*_split(hdn), *_split(w2_ref[...])) + b2_ref[...]))
    hdn_hi, hdn_lo = _split(hdn)

    pos = lax.broadcasted_iota(jnp.int32, (seq, 1), 0)
    t = pos.astype(F32) / max(seq - 1, 1)
    decay = jnp.exp(-t * dl_ref[...])
    h_fwd = _dot3(hdn_hi, hdn_lo, *_split(w3f_ref[...])) * decay
    h_bwd = _dot3(hdn_hi, hdn_lo, *_split(w3b_ref[...])) * decay

    h_bwd_shift = jnp.where(pos == 0, 0.0, _shift_rows(h_bwd, down=True))
    alt = _alt_sign(seq)
    kr = _dot3(ch_ref[...], cl_ref[...], *_split(h_fwd + h_bwd_shift)) + alt * h_bwd[seq - 1:seq, :]
    ki = _dot3(sh_ref[...], sl_ref[...], *_split(h_bwd_shift - h_fwd))
    wgt = jnp.where(pos == 0, 1.0, 2.0) * (0.5 / seq)
    kr_ref[...] = kr * wgt
    ki_ref[...] = ki * wgt
    kn_ref[...] = jnp.sum(alt * (h_fwd - h_bwd), axis=0, keepdims=True) * (0.5 / seq)


def _hyena_filters(seq, zfeat, w1, b1, freq, w2, b2, w3, abs_deltas, dft):
    w = HYENA_WIDTH
    const = lambda o: (0, 0)
    once = pl.Buffered(1)
    mat = pl.BlockSpec((seq, seq), const, pipeline_mode=once)
    vec = pl.BlockSpec((1, FILTER_PAD), const)
    return pl.pallas_call(
        functools.partial(_filter_kernel, seq=seq),
        grid=(HYENA_ORDER,),
        in_specs=[
            pl.BlockSpec((seq, FILTER_PAD), const),
            pl.BlockSpec((FILTER_PAD, FILTER_PAD), const), vec, vec,
            pl.BlockSpec((FILTER_PAD, FILTER_PAD), const), vec,
            pl.BlockSpec((FILTER_PAD, w), lambda o: (0, 2 * o)),
            pl.BlockSpec((FILTER_PAD, w), lambda o: (0, 2 * o + 1)),
            pl.BlockSpec((1, w), const),
            mat, mat, mat, mat,
        ],
        out_specs=[
            pl.BlockSpec((None, seq, w), lambda o: (o, 0, 0)),
            pl.BlockSpec((None, seq, w), lambda o: (o, 0, 0)),
            pl.BlockSpec((None, 1, w), lambda o: (o, 0, 0)),
        ],
        out_shape=[
            jax.ShapeDtypeStruct((HYENA_ORDER, seq, HYENA_WIDTH), F32),
            jax.ShapeDtypeStruct((HYENA_ORDER, seq, HYENA_WIDTH), F32),
            jax.ShapeDtypeStruct((HYENA_ORDER, 1, HYENA_WIDTH), F32),
        ],
        compiler_params=_params(1),
        name="hyena_filters",
    )(zfeat, w1, b1, freq, w2, b2, w3, w3, abs_deltas, *dft)


def _dwconv3_seq(x, w, pos, seq):
    prev = jnp.where(pos == 0, 0.0, _shift_rows(x, down=True))
    nxt = jnp.where(pos == seq - 1, 0.0, _shift_rows(x, down=False))
    return prev * w[0:1] + x * w[1:2] + nxt * w[2:3]


def _hyena_kernel(v_ref, x1_ref, x2_ref, cv_ref, c1_ref, c2_ref, bias_ref, kr_ref, ki_ref, kn_ref,
                  cos_ref, sin_ref, o_ref, *, seq, n_seq):
    pos = lax.broadcasted_iota(jnp.int32, (seq, 1), 0)
    alt = _alt_sign(seq)

    def dft(m_ref, a):
        return _dot(m_ref[...], a.astype(BF16))

    for s in range(n_seq):
        rows = slice(s * seq, (s + 1) * seq)
        z = _dwconv3_seq(v_ref[rows, :], cv_ref[...], pos, seq)
        for o, (x_ref, c_ref) in enumerate(((x1_ref, c1_ref), (x2_ref, c2_ref))):
            ur = dft(cos_ref, z)
            ui = dft(sin_ref, z)
            u_nyq = jnp.sum(alt * z, axis=0, keepdims=True)
            kr, ki = kr_ref[o], ki_ref[o]
            yr = ur * kr + ui * ki
            yi = ui * kr - ur * ki
            y = dft(cos_ref, yr) + dft(sin_ref, yi) + alt * (u_nyq * kn_ref[o])
            z = _dwconv3_seq(x_ref[rows, :], c_ref[...], pos, seq) * (y + z * bias_ref[o:o + 1, :])
        o_ref[rows, :] = z.astype(o_ref.dtype)


def _hyena(proj, n_batch, seq, hy_conv, hy_bias, spectra, dft):
    rows = proj.shape[0]
    w = HYENA_WIDTH
    kr, ki, kn = spectra
    once = pl.Buffered(1)
    n_seq = max(1, HYENA_ROWS // seq)

    def part(p):
        return pl.BlockSpec((n_seq * seq, w), lambda b: (b, COL_HY // w + p))

    def conv(p):
        return pl.BlockSpec((3, w), lambda b: (0, p), pipeline_mode=once)

    mat = pl.BlockSpec((seq, seq), lambda b: (0, 0), pipeline_mode=once)
    spec = pl.BlockSpec((HYENA_ORDER, seq, w), lambda b: (0, 0, 0), pipeline_mode=once)
    return pl.pallas_call(
        functools.partial(_hyena_kernel, seq=seq, n_seq=n_seq),
        grid=(n_batch // n_seq,),
        in_specs=[
            part(0), part(1), part(2), conv(0), conv(1), conv(2),
            pl.BlockSpec((HYENA_ORDER, w), lambda b: (0, 0), pipeline_mode=once),
            spec, spec,
            pl.BlockSpec((HYENA_ORDER, 1, w), lambda b: (0, 0, 0), pipeline_mode=once),
            mat, mat,
        ],
        out_specs=pl.BlockSpec((n_seq * seq, w), lambda b: (b, 0)),
        out_shape=jax.ShapeDtypeStruct((rows, w), BF16),
        compiler_params=_params(1),
        name="hyena",
    )(proj, proj, proj, hy_conv, hy_conv, hy_conv, hy_bias, kr, ki, kn, dft[0], dft[2])


def _outproj_kernel(att_ref, ret_ref, hy_ref, x_ref, g_ref, w_ref, lw_ref, lb_ref, o_ref):
    r0, r1 = ATT_WIDTH, ATT_WIDTH + RET_WIDTH
    for t0 in range(0, x_ref.shape[0], OUTPROJ_SUB):
        rows = slice(t0, t0 + OUTPROJ_SUB)
        mix = (_dot(att_ref[rows, :], w_ref[0:r0, :]) + _dot(ret_ref[rows, :], w_ref[r0:r1, :])
               + _dot(hy_ref[rows, :], w_ref[r1:, :]))
        y = DEEPNORM_ALPHA * x_ref[rows, :] + g_ref[...] * mix
        o_ref[rows, :] = _layer_norm(y, lw_ref[...], lb_ref[...])


def _outproj(att, ret, hyz, x, mod, w_out, layer, ln_w, ln_b):
    rows = x.shape[0]
    tm = OUTPROJ_TM
    tiles_per_mod = rows // mod.shape[0] // tm
    row = lambda width: pl.BlockSpec((tm, width), lambda i: (i, 0))
    vec = pl.BlockSpec((1, D_MODEL), lambda i: (0, 0))
    return pl.pallas_call(
        _outproj_kernel,
        grid=(rows // tm,),
        in_specs=[
            row(ATT_WIDTH), row(RET_WIDTH), row(HYENA_WIDTH), row(D_MODEL),
            pl.BlockSpec((None, 1, D_MODEL), lambda i: (i // tiles_per_mod, 0, 2)),
            pl.BlockSpec((None, D_MODEL, D_MODEL), lambda i: (layer, 0, 0), pipeline_mode=pl.Buffered(1)),
            vec, vec,
        ],
        out_specs=row(D_MODEL),
        out_shape=jax.ShapeDtypeStruct((rows, D_MODEL), F32),
        compiler_params=_params(1),
        name="outproj_ln",
    )(att, ret, hyz, x, mod, w_out, ln_w.reshape(1, D_MODEL), ln_b.reshape(1, D_MODEL))


def _ffn_kernel(x_ref, xp_ref, xn_ref, sh_ref, sc_ref, g_ref, wg_ref, wu_ref, cg_ref, cu_ref, wd_ref,
                lw_ref, lb_ref, o_ref, h_scr, *, seq, tm):
    i = pl.program_id(0)
    j = pl.program_id(1)

    @pl.when(j == 0)
    def _():
        sc = 1.0 + sc_ref[...]
        sh = sh_ref[...]
        h_scr[0:HALO, :] = (xp_ref[...] * sc + sh).astype(BF16)
        h_scr[HALO:HALO + tm, :] = (x_ref[...] * sc + sh).astype(BF16)
        h_scr[HALO + tm:, :] = (xn_ref[...] * sc + sh).astype(BF16)
        o_ref[...] = jnp.zeros_like(o_ref)

    pos = (i * tm + lax.broadcasted_iota(jnp.int32, (tm, 1), 0)) & (seq - 1)

    def up_conv(w_ref, c_ref):
        up = _dot(h_scr[...], w_ref[...])
        prev = _shift_rows(up, down=True)[HALO:HALO + tm]
        nxt = _shift_rows(up, down=False)[HALO:HALO + tm]
        c = c_ref[...]
        return (jnp.where(pos == 0, 0.0, prev) * c[0:1] + up[HALO:HALO + tm] * c[1:2]
                + jnp.where(pos == seq - 1, 0.0, nxt) * c[2:3])

    act = _silu(up_conv(wg_ref, cg_ref)) * up_conv(wu_ref, cu_ref)
    o_ref[...] += _dot(act.astype(BF16), wd_ref[...])

    @pl.when(j == pl.num_programs(1) - 1)
    def _():
        y = DEEPNORM_ALPHA * x_ref[...] + g_ref[...] * o_ref[...]
        o_ref[...] = _layer_norm(y, lw_ref[...], lb_ref[...])


def _ffn(x, seq, mod, w_up, ffn_conv, w_down, layer, ln_w, ln_b):
    rows = x.shape[0]
    tm, tf = FFN_TM, FFN_TF
    n_f = D_FF // tf
    tiles_per_mod = rows // mod.shape[0] // tm
    halo_per_tile = tm // HALO
    last_halo = rows // HALO - 1

    def mod_spec(chunk):
        return pl.BlockSpec((None, 1, D_MODEL), lambda i, j: (i // tiles_per_mod, 0, chunk))

    vec = pl.BlockSpec((1, D_MODEL), lambda i, j: (0, 0))
    return pl.pallas_call(
        functools.partial(_ffn_kernel, seq=seq, tm=tm),
        grid=(rows // tm, n_f),
        in_specs=[
            pl.BlockSpec((tm, D_MODEL), lambda i, j: (i, 0), pipeline_mode=pl.Buffered(1)),
            pl.BlockSpec((HALO, D_MODEL), lambda i, j: (jnp.maximum(i * halo_per_tile - 1, 0), 0)),
            pl.BlockSpec((HALO, D_MODEL), lambda i, j: (jnp.minimum((i + 1) * halo_per_tile, last_halo), 0)),
            mod_spec(3), mod_spec(4), mod_spec(5),
            pl.BlockSpec((None, D_MODEL, tf), lambda i, j: (layer, 0, j)),
            pl.BlockSpec((None, D_MODEL, tf), lambda i, j: (layer, 0, n_f + j)),
            pl.BlockSpec((None, 3, tf), lambda i, j: (layer, 0, j)),
            pl.BlockSpec((None, 3, tf), lambda i, j: (layer, 0, n_f + j)),
            pl.BlockSpec((None, tf, D_MODEL), lambda i, j: (layer, j, 0)),
            vec, vec,
        ],
        out_specs=pl.BlockSpec((tm, D_MODEL), lambda i, j: (i, 0)),
        out_shape=jax.ShapeDtypeStruct((rows, D_MODEL), F32),
        scratch_shapes=[pltpu.VMEM((tm + 2 * HALO, D_MODEL), BF16)],
        compiler_params=_params(2),
        name="ffn_ln",
    )(x, x, x, mod, mod, mod, w_up, w_up, ffn_conv, ffn_conv, w_down,
      ln_w.reshape(1, D_MODEL), ln_b.reshape(1, D_MODEL))


def _rope_tables(n_tokens):
    t = np.arange(n_tokens)
    n_freq = HEAD_DIM // 4
    inv_freq = ROPE_THETA ** (-np.arange(n_freq) / n_freq)
    ang = np.concatenate([(t // GRID_W)[:, None] * inv_freq[None], (t % GRID_W)[:, None] * inv_freq[None]], -1)
    cos, sin = np.cos(ang), np.sin(ang)
    return (jnp.asarray(np.concatenate([cos, cos], -1), F32), jnp.asarray(np.concatenate([-sin, sin], -1), F32))


def _dft_tables(seq):
    idx = np.arange(seq)
    ang = ((idx[:, None] * idx[None, :]) % (2 * seq)) * (math.pi / seq)
    return (*_split(jnp.asarray(np.cos(ang), F32)), *_split(jnp.asarray(np.sin(ang), F32)))


def _filter_features(seq):
    pos = np.arange(seq)
    t = pos / max(seq - 1, 1)
    ang = (2.0 * math.pi * pos / seq)[:, None] * np.linspace(1e-4, FILTER_BANDS - 1, FILTER_BANDS)[None, :]
    z = np.concatenate([t[:, None], np.cos(ang), -np.sin(ang)], -1)
    return jnp.asarray(np.pad(z, ((0, 0), (0, FILTER_PAD - FILTER_EMB))), F32)


def _pad_to(a, shape):
    return jnp.pad(a, [(0, s - d) for d, s in zip(a.shape, shape)])


def kernel(x_prompt, x_sample, cache_k, cache_v, state_ret, c, c_ctx, w_ada, b_ada, w_in, q_norm, k_norm, ret_decay, hy_conv, hf_w1, hf_b1, hf_freq, hf_w2, hf_b2, hf_w3, hy_bias, w_out, ln1_w, ln1_b, w_up, ffn_conv, w_down, ln2_w, ln2_b):
    n_ctx, seq_ctx, _ = x_prompt.shape
    n_lat, seq_lat, _ = x_sample.shape

    cond = _pad_to(jnp.concatenate([c_ctx[None, :], c], 0), (COND_ROWS, D_MODEL))
    mod_all = _ada(cond, w_ada, b_ada)

    w_in_b, w_out_b = w_in.astype(BF16), w_out.astype(BF16)
    w_up_b, w_down_b = w_up.astype(BF16), w_down.astype(BF16)
    log_gamma = jax.nn.log_sigmoid(ret_decay.astype(F32))
    abs_deltas = jnp.asarray(np.abs(np.linspace(math.log(HYENA_DECAY_TARGET) / HYENA_DECAY_PCT_MIN,
                                                math.log(HYENA_DECAY_TARGET) / HYENA_DECAY_PCT_MAX,
                                                HYENA_WIDTH))[None, :], F32)
    ret_zero = jnp.zeros((n_ctx, 1, 2, N_RET_HEADS, HEAD_DIM, HEAD_DIM), F32)

    groups = {
        "ctx": dict(n=n_ctx, seq=seq_ctx, rope=None, dft=_dft_tables(seq_ctx), feat=_filter_features(seq_ctx)),
        "lat": dict(n=n_lat, seq=seq_lat, rope=_rope_tables(seq_lat), dft=_dft_tables(seq_lat),
                    feat=_filter_features(seq_lat)),
    }
    xs = {"ctx": x_prompt.reshape(n_ctx * seq_ctx, D_MODEL), "lat": x_sample.reshape(n_lat * seq_lat, D_MODEL)}
    ks_out, vs_out, ss_out = [], [], []

    for l in range(DEPTH):
        mods = {"ctx": mod_all[l, 0:1, None, :], "lat": mod_all[l, 1:1 + n_lat, None, :]}
        pad2 = (FILTER_PAD, FILTER_PAD)
        filt_w = (_pad_to(hf_w1[l], pad2), _pad_to(hf_b1[l][None, :], (1, FILTER_PAD)),
                  _pad_to(hf_freq[l][None, :], (1, FILTER_PAD)), _pad_to(hf_w2[l], pad2),
                  _pad_to(hf_b2[l][None, :], (1, FILTER_PAD)), _pad_to(hf_w3[l], (FILTER_PAD, hf_w3.shape[-1])))
        for name, g in groups.items():
            n, seq, x, mod = g["n"], g["seq"], xs[name], mods[name]
            is_ctx = name == "ctx"
            proj = _inproj(x, mod, w_in_b, l)
            att_out = _attention(proj, n, seq, q_norm[l], k_norm[l], g["rope"],
                                 None if is_ctx else (cache_k, cache_v), l, emit_cache=is_ctx)
            ret, s_fin = _retention(proj, n, seq, log_gamma[l],
                                    ret_zero if is_ctx else state_ret, 0 if is_ctx else l)
            spectra = _hyena_filters(seq, g["feat"], *filt_w, abs_deltas, g["dft"])
            hyz = _hyena(proj, n, seq, hy_conv[l], hy_bias[l], spectra, g["dft"])
            if is_ctx:
                att, own_k, own_v = att_out
                ks_out.append(own_k)
                vs_out.append(own_v)
                ss_out.append(s_fin)
            else:
                att = att_out[0]
            x = _outproj(att, ret, hyz, x, mod, w_out_b, l, ln1_w[l], ln1_b[l])
            xs[name] = _ffn(x, seq, mod, w_up_b, ffn_conv, w_down_b, l, ln2_w[l], ln2_b[l])

    return (xs["ctx"].reshape(n_ctx, seq_ctx, D_MODEL), xs["lat"].reshape(n_lat, seq_lat, D_MODEL),
            jnp.stack(ks_out, axis=1), jnp.stack(vs_out, axis=1), jnp.stack(ss_out, axis=1))
```

```python
import functools
import math

import numpy as np
import jax
import jax.numpy as jnp
from jax import lax
from jax.experimental import pallas as pl
from jax.experimental.pallas import tpu as pltpu

F32 = jnp.float32
BF16 = jnp.bfloat16

D_MODEL = 2048
DEPTH = 2
GRID_W = 64
HEAD_DIM = 128
ATT_WIDTH = D_MODEL // 2
N_ATT_HEADS = ATT_WIDTH // HEAD_DIM
N_KV_HEADS = N_ATT_HEADS // 4
GQA_GROUP = N_ATT_HEADS // N_KV_HEADS
KV_WIDTH = N_KV_HEADS * HEAD_DIM
RET_WIDTH = D_MODEL // 4
N_RET_HEADS = RET_WIDTH // HEAD_DIM
HYENA_WIDTH = D_MODEL // 4
HYENA_ORDER = 2
IN_WIDTH = ATT_WIDTH + 2 * KV_WIDTH + 4 * RET_WIDTH + (HYENA_ORDER + 1) * HYENA_WIDTH
D_FF = 5632
ROPE_THETA = 10000.0
FILTER_BANDS = 16
FILTER_EMB = 1 + 2 * FILTER_BANDS
FILTER_HID = 64
HYENA_DECAY_TARGET = 1e-2
HYENA_DECAY_PCT_MIN = 0.3
HYENA_DECAY_PCT_MAX = 1.5
DEEPNORM_ALPHA = (2 * DEPTH) ** 0.25
EPS = 1e-6

COL_Q = 0
COL_K = ATT_WIDTH
COL_V = COL_K + KV_WIDTH
COL_RQ = COL_V + KV_WIDTH
COL_RK = COL_RQ + RET_WIDTH
COL_RV = COL_RK + RET_WIDTH
COL_RG = COL_RV + RET_WIDTH
COL_HY = COL_RG + RET_WIDTH

LANES = 128
HALO = 16
VMEM_LIMIT = 60 * 1024 * 1024
COND_ROWS = 16

ADA_TN = 1024
INPROJ_TM, INPROJ_TN = 256, 512
OUTPROJ_TM, OUTPROJ_SUB = 512, 256
FFN_TM, FFN_TF = 1024, 512
ATT_TQ, ATT_CHAIN = 512, 256
ATT_ROWS = 1024
RET_ROWS = 1024
HYENA_ROWS = 1024
FILTER_PAD = 128

NT_DIMS = (((1,), (1,)), ((), ()))
TN_DIMS = (((0,), (0,)), ((), ()))


def _params(n_grid):
    return pltpu.CompilerParams(dimension_semantics=("arbitrary",) * n_grid,
                                vmem_limit_bytes=VMEM_LIMIT)


def _dot(a, b):
    return jnp.dot(a, b, preferred_element_type=F32)


def _split(a):
    hi = a.astype(BF16)
    lo = (a - hi.astype(F32)).astype(BF16)
    return hi, lo


def _dot3(a_hi, a_lo, b_hi, b_lo):
    return _dot(a_hi, b_hi) + _dot(a_lo, b_hi) + _dot(a_hi, b_lo)


def _silu(x):
    return x * jax.nn.sigmoid(x)


def _layer_norm(y, w, b):
    yc = y - jnp.mean(y, -1, keepdims=True)
    var = jnp.mean(yc * yc, -1, keepdims=True)
    return yc * lax.rsqrt(var + EPS) * w + b


def _shift_rows(x, down):
    n = x.shape[0]
    return pltpu.roll(x, 1 if down else n - 1, axis=0)


def _ada_kernel(cond_ref, w_ref, b_ref, o_ref):
    a = _silu(cond_ref[...]).astype(BF16)
    o_ref[...] = _dot(a, w_ref[...].astype(BF16)) + b_ref[...]


def _ada(cond, w_ada, b_ada):
    n_out = w_ada.shape[-1]
    return pl.pallas_call(
        _ada_kernel,
        grid=(DEPTH, n_out // ADA_TN),
        in_specs=[
            pl.BlockSpec((COND_ROWS, D_MODEL), lambda l, j: (0, 0)),
            pl.BlockSpec((None, D_MODEL, ADA_TN), lambda l, j: (l, 0, j)),
            pl.BlockSpec((None, 1, ADA_TN), lambda l, j: (l, 0, j)),
        ],
        out_specs=pl.BlockSpec((None, COND_ROWS, ADA_TN), lambda l, j: (l, 0, j)),
        out_shape=jax.ShapeDtypeStruct((DEPTH, COND_ROWS, n_out), F32),
        compiler_params=_params(2),
        name="ada",
    )(cond, w_ada, b_ada.reshape(DEPTH, 1, n_out))


def _inproj_kernel(x_ref, sh_ref, sc_ref, w_ref, o_ref):
    h = (x_ref[...] * (1.0 + sc_ref[...]) + sh_ref[...]).astype(BF16)
    for n0 in range(0, IN_WIDTH, INPROJ_TN):
        o_ref[:, n0:n0 + INPROJ_TN] = _dot(h, w_ref[:, n0:n0 + INPROJ_TN])


def _inproj(x, mod, w_in, layer):
    rows = x.shape[0]
    tm = INPROJ_TM
    tiles_per_mod = rows // mod.shape[0] // tm
    return pl.pallas_call(
        _inproj_kernel,
        grid=(rows // tm,),
        in_specs=[
            pl.BlockSpec((tm, D_MODEL), lambda i: (i, 0)),
            pl.BlockSpec((None, 1, D_MODEL), lambda i: (i // tiles_per_mod, 0, 0)),
            pl.BlockSpec((None, 1, D_MODEL), lambda i: (i // tiles_per_mod, 0, 1)),
            pl.BlockSpec((None, D_MODEL, IN_WIDTH), lambda i: (layer, 0, 0), pipeline_mode=pl.Buffered(1)),
        ],
        out_specs=pl.BlockSpec((tm, IN_WIDTH), lambda i: (i, 0)),
        out_shape=jax.ShapeDtypeStruct((rows, IN_WIDTH), F32),
        compiler_params=_params(1),
        name="inproj",
    )(x, mod, mod, w_in)


def _rms_norm(x, w):
    return x * lax.rsqrt(jnp.mean(x * x, -1, keepdims=True) + EPS) * w


def _rope(x, cos_full, sin_signed):
    return x * cos_full + pltpu.roll(x, HEAD_DIM // 2, axis=1) * sin_signed


def _attn_kernel(*refs, seq, n_seq, rope, has_ctx, emit_cache):
    refs = list(refs)
    q_ref, k_ref, v_ref, qw_ref, kw_ref = refs[:5]
    del refs[:5]
    if rope:
        qcos_ref, qsin_ref, kcos_ref, ksin_ref = refs[:4]
        del refs[:4]
    if has_ctx:
        ck_ref, cv_ref = refs[:2]
        del refs[:2]
    o_ref = refs.pop(0)
    if emit_cache:
        ownk_ref, ownv_ref = refs[:2]
        del refs[:2]
    k_scr, v_scr = refs
    own = n_seq * seq

    @pl.when(pl.program_id(2) == 0)
    def _():
        kn = _rms_norm(k_ref[...], kw_ref[...])
        if emit_cache:
            for s in range(n_seq):
                ownk_ref[s] = kn[s * seq:(s + 1) * seq]
                ownv_ref[s] = v_ref[s * seq:(s + 1) * seq, :]
        if rope:
            kn = _rope(kn, kcos_ref[...], ksin_ref[...])
        k_scr[0:own, :] = kn.astype(BF16)
        v_scr[0:own, 0:HEAD_DIM] = v_ref[...].astype(BF16)
        if has_ctx:
            k_scr[own:, :] = ck_ref[...].astype(BF16)
            v_scr[own:, 0:HEAD_DIM] = cv_ref[...].astype(BF16)
        v_scr[:, HEAD_DIM:] = jnp.ones((v_scr.shape[0], HEAD_DIM), BF16)

    tq = q_ref.shape[0] // n_seq
    for s in range(n_seq):
        keys = slice(s * seq, (s + 1) * seq) if n_seq > 1 else slice(None)
        for g in range(GQA_GROUP):
            cols = slice(g * HEAD_DIM, (g + 1) * HEAD_DIM)
            for r0 in range(s * tq, (s + 1) * tq, ATT_CHAIN):
                rows = slice(r0, r0 + ATT_CHAIN)
                qn = _rms_norm(q_ref[rows, cols], qw_ref[...])
                if rope:
                    qn = _rope(qn, qcos_ref[rows, :], qsin_ref[rows, :])
                qb = (qn * (HEAD_DIM ** -0.5)).astype(BF16)
                sc = lax.dot_general(qb, k_scr[keys, :], NT_DIMS, preferred_element_type=F32)
                p = jnp.exp(sc - jnp.max(sc, -1, keepdims=True))
                o = _dot(p.astype(BF16), v_scr[keys, :])
                o_ref[rows, cols] = (o[:, :HEAD_DIM] / o[:, HEAD_DIM:HEAD_DIM + 1]).astype(o_ref.dtype)


def _attention(proj, n_batch, seq, q_norm, k_norm, rope_tabs, ctx_kv, layer, emit_cache):
    rows = proj.shape[0]
    tq = min(ATT_TQ, seq)
    nq = seq // tq
    group_w = GQA_GROUP * HEAD_DIM
    rope = rope_tabs is not None
    has_ctx = ctx_kv is not None
    n_seq = max(1, ATT_ROWS // seq) if not (rope or has_ctx) else 1

    in_specs = [
        pl.BlockSpec((n_seq * tq, group_w), lambda b, kv, qi: (b * nq + qi, COL_Q // group_w + kv)),
        pl.BlockSpec((n_seq * seq, HEAD_DIM), lambda b, kv, qi: (b, COL_K // HEAD_DIM + kv)),
        pl.BlockSpec((n_seq * seq, HEAD_DIM), lambda b, kv, qi: (b, COL_V // HEAD_DIM + kv)),
        pl.BlockSpec((1, HEAD_DIM), lambda b, kv, qi: (0, 0)),
        pl.BlockSpec((1, HEAD_DIM), lambda b, kv, qi: (0, 0)),
    ]
    args = [proj, proj, proj, q_norm.reshape(1, HEAD_DIM), k_norm.reshape(1, HEAD_DIM)]
    if rope:
        cos_full, sin_signed = rope_tabs
        in_specs += [
            pl.BlockSpec((tq, HEAD_DIM), lambda b, kv, qi: (qi, 0)),
            pl.BlockSpec((tq, HEAD_DIM), lambda b, kv, qi: (qi, 0)),
            pl.BlockSpec((seq, HEAD_DIM), lambda b, kv, qi: (0, 0)),
            pl.BlockSpec((seq, HEAD_DIM), lambda b, kv, qi: (0, 0)),
        ]
        args += [cos_full, sin_signed, cos_full, sin_signed]
    if has_ctx:
        ck, cv = ctx_kv
        past = ck.shape[3]
        ctx_spec = pl.BlockSpec((None, None, None, past, HEAD_DIM), lambda b, kv, qi: (b, layer, kv, 0, 0))
        in_specs += [ctx_spec, ctx_spec]
        args += [ck, cv]

    out_specs = [pl.BlockSpec((n_seq * tq, group_w), lambda b, kv, qi: (b * nq + qi, kv))]
    out_shape = [jax.ShapeDtypeStruct((rows, ATT_WIDTH), BF16)]
    if emit_cache:
        own_spec = pl.BlockSpec((n_seq, None, seq, HEAD_DIM), lambda b, kv, qi: (b, kv, 0, 0))
        out_specs += [own_spec, own_spec]
        out_shape += [jax.ShapeDtypeStruct((n_batch, N_KV_HEADS, seq, HEAD_DIM), F32)] * 2

    return pl.pallas_call(
        functools.partial(_attn_kernel, seq=seq, n_seq=n_seq, rope=rope, has_ctx=has_ctx, emit_cache=emit_cache),
        grid=(n_batch // n_seq, N_KV_HEADS, nq),
        in_specs=in_specs,
        out_specs=out_specs,
        out_shape=out_shape,
        scratch_shapes=[pltpu.VMEM((n_seq * seq + (past if has_ctx else 0), width), BF16)
                        for width in (HEAD_DIM, 2 * HEAD_DIM)],
        compiler_params=_params(3),
        name="attention",
    )(*args)


def _retention_kernel(lg_ref, q_ref, k_ref, v_ref, g_ref, s0_ref, o_ref, sf_ref, decay_scr, *, seq, n_seq):
    h = pl.program_id(0)
    lg_f = lg_ref[h]
    lg_b = lg_ref[N_RET_HEADS + h]

    @pl.when(pl.program_id(1) == 0)
    def _():
        i = lax.broadcasted_iota(jnp.int32, (seq, seq), 0)
        j = lax.broadcasted_iota(jnp.int32, (seq, seq), 1)
        d = (i - j).astype(F32)
        decay = jnp.exp(jnp.where(d >= 0, lg_f, lg_b) * jnp.abs(d))
        decay_scr[...] = jnp.where(d == 0, 2.0, decay)

    pos = lax.broadcasted_iota(jnp.int32, (seq, 1), 0).astype(F32)
    q_decay_f, q_decay_b = jnp.exp(lg_f * (pos + 1.0)), jnp.exp(lg_b * (seq - pos))
    k_decay_f, k_decay_b = jnp.exp(lg_f * (seq - 1.0 - pos)), jnp.exp(lg_b * pos)
    ones = jnp.ones((1, HEAD_DIM), F32)
    s_decay_f, s_decay_b = jnp.exp(ones * (lg_f * seq)), jnp.exp(ones * (lg_b * seq))

    for s in range(n_seq):
        rows = slice(s * seq, (s + 1) * seq)
        q = q_ref[rows, :]
        k = k_ref[rows, :] * (HEAD_DIM ** -0.5)
        vb = v_ref[rows, :].astype(BF16)
        a = lax.dot_general(q.astype(BF16), k.astype(BF16), NT_DIMS, preferred_element_type=F32)
        o = _dot((a * decay_scr[...]).astype(BF16), vb)
        s0_f = s0_ref[s, 0]
        s0_b = s0_ref[s, 1]
        o = o + _dot((q * q_decay_f).astype(BF16), s0_f.astype(BF16))
        o = o + _dot((q * q_decay_b).astype(BF16), s0_b.astype(BF16))

        kf = (k * k_decay_f).astype(BF16)
        kb = (k * k_decay_b).astype(BF16)
        sf_ref[s, 0] = s_decay_f * s0_f + lax.dot_general(kf, vb, TN_DIMS, preferred_element_type=F32)
        sf_ref[s, 1] = s_decay_b * s0_b + lax.dot_general(kb, vb, TN_DIMS, preferred_element_type=F32)

        oc = o - jnp.mean(o, -1, keepdims=True)
        var = jnp.mean(oc * oc, -1, keepdims=True)
        o_ref[rows, :] = (_silu(g_ref[rows, :]) * (oc * lax.rsqrt(var + EPS))).astype(o_ref.dtype)


def _retention(proj, n_batch, seq, log_gamma, s0, layer):
    rows = proj.shape[0]
    n_seq = max(1, RET_ROWS // seq)
    hd = HEAD_DIM

    def col(offset):
        return pl.BlockSpec((n_seq * seq, hd), lambda h, b: (b, offset // hd + h))

    return pl.pallas_call(
        functools.partial(_retention_kernel, seq=seq, n_seq=n_seq),
        grid=(N_RET_HEADS, n_batch // n_seq),
        in_specs=[
            pl.BlockSpec(memory_space=pltpu.SMEM),
            col(COL_RQ), col(COL_RK), col(COL_RV), col(COL_RG),
            pl.BlockSpec((n_seq, None, 2, None, hd, hd), lambda h, b: (b, layer, 0, h, 0, 0)),
        ],
        out_specs=[
            pl.BlockSpec((n_seq * seq, hd), lambda h, b: (b, h)),
            pl.BlockSpec((n_seq, 2, None, hd, hd), lambda h, b: (b, 0, h, 0, 0)),
        ],
        out_shape=[
            jax.ShapeDtypeStruct((rows, RET_WIDTH), BF16),
            jax.ShapeDtypeStruct((n_batch, 2, N_RET_HEADS, hd, hd), F32),
        ],
        scratch_shapes=[pltpu.VMEM((seq, seq), F32)],
        compiler_params=_params(2),
        name="retention",
    )(log_gamma.reshape(2 * N_RET_HEADS), proj, proj, proj, proj, s0)


def _alt_sign(n):
    r = lax.broadcasted_iota(jnp.int32, (n, 1), 0)
    return jnp.where((r & 1) == 0, 1.0, -1.0).astype(F32)


def _filter_kernel(z_ref, w1_ref, b1_ref, fr_ref, w2_ref, b2_ref, w3f_ref, w3b_ref, dl_ref,
                   ch_ref, cl_ref, sh_ref, sl_ref, kr_ref, ki_ref, kn_ref, *, seq):
    freq = fr_ref[...]
    hdn = jnp.sin(freq * (_dot3(*_split(z_ref[...]), *_split(w1_ref[...])) + b1_ref[...]))
    hdn = jnp.sin(freq * (_dot3(*_split(hdn), *_split(w2_ref[...])) + b2_ref[...]))
    hdn_hi, hdn_lo = _split(hdn)

    pos = lax.broadcasted_iota(jnp.int32, (seq, 1), 0)
    t = pos.astype(F32) / max(seq - 1, 1)
    decay = jnp.exp(-t * dl_ref[...])
    h_fwd = _dot3(hdn_hi, hdn_lo, *_split(w3f_ref[...])) * decay
    h_bwd = _dot3(hdn_hi, hdn_lo, *_split(w3b_ref[...])) * decay

    h_bwd_shift = jnp.where(pos == 0, 0.0, _shift_rows(h_bwd, down=True))
    alt = _alt_sign(seq)
    kr = _dot3(ch_ref[...], cl_ref[...], *_split(h_fwd + h_bwd_shift)) + alt * h_bwd[seq - 1:seq, :]
    ki = _dot3(sh_ref[...], sl_ref[...], *_split(h_bwd_shift - h_fwd))
    wgt = jnp.where(pos == 0, 1.0, 2.0) * (0.5 / seq)
    kr_ref[...] = kr * wgt
    ki_ref[...] = ki * wgt
    kn_ref[...] = jnp.sum(alt * (h_fwd - h_bwd), axis=0, keepdims=True) * (0.5 / seq)


def _hyena_filters(seq, zfeat, w1, b1, freq, w2, b2, w3, abs_deltas, dft):
    w = HYENA_WIDTH
    const = lambda o: (0, 0)
    once = pl.Buffered(1)
    mat = pl.BlockSpec((seq, seq), const, pipeline_mode=once)
    vec = pl.BlockSpec((1, FILTER_PAD), const)
    return pl.pallas_call(
        functools.partial(_filter_kernel, seq=seq),
        grid=(HYENA_ORDER,),
        in_specs=[
            pl.BlockSpec((seq, FILTER_PAD), const),
            pl.BlockSpec((FILTER_PAD, FILTER_PAD), const), vec, vec,
            pl.BlockSpec((FILTER_PAD, FILTER_PAD), const), vec,
            pl.BlockSpec((FILTER_PAD, w), lambda o: (0, 2 * o)),
            pl.BlockSpec((FILTER_PAD, w), lambda o: (0, 2 * o + 1)),
            pl.BlockSpec((1, w), const),
            mat, mat, mat, mat,
        ],
        out_specs=[
            pl.BlockSpec((None, seq, w), lambda o: (o, 0, 0)),
            pl.BlockSpec((None, seq, w), lambda o: (o, 0, 0)),
            pl.BlockSpec((None, 1, w), lambda o: (o, 0, 0)),
        ],
        out_shape=[
            jax.ShapeDtypeStruct((HYENA_ORDER, seq, HYENA_WIDTH), F32),
            jax.ShapeDtypeStruct((HYENA_ORDER, seq, HYENA_WIDTH), F32),
            jax.ShapeDtypeStruct((HYENA_ORDER, 1, HYENA_WIDTH), F32),
        ],
        compiler_params=_params(1),
        name="hyena_filters",
    )(zfeat, w1, b1, freq, w2, b2, w3, w3, abs_deltas, *dft)


def _dwconv3_seq(x, w, pos, seq):
    prev = jnp.where(pos == 0, 0.0, _shift_rows(x, down=True))
    nxt = jnp.where(pos == seq - 1, 0.0, _shift_rows(x, down=False))
    return prev * w[0:1] + x * w[1:2] + nxt * w[2:3]


def _hyena_kernel(v_ref, x1_ref, x2_ref, cv_ref, c1_ref, c2_ref, bias_ref, kr_ref, ki_ref, kn_ref,
                  cos_ref, sin_ref, o_ref, *, seq, n_seq):
    pos = lax.broadcasted_iota(jnp.int32, (seq, 1), 0)
    alt = _alt_sign(seq)

    def dft(m_ref, a):
        return _dot(m_ref[...], a.astype(BF16))

    for s in range(n_seq):
        rows = slice(s * seq, (s + 1) * seq)
        z = _dwconv3_seq(v_ref[rows, :], cv_ref[...], pos, seq)
        for o, (x_ref, c_ref) in enumerate(((x1_ref, c1_ref), (x2_ref, c2_ref))):
            ur = dft(cos_ref, z)
            ui = dft(sin_ref, z)
            u_nyq = jnp.sum(alt * z, axis=0, keepdims=True)
            kr, ki = kr_ref[o], ki_ref[o]
            yr = ur * kr + ui * ki
            yi = ui * kr - ur * ki
            y = dft(cos_ref, yr) + dft(sin_ref, yi) + alt * (u_nyq * kn_ref[o])
            z = _dwconv3_seq(x_ref[rows, :], c_ref[...], pos, seq) * (y + z * bias_ref[o:o + 1, :])
        o_ref[rows, :] = z.astype(o_ref.dtype)


def _hyena(proj, n_batch, seq, hy_conv, hy_bias, spectra, dft):
    rows = proj.shape[0]
    w = HYENA_WIDTH
    kr, ki, kn = spectra
    once = pl.Buffered(1)
    n_seq = max(1, HYENA_ROWS // seq)

    def part(p):
        return pl.BlockSpec((n_seq * seq, w), lambda b: (b, COL_HY // w + p))

    def conv(p):
        return pl.BlockSpec((3, w), lambda b: (0, p), pipeline_mode=once)

    mat = pl.BlockSpec((seq, seq), lambda b: (0, 0), pipeline_mode=once)
    spec = pl.BlockSpec((HYENA_ORDER, seq, w), lambda b: (0, 0, 0), pipeline_mode=once)
    return pl.pallas_call(
        functools.partial(_hyena_kernel, seq=seq, n_seq=n_seq),
        grid=(n_batch // n_seq,),
        in_specs=[
            part(0), part(1), part(2), conv(0), conv(1), conv(2),
            pl.BlockSpec((HYENA_ORDER, w), lambda b: (0, 0), pipeline_mode=once),
            spec, spec,
            pl.BlockSpec((HYENA_ORDER, 1, w), lambda b: (0, 0, 0), pipeline_mode=once),
            mat, mat,
        ],
        out_specs=pl.BlockSpec((n_seq * seq, w), lambda b: (b, 0)),
        out_shape=jax.ShapeDtypeStruct((rows, w), BF16),
        compiler_params=_params(1),
        name="hyena",
    )(proj, proj, proj, hy_conv, hy_conv, hy_conv, hy_bias, kr, ki, kn, dft[0], dft[2])


def _outproj_kernel(att_ref, ret_ref, hy_ref, x_ref, g_ref, w_ref, lw_ref, lb_ref, o_ref):
    r0, r1 = ATT_WIDTH, ATT_WIDTH + RET_WIDTH
    for t0 in range(0, x_ref.shape[0], OUTPROJ_SUB):
        rows = slice(t0, t0 + OUTPROJ_SUB)
        mix = (_dot(att_ref[rows, :], w_ref[0:r0, :]) + _dot(ret_ref[rows, :], w_ref[r0:r1, :])
               + _dot(hy_ref[rows, :], w_ref[r1:, :]))
        y = DEEPNORM_ALPHA * x_ref[rows, :] + g_ref[...] * mix
        o_ref[rows, :] = _layer_norm(y, lw_ref[...], lb_ref[...])


def _outproj(att, ret, hyz, x, mod, w_out, layer, ln_w, ln_b):
    rows = x.shape[0]
    tm = OUTPROJ_TM
    tiles_per_mod = rows // mod.shape[0] // tm
    row = lambda width: pl.BlockSpec((tm, width), lambda i: (i, 0))
    vec = pl.BlockSpec((1, D_MODEL), lambda i: (0, 0))
    return pl.pallas_call(
        _outproj_kernel,
        grid=(rows // tm,),
        in_specs=[
            row(ATT_WIDTH), row(RET_WIDTH), row(HYENA_WIDTH), row(D_MODEL),
            pl.BlockSpec((None, 1, D_MODEL), lambda i: (i // tiles_per_mod, 0, 2)),
            pl.BlockSpec((None, D_MODEL, D_MODEL), lambda i: (layer, 0, 0), pipeline_mode=pl.Buffered(1)),
            vec, vec,
        ],
        out_specs=row(D_MODEL),
        out_shape=jax.ShapeDtypeStruct((rows, D_MODEL), F32),
        compiler_params=_params(1),
        name="outproj_ln",
    )(att, ret, hyz, x, mod, w_out, ln_w.reshape(1, D_MODEL), ln_b.reshape(1, D_MODEL))


def _ffn_kernel(x_ref, xp_ref, xn_ref, sh_ref, sc_ref, g_ref, wg_ref, wu_ref, cg_ref, cu_ref, wd_ref,
                lw_ref, lb_ref, o_ref, h_scr, *, seq, tm):
    i = pl.program_id(0)
    j = pl.program_id(1)
    ends_inside = seq < tm

    @pl.when(j == 0)
    def _():
        sc = 1.0 + sc_ref[...]
        sh = sh_ref[...]
        h_prev = xp_ref[...] * sc + sh
        h_next = xn_ref[...] * sc + sh
        if not ends_inside:
            h_prev = jnp.where(((i * tm) & (seq - 1)) == 0, 0.0, h_prev)
            h_next = jnp.where((((i + 1) * tm) & (seq - 1)) == 0, 0.0, h_next)
        h_scr[0:HALO, :] = h_prev.astype(BF16)
        h_scr[HALO:HALO + tm, :] = (x_ref[...] * sc + sh).astype(BF16)
        h_scr[HALO + tm:, :] = h_next.astype(BF16)
        o_ref[...] = jnp.zeros_like(o_ref)

    pos = (i * tm + lax.broadcasted_iota(jnp.int32, (tm, 1), 0)) & (seq - 1)

    def up_conv(w_ref, c_ref):
        up = _dot(h_scr[...], w_ref[...])
        prev = _shift_rows(up, down=True)[HALO:HALO + tm]
        nxt = _shift_rows(up, down=False)[HALO:HALO + tm]
        if ends_inside:
            prev = jnp.where(pos == 0, 0.0, prev)
            nxt = jnp.where(pos == seq - 1, 0.0, nxt)
        c = c_ref[...]
        return prev * c[0:1] + up[HALO:HALO + tm] * c[1:2] + nxt * c[2:3]

    act = _silu(up_conv(wg_ref, cg_ref)) * up_conv(wu_ref, cu_ref)
    o_ref[...] += _dot(act.astype(BF16), wd_ref[...])

    @pl.when(j == pl.num_programs(1) - 1)
    def _():
        y = DEEPNORM_ALPHA * x_ref[...] + g_ref[...] * o_ref[...]
        o_ref[...] = _layer_norm(y, lw_ref[...], lb_ref[...])


def _ffn(x, seq, mod, w_up, ffn_conv, w_down, layer, ln_w, ln_b):
    rows = x.shape[0]
    tm, tf = FFN_TM, FFN_TF
    n_f = D_FF // tf
    tiles_per_mod = rows // mod.shape[0] // tm
    halo_per_tile = tm // HALO
    last_halo = rows // HALO - 1

    def mod_spec(chunk):
        return pl.BlockSpec((None, 1, D_MODEL), lambda i, j: (i // tiles_per_mod, 0, chunk))

    vec = pl.BlockSpec((1, D_MODEL), lambda i, j: (0, 0))
    return pl.pallas_call(
        functools.partial(_ffn_kernel, seq=seq, tm=tm),
        grid=(rows // tm, n_f),
        in_specs=[
            pl.BlockSpec((tm, D_MODEL), lambda i, j: (i, 0), pipeline_mode=pl.Buffered(1)),
            pl.BlockSpec((HALO, D_MODEL), lambda i, j: (jnp.maximum(i * halo_per_tile - 1, 0), 0)),
            pl.BlockSpec((HALO, D_MODEL), lambda i, j: (jnp.minimum((i + 1) * halo_per_tile, last_halo), 0)),
            mod_spec(3), mod_spec(4), mod_spec(5),
            pl.BlockSpec((None, D_MODEL, tf), lambda i, j: (layer, 0, j)),
            pl.BlockSpec((None, D_MODEL, tf), lambda i, j: (layer, 0, n_f + j)),
            pl.BlockSpec((None, 3, tf), lambda i, j: (layer, 0, j)),
            pl.BlockSpec((None, 3, tf), lambda i, j: (layer, 0, n_f + j)),
            pl.BlockSpec((None, tf, D_MODEL), lambda i, j: (layer, j, 0)),
            vec, vec,
        ],
        out_specs=pl.BlockSpec((tm, D_MODEL), lambda i, j: (i, 0)),
        out_shape=jax.ShapeDtypeStruct((rows, D_MODEL), F32),
        scratch_shapes=[pltpu.VMEM((tm + 2 * HALO, D_MODEL), BF16)],
        compiler_params=_params(2),
        name="ffn_ln",
    )(x, x, x, mod, mod, mod, w_up, w_up, ffn_conv, ffn_conv, w_down,
      ln_w.reshape(1, D_MODEL), ln_b.reshape(1, D_MODEL))


def _rope_tables(n_tokens):
    t = np.arange(n_tokens)
    n_freq = HEAD_DIM // 4
    inv_freq = ROPE_THETA ** (-np.arange(n_freq) / n_freq)
    ang = np.concatenate([(t // GRID_W)[:, None] * inv_freq[None], (t % GRID_W)[:, None] * inv_freq[None]], -1)
    cos, sin = np.cos(ang), np.sin(ang)
    return (jnp.asarray(np.concatenate([cos, cos], -1), F32), jnp.asarray(np.concatenate([-sin, sin], -1), F32))


def _dft_tables(seq):
    idx = np.arange(seq)
    ang = ((idx[:, None] * idx[None, :]) % (2 * seq)) * (math.pi / seq)
    return (*_split(jnp.asarray(np.cos(ang), F32)), *_split(jnp.asarray(np.sin(ang), F32)))


def _filter_features(seq):
    pos = np.arange(seq)
    t = pos / max(seq - 1, 1)
    ang = (2.0 * math.pi * pos / seq)[:, None] * np.linspace(1e-4, FILTER_BANDS - 1, FILTER_BANDS)[None, :]
    z = np.concatenate([t[:, None], np.cos(ang), -np.sin(ang)], -1)
    return jnp.asarray(np.pad(z, ((0, 0), (0, FILTER_PAD - FILTER_EMB))), F32)


def _pad_to(a, shape):
    return jnp.pad(a, [(0, s - d) for d, s in zip(a.shape, shape)])


def kernel(x_prompt, x_sample, cache_k, cache_v, state_ret, c, c_ctx, w_ada, b_ada, w_in, q_norm, k_norm, ret_decay, hy_conv, hf_w1, hf_b1, hf_freq, hf_w2, hf_b2, hf_w3, hy_bias, w_out, ln1_w, ln1_b, w_up, ffn_conv, w_down, ln2_w, ln2_b):
    n_ctx, seq_ctx, _ = x_prompt.shape
    n_lat, seq_lat, _ = x_sample.shape

    cond = _pad_to(jnp.concatenate([c_ctx[None, :], c], 0), (COND_ROWS, D_MODEL))
    mod_all = _ada(cond, w_ada, b_ada)

    w_in_b, w_out_b = w_in.astype(BF16), w_out.astype(BF16)
    w_up_b, w_down_b = w_up.astype(BF16), w_down.astype(BF16)
    log_gamma = jax.nn.log_sigmoid(ret_decay.astype(F32))
    abs_deltas = jnp.asarray(np.abs(np.linspace(math.log(HYENA_DECAY_TARGET) / HYENA_DECAY_PCT_MIN,
                                                math.log(HYENA_DECAY_TARGET) / HYENA_DECAY_PCT_MAX,
                                                HYENA_WIDTH))[None, :], F32)
    ret_zero = jnp.zeros((n_ctx, 1, 2, N_RET_HEADS, HEAD_DIM, HEAD_DIM), F32)

    groups = {
        "ctx": dict(n=n_ctx, seq=seq_ctx, rope=None, dft=_dft_tables(seq_ctx), feat=_filter_features(seq_ctx)),
        "lat": dict(n=n_lat, seq=seq_lat, rope=_rope_tables(seq_lat), dft=_dft_tables(seq_lat),
                    feat=_filter_features(seq_lat)),
    }
    xs = {"ctx": x_prompt.reshape(n_ctx * seq_ctx, D_MODEL), "lat": x_sample.reshape(n_lat * seq_lat, D_MODEL)}
    ks_out, vs_out, ss_out = [], [], []

    for l in range(DEPTH):
        mods = {"ctx": mod_all[l, 0:1, None, :], "lat": mod_all[l, 1:1 + n_lat, None, :]}
        pad2 = (FILTER_PAD, FILTER_PAD)
        filt_w = (_pad_to(hf_w1[l], pad2), _pad_to(hf_b1[l][None, :], (1, FILTER_PAD)),
                  _pad_to(hf_freq[l][None, :], (1, FILTER_PAD)), _pad_to(hf_w2[l], pad2),
                  _pad_to(hf_b2[l][None, :], (1, FILTER_PAD)), _pad_to(hf_w3[l], (FILTER_PAD, hf_w3.shape[-1])))
        for name, g in groups.items():
            n, seq, x, mod = g["n"], g["seq"], xs[name], mods[name]
            is_ctx = name == "ctx"
            proj = _inproj(x, mod, w_in_b, l)
            att_out = _attention(proj, n, seq, q_norm[l], k_norm[l], g["rope"],
                                 None if is_ctx else (cache_k, cache_v), l, emit_cache=is_ctx)
            ret, s_fin = _retention(proj, n, seq, log_gamma[l],
                                    ret_zero if is_ctx else state_ret, 0 if is_ctx else l)
            spectra = _hyena_filters(seq, g["feat"], *filt_w, abs_deltas, g["dft"])
            hyz = _hyena(proj, n, seq, hy_conv[l], hy_bias[l], spectra, g["dft"])
            if is_ctx:
                att, own_k, own_v = att_out
                ks_out.append(own_k)
                vs_out.append(own_v)
                ss_out.append(s_fin)
            else:
                att = att_out[0]
            x = _outproj(att, ret, hyz, x, mod, w_out_b, l, ln1_w[l], ln1_b[l])
            xs[name] = _ffn(x, seq, mod, w_up_b, ffn_conv, w_down_b, l, ln2_w[l], ln2_b[l])

    return (xs["ctx"].reshape(n_ctx, seq_ctx, D_MODEL), xs["lat"].reshape(n_lat, seq_lat, D_MODEL),
            jnp.stack(ks_out, axis=1), jnp.stack(vs_out, axis=1), jnp.stack(ss_out, axis=1))
```

```python
import functools
import math

import numpy as np
import jax
import jax.numpy as jnp
from jax import lax
from jax.experimental import pallas as pl
from jax.experimental.pallas import tpu as pltpu

F32 = jnp.float32
BF16 = jnp.bfloat16

D_MODEL = 2048
DEPTH = 2
GRID_W = 64
HEAD_DIM = 128
ATT_WIDTH = D_MODEL // 2
N_ATT_HEADS = ATT_WIDTH // HEAD_DIM
N_KV_HEADS = N_ATT_HEADS // 4
GQA_GROUP = N_ATT_HEADS // N_KV_HEADS
KV_WIDTH = N_KV_HEADS * HEAD_DIM
RET_WIDTH = D_MODEL // 4
N_RET_HEADS = RET_WIDTH // HEAD_DIM
HYENA_WIDTH = D_MODEL // 4
HYENA_ORDER = 2
IN_WIDTH = ATT_WIDTH + 2 * KV_WIDTH + 4 * RET_WIDTH + (HYENA_ORDER + 1) * HYENA_WIDTH
D_FF = 5632
ROPE_THETA = 10000.0
FILTER_BANDS = 16
FILTER_EMB = 1 + 2 * FILTER_BANDS
FILTER_HID = 64
HYENA_DECAY_TARGET = 1e-2
HYENA_DECAY_PCT_MIN = 0.3
HYENA_DECAY_PCT_MAX = 1.5
DEEPNORM_ALPHA = (2 * DEPTH) ** 0.25
EPS = 1e-6

COL_Q = 0
COL_K = ATT_WIDTH
COL_V = COL_K + KV_WIDTH
COL_RQ = COL_V + KV_WIDTH
COL_RK = COL_RQ + RET_WIDTH
COL_RV = COL_RK + RET_WIDTH
COL_RG = COL_RV + RET_WIDTH
COL_HY = COL_RG + RET_WIDTH

LANES = 128
HALO = 16
VMEM_LIMIT = 60 * 1024 * 1024
COND_ROWS = 16

ADA_TN = 1024
INPROJ_TM, INPROJ_TN = 512, 512
OUTPROJ_TM, OUTPROJ_SUB = 1024, 256
FFN_TM, FFN_TF = 1024, 512
ATT_TQ, ATT_CHAIN = 1024, 256
ATT_ROWS = 1024
RET_ROWS = 2048
HYENA_ROWS = 1024
FILTER_PAD = 128

NT_DIMS = (((1,), (1,)), ((), ()))
TN_DIMS = (((0,), (0,)), ((), ()))


def _params(n_grid):
    return pltpu.CompilerParams(dimension_semantics=("arbitrary",) * n_grid,
                                vmem_limit_bytes=VMEM_LIMIT)


def _dot(a, b):
    return jnp.dot(a, b, preferred_element_type=F32)


def _split(a):
    hi = a.astype(BF16)
    lo = (a - hi.astype(F32)).astype(BF16)
    return hi, lo


def _dot3(a_hi, a_lo, b_hi, b_lo):
    return _dot(a_hi, b_hi) + _dot(a_lo, b_hi) + _dot(a_hi, b_lo)


def _silu(x):
    return x * jax.nn.sigmoid(x)


def _layer_norm(y, w, b):
    yc = y - jnp.mean(y, -1, keepdims=True)
    var = jnp.mean(yc * yc, -1, keepdims=True)
    return yc * lax.rsqrt(var + EPS) * w + b


def _shift_rows(x, down):
    n = x.shape[0]
    return pltpu.roll(x, 1 if down else n - 1, axis=0)


def _ada_kernel(cond_ref, w_ref, b_ref, o_ref):
    a = _silu(cond_ref[...]).astype(BF16)
    o_ref[...] = _dot(a, w_ref[...].astype(BF16)) + b_ref[...]


def _ada(cond, w_ada, b_ada):
    n_out = w_ada.shape[-1]
    return pl.pallas_call(
        _ada_kernel,
        grid=(DEPTH, n_out // ADA_TN),
        in_specs=[
            pl.BlockSpec((COND_ROWS, D_MODEL), lambda l, j: (0, 0)),
            pl.BlockSpec((None, D_MODEL, ADA_TN), lambda l, j: (l, 0, j)),
            pl.BlockSpec((None, 1, ADA_TN), lambda l, j: (l, 0, j)),
        ],
        out_specs=pl.BlockSpec((None, COND_ROWS, ADA_TN), lambda l, j: (l, 0, j)),
        out_shape=jax.ShapeDtypeStruct((DEPTH, COND_ROWS, n_out), F32),
        compiler_params=_params(2),
        name="ada",
    )(cond, w_ada, b_ada.reshape(DEPTH, 1, n_out))


def _inproj_kernel(x_ref, sh_ref, sc_ref, w_ref, o_ref):
    h = (x_ref[...] * (1.0 + sc_ref[...]) + sh_ref[...]).astype(BF16)
    for n0 in range(0, IN_WIDTH, INPROJ_TN):
        o_ref[:, n0:n0 + INPROJ_TN] = _dot(h, w_ref[:, n0:n0 + INPROJ_TN])


def _inproj(x, mod, w_in, layer):
    rows = x.shape[0]
    tm = INPROJ_TM
    tiles_per_mod = rows // mod.shape[0] // tm
    return pl.pallas_call(
        _inproj_kernel,
        grid=(rows // tm,),
        in_specs=[
            pl.BlockSpec((tm, D_MODEL), lambda i: (i, 0)),
            pl.BlockSpec((None, 1, D_MODEL), lambda i: (i // tiles_per_mod, 0, 0)),
            pl.BlockSpec((None, 1, D_MODEL), lambda i: (i // tiles_per_mod, 0, 1)),
            pl.BlockSpec((None, D_MODEL, IN_WIDTH), lambda i: (layer, 0, 0), pipeline_mode=pl.Buffered(1)),
        ],
        out_specs=pl.BlockSpec((tm, IN_WIDTH), lambda i: (i, 0)),
        out_shape=jax.ShapeDtypeStruct((rows, IN_WIDTH), F32),
        compiler_params=_params(1),
        name="inproj",
    )(x, mod, mod, w_in)


def _rms_norm(x, w):
    return x * lax.rsqrt(jnp.mean(x * x, -1, keepdims=True) + EPS) * w


def _rope(x, cos_full, sin_signed):
    return x * cos_full + pltpu.roll(x, HEAD_DIM // 2, axis=1) * sin_signed


def _attn_kernel(*refs, seq, n_seq, rope, has_ctx, emit_cache):
    refs = list(refs)
    q_ref, k_ref, v_ref, qw_ref, kw_ref = refs[:5]
    del refs[:5]
    if rope:
        qcos_ref, qsin_ref, kcos_ref, ksin_ref = refs[:4]
        del refs[:4]
    if has_ctx:
        ck_ref, cv_ref = refs[:2]
        del refs[:2]
    o_ref = refs.pop(0)
    if emit_cache:
        ownk_ref, ownv_ref = refs[:2]
        del refs[:2]
    k_scr, v_scr = refs
    own = n_seq * seq

    @pl.when(pl.program_id(2) == 0)
    def _():
        kn = _rms_norm(k_ref[...], kw_ref[...])
        if emit_cache:
            for s in range(n_seq):
                ownk_ref[s] = kn[s * seq:(s + 1) * seq]
                ownv_ref[s] = v_ref[s * seq:(s + 1) * seq, :]
        if rope:
            kn = _rope(kn, kcos_ref[...], ksin_ref[...])
        k_scr[0:own, :] = kn.astype(BF16)
        v_scr[0:own, 0:HEAD_DIM] = v_ref[...].astype(BF16)
        if has_ctx:
            k_scr[own:, :] = ck_ref[...].astype(BF16)
            v_scr[own:, 0:HEAD_DIM] = cv_ref[...].astype(BF16)
        v_scr[:, HEAD_DIM:] = jnp.ones((v_scr.shape[0], HEAD_DIM), BF16)

    tq = q_ref.shape[0] // n_seq
    for s in range(n_seq):
        keys = slice(s * seq, (s + 1) * seq) if n_seq > 1 else slice(None)
        for g in range(GQA_GROUP):
            cols = slice(g * HEAD_DIM, (g + 1) * HEAD_DIM)
            for r0 in range(s * tq, (s + 1) * tq, ATT_CHAIN):
                rows = slice(r0, r0 + ATT_CHAIN)
                qn = _rms_norm(q_ref[rows, cols], qw_ref[...])
                if rope:
                    qn = _rope(qn, qcos_ref[rows, :], qsin_ref[rows, :])
                qb = (qn * (HEAD_DIM ** -0.5)).astype(BF16)
                sc = lax.dot_general(qb, k_scr[keys, :], NT_DIMS, preferred_element_type=F32)
                p = jnp.exp(sc - jnp.max(sc, -1, keepdims=True))
                o = _dot(p.astype(BF16), v_scr[keys, :])
                o_ref[rows, cols] = (o[:, :HEAD_DIM] / o[:, HEAD_DIM:HEAD_DIM + 1]).astype(o_ref.dtype)


def _attention(proj, n_batch, seq, q_norm, k_norm, rope_tabs, ctx_kv, layer, emit_cache):
    rows = proj.shape[0]
    tq = min(ATT_TQ, seq)
    nq = seq // tq
    group_w = GQA_GROUP * HEAD_DIM
    rope = rope_tabs is not None
    has_ctx = ctx_kv is not None
    n_seq = max(1, ATT_ROWS // seq) if not (rope or has_ctx) else 1

    in_specs = [
        pl.BlockSpec((n_seq * tq, group_w), lambda b, kv, qi: (b * nq + qi, COL_Q // group_w + kv)),
        pl.BlockSpec((n_seq * seq, HEAD_DIM), lambda b, kv, qi: (b, COL_K // HEAD_DIM + kv)),
        pl.BlockSpec((n_seq * seq, HEAD_DIM), lambda b, kv, qi: (b, COL_V // HEAD_DIM + kv)),
        pl.BlockSpec((1, HEAD_DIM), lambda b, kv, qi: (0, 0)),
        pl.BlockSpec((1, HEAD_DIM), lambda b, kv, qi: (0, 0)),
    ]
    args = [proj, proj, proj, q_norm.reshape(1, HEAD_DIM), k_norm.reshape(1, HEAD_DIM)]
    if rope:
        cos_full, sin_signed = rope_tabs
        in_specs += [
            pl.BlockSpec((tq, HEAD_DIM), lambda b, kv, qi: (qi, 0)),
            pl.BlockSpec((tq, HEAD_DIM), lambda b, kv, qi: (qi, 0)),
            pl.BlockSpec((seq, HEAD_DIM), lambda b, kv, qi: (0, 0)),
            pl.BlockSpec((seq, HEAD_DIM), lambda b, kv, qi: (0, 0)),
        ]
        args += [cos_full, sin_signed, cos_full, sin_signed]
    if has_ctx:
        ck, cv = ctx_kv
        past = ck.shape[3]
        ctx_spec = pl.BlockSpec((None, None, None, past, HEAD_DIM), lambda b, kv, qi: (b, layer, kv, 0, 0))
        in_specs += [ctx_spec, ctx_spec]
        args += [ck, cv]

    out_specs = [pl.BlockSpec((n_seq * tq, group_w), lambda b, kv, qi: (b * nq + qi, kv))]
    out_shape = [jax.ShapeDtypeStruct((rows, ATT_WIDTH), BF16)]
    if emit_cache:
        own_spec = pl.BlockSpec((n_seq, None, seq, HEAD_DIM), lambda b, kv, qi: (b, kv, 0, 0))
        out_specs += [own_spec, own_spec]
        out_shape += [jax.ShapeDtypeStruct((n_batch, N_KV_HEADS, seq, HEAD_DIM), F32)] * 2

    return pl.pallas_call(
        functools.partial(_attn_kernel, seq=seq, n_seq=n_seq, rope=rope, has_ctx=has_ctx, emit_cache=emit_cache),
        grid=(n_batch // n_seq, N_KV_HEADS, nq),
        in_specs=in_specs,
        out_specs=out_specs,
        out_shape=out_shape,
        scratch_shapes=[pltpu.VMEM((n_seq * seq + (past if has_ctx else 0), width), BF16)
                        for width in (HEAD_DIM, 2 * HEAD_DIM)],
        compiler_params=_params(3),
        name="attention",
    )(*args)


def _retention_kernel(lg_ref, q_ref, k_ref, v_ref, g_ref, s0_ref, o_ref, sf_ref, decay_scr, *, seq, n_seq):
    h = pl.program_id(0)
    lg_f = lg_ref[h]
    lg_b = lg_ref[N_RET_HEADS + h]

    @pl.when(pl.program_id(1) == 0)
    def _():
        i = lax.broadcasted_iota(jnp.int32, (seq, seq), 0)
        j = lax.broadcasted_iota(jnp.int32, (seq, seq), 1)
        d = (i - j).astype(F32)
        decay = jnp.exp(jnp.where(d >= 0, lg_f, lg_b) * jnp.abs(d))
        decay_scr[...] = jnp.where(d == 0, 2.0, decay)

    pos = lax.broadcasted_iota(jnp.int32, (seq, 1), 0).astype(F32)
    q_decay_f, q_decay_b = jnp.exp(lg_f * (pos + 1.0)), jnp.exp(lg_b * (seq - pos))
    k_decay_f, k_decay_b = jnp.exp(lg_f * (seq - 1.0 - pos)), jnp.exp(lg_b * pos)
    ones = jnp.ones((1, HEAD_DIM), F32)
    s_decay_f, s_decay_b = jnp.exp(ones * (lg_f * seq)), jnp.exp(ones * (lg_b * seq))

    for s in range(n_seq):
        rows = slice(s * seq, (s + 1) * seq)
        q = q_ref[rows, :]
        k = k_ref[rows, :] * (HEAD_DIM ** -0.5)
        vb = v_ref[rows, :].astype(BF16)
        a = lax.dot_general(q.astype(BF16), k.astype(BF16), NT_DIMS, preferred_element_type=F32)
        o = _dot((a * decay_scr[...]).astype(BF16), vb)
        s0_f = s0_ref[s, 0]
        s0_b = s0_ref[s, 1]
        o = o + _dot((q * q_decay_f).astype(BF16), s0_f.astype(BF16))
        o = o + _dot((q * q_decay_b).astype(BF16), s0_b.astype(BF16))

        kf = (k * k_decay_f).astype(BF16)
        kb = (k * k_decay_b).astype(BF16)
        sf_ref[s, 0] = s_decay_f * s0_f + lax.dot_general(kf, vb, TN_DIMS, preferred_element_type=F32)
        sf_ref[s, 1] = s_decay_b * s0_b + lax.dot_general(kb, vb, TN_DIMS, preferred_element_type=F32)

        oc = o - jnp.mean(o, -1, keepdims=True)
        var = jnp.mean(oc * oc, -1, keepdims=True)
        o_ref[rows, :] = (_silu(g_ref[rows, :]) * (oc * lax.rsqrt(var + EPS))).astype(o_ref.dtype)


def _retention(proj, n_batch, seq, log_gamma, s0, layer):
    rows = proj.shape[0]
    n_seq = max(1, RET_ROWS // seq)
    hd = HEAD_DIM

    def col(offset):
        return pl.BlockSpec((n_seq * seq, hd), lambda h, b: (b, offset // hd + h))

    return pl.pallas_call(
        functools.partial(_retention_kernel, seq=seq, n_seq=n_seq),
        grid=(N_RET_HEADS, n_batch // n_seq),
        in_specs=[
            pl.BlockSpec(memory_space=pltpu.SMEM),
            col(COL_RQ), col(COL_RK), col(COL_RV), col(COL_RG),
            pl.BlockSpec((n_seq, None, 2, None, hd, hd), lambda h, b: (b, layer, 0, h, 0, 0)),
        ],
        out_specs=[
            pl.BlockSpec((n_seq * seq, hd), lambda h, b: (b, h)),
            pl.BlockSpec((n_seq, 2, None, hd, hd), lambda h, b: (b, 0, h, 0, 0)),
        ],
        out_shape=[
            jax.ShapeDtypeStruct((rows, RET_WIDTH), BF16),
            jax.ShapeDtypeStruct((n_batch, 2, N_RET_HEADS, hd, hd), F32),
        ],
        scratch_shapes=[pltpu.VMEM((seq, seq), F32)],
        compiler_params=_params(2),
        name="retention",
    )(log_gamma.reshape(2 * N_RET_HEADS), proj, proj, proj, proj, s0)


def _alt_sign(n):
    r = lax.broadcasted_iota(jnp.int32, (n, 1), 0)
    return jnp.where((r & 1) == 0, 1.0, -1.0).astype(F32)


def _filter_kernel(z_ref, w1_ref, b1_ref, fr_ref, w2_ref, b2_ref, w3f_ref, w3b_ref, dl_ref,
                   ch_ref, cl_ref, sh_ref, sl_ref, kr_ref, ki_ref, kn_ref, *, seq):
    freq = fr_ref[...]
    hdn = jnp.sin(freq * (_dot3(*_split(z_ref[...]), *_split(w1_ref[...])) + b1_ref[...]))
    hdn = jnp.sin(freq * (_dot3(*_split(hdn), *_split(w2_ref[...])) + b2_ref[...]))
    hdn_hi, hdn_lo = _split(hdn)

    pos = lax.broadcasted_iota(jnp.int32, (seq, 1), 0)
    t = pos.astype(F32) / max(seq - 1, 1)
    decay = jnp.exp(-t * dl_ref[...])
    h_fwd = _dot3(hdn_hi, hdn_lo, *_split(w3f_ref[...])) * decay
    h_bwd = _dot3(hdn_hi, hdn_lo, *_split(w3b_ref[...])) * decay

    h_bwd_shift = jnp.where(pos == 0, 0.0, _shift_rows(h_bwd, down=True))
    alt = _alt_sign(seq)
    kr = _dot3(ch_ref[...], cl_ref[...], *_split(h_fwd + h_bwd_shift)) + alt * h_bwd[seq - 1:seq, :]
    ki = _dot3(sh_ref[...], sl_ref[...], *_split(h_bwd_shift - h_fwd))
    wgt = jnp.where(pos == 0, 1.0, 2.0) * (0.5 / seq)
    kr_ref[...] = kr * wgt
    ki_ref[...] = ki * wgt
    kn_ref[...] = jnp.sum(alt * (h_fwd - h_bwd), axis=0, keepdims=True) * (0.5 / seq)


def _hyena_filters(seq, zfeat, w1, b1, freq, w2, b2, w3, abs_deltas, dft):
    w = HYENA_WIDTH
    const = lambda o: (0, 0)
    once = pl.Buffered(1)
    mat = pl.BlockSpec((seq, seq), const, pipeline_mode=once)
    vec = pl.BlockSpec((1, FILTER_PAD), const)
    return pl.pallas_call(
        functools.partial(_filter_kernel, seq=seq),
        grid=(HYENA_ORDER,),
        in_specs=[
            pl.BlockSpec((seq, FILTER_PAD), const),
            pl.BlockSpec((FILTER_PAD, FILTER_PAD), const), vec, vec,
            pl.BlockSpec((FILTER_PAD, FILTER_PAD), const), vec,
            pl.BlockSpec((FILTER_PAD, w), lambda o: (0, 2 * o)),
            pl.BlockSpec((FILTER_PAD, w), lambda o: (0, 2 * o + 1)),
            pl.BlockSpec((1, w), const),
            mat, mat, mat, mat,
        ],
        out_specs=[
            pl.BlockSpec((None, seq, w), lambda o: (o, 0, 0)),
            pl.BlockSpec((None, seq, w), lambda o: (o, 0, 0)),
            pl.BlockSpec((None, 1, w), lambda o: (o, 0, 0)),
        ],
        out_shape=[
            jax.ShapeDtypeStruct((HYENA_ORDER, seq, HYENA_WIDTH), F32),
            jax.ShapeDtypeStruct((HYENA_ORDER, seq, HYENA_WIDTH), F32),
            jax.ShapeDtypeStruct((HYENA_ORDER, 1, HYENA_WIDTH), F32),
        ],
        compiler_params=_params(1),
        name="hyena_filters",
    )(zfeat, w1, b1, freq, w2, b2, w3, w3, abs_deltas, *dft)


def _dwconv3_seq(x, w, pos, seq):
    prev = jnp.where(pos == 0, 0.0, _shift_rows(x, down=True))
    nxt = jnp.where(pos == seq - 1, 0.0, _shift_rows(x, down=False))
    return prev * w[0:1] + x * w[1:2] + nxt * w[2:3]


def _hyena_kernel(v_ref, x1_ref, x2_ref, cv_ref, c1_ref, c2_ref, bias_ref, kr_ref, ki_ref, kn_ref,
                  cos_ref, sin_ref, o_ref, *, seq, n_seq):
    pos = lax.broadcasted_iota(jnp.int32, (seq, 1), 0)
    alt = _alt_sign(seq)

    def dft(m_ref, a):
        return _dot(m_ref[...], a.astype(BF16))

    for s in range(n_seq):
        rows = slice(s * seq, (s + 1) * seq)
        z = _dwconv3_seq(v_ref[rows, :], cv_ref[...], pos, seq)
        for o, (x_ref, c_ref) in enumerate(((x1_ref, c1_ref), (x2_ref, c2_ref))):
            ur = dft(cos_ref, z)
            ui = dft(sin_ref, z)
            u_nyq = jnp.sum(alt * z, axis=0, keepdims=True)
            kr, ki = kr_ref[o], ki_ref[o]
            yr = ur * kr + ui * ki
            yi = ui * kr - ur * ki
            y = dft(cos_ref, yr) + dft(sin_ref, yi) + alt * (u_nyq * kn_ref[o])
            z = _dwconv3_seq(x_ref[rows, :], c_ref[...], pos, seq) * (y + z * bias_ref[o:o + 1, :])
        o_ref[rows, :] = z.astype(o_ref.dtype)


def _hyena(proj, n_batch, seq, hy_conv, hy_bias, spectra, dft):
    rows = proj.shape[0]
    w = HYENA_WIDTH
    kr, ki, kn = spectra
    once = pl.Buffered(1)
    n_seq = max(1, HYENA_ROWS // seq)

    def part(p):
        return pl.BlockSpec((n_seq * seq, w), lambda b: (b, COL_HY // w + p))

    def conv(p):
        return pl.BlockSpec((3, w), lambda b: (0, p), pipeline_mode=once)

    mat = pl.BlockSpec((seq, seq), lambda b: (0, 0), pipeline_mode=once)
    spec = pl.BlockSpec((HYENA_ORDER, seq, w), lambda b: (0, 0, 0), pipeline_mode=once)
    return pl.pallas_call(
        functools.partial(_hyena_kernel, seq=seq, n_seq=n_seq),
        grid=(n_batch // n_seq,),
        in_specs=[
            part(0), part(1), part(2), conv(0), conv(1), conv(2),
            pl.BlockSpec((HYENA_ORDER, w), lambda b: (0, 0), pipeline_mode=once),
            spec, spec,
            pl.BlockSpec((HYENA_ORDER, 1, w), lambda b: (0, 0, 0), pipeline_mode=once),
            mat, mat,
        ],
        out_specs=pl.BlockSpec((n_seq * seq, w), lambda b: (b, 0)),
        out_shape=jax.ShapeDtypeStruct((rows, w), BF16),
        compiler_params=_params(1),
        name="hyena",
    )(proj, proj, proj, hy_conv, hy_conv, hy_conv, hy_bias, kr, ki, kn, dft[0], dft[2])


def _outproj_kernel(att_ref, ret_ref, hy_ref, x_ref, g_ref, w_ref, lw_ref, lb_ref, o_ref):
    r0, r1 = ATT_WIDTH, ATT_WIDTH + RET_WIDTH
    for t0 in range(0, x_ref.shape[0], OUTPROJ_SUB):
        rows = slice(t0, t0 + OUTPROJ_SUB)
        mix = (_dot(att_ref[rows, :], w_ref[0:r0, :]) + _dot(ret_ref[rows, :], w_ref[r0:r1, :])
               + _dot(hy_ref[rows, :], w_ref[r1:, :]))
        y = DEEPNORM_ALPHA * x_ref[rows, :] + g_ref[...] * mix
        o_ref[rows, :] = _layer_norm(y, lw_ref[...], lb_ref[...])


def _outproj(att, ret, hyz, x, mod, w_out, layer, ln_w, ln_b):
    rows = x.shape[0]
    tm = OUTPROJ_TM
    tiles_per_mod = rows // mod.shape[0] // tm
    row = lambda width: pl.BlockSpec((tm, width), lambda i: (i, 0))
    vec = pl.BlockSpec((1, D_MODEL), lambda i: (0, 0))
    return pl.pallas_call(
        _outproj_kernel,
        grid=(rows // tm,),
        in_specs=[
            row(ATT_WIDTH), row(RET_WIDTH), row(HYENA_WIDTH), row(D_MODEL),
            pl.BlockSpec((None, 1, D_MODEL), lambda i: (i // tiles_per_mod, 0, 2)),
            pl.BlockSpec((None, D_MODEL, D_MODEL), lambda i: (layer, 0, 0), pipeline_mode=pl.Buffered(1)),
            vec, vec,
        ],
        out_specs=row(D_MODEL),
        out_shape=jax.ShapeDtypeStruct((rows, D_MODEL), F32),
        compiler_params=_params(1),
        name="outproj_ln",
    )(att, ret, hyz, x, mod, w_out, ln_w.reshape(1, D_MODEL), ln_b.reshape(1, D_MODEL))


def _ffn_kernel(x_ref, xp_ref, xn_ref, sh_ref, sc_ref, g_ref, wg_ref, wu_ref, cg_ref, cu_ref, wd_ref,
                lw_ref, lb_ref, o_ref, h_scr, *, seq, tm):
    i = pl.program_id(0)
    j = pl.program_id(1)
    ends_inside = seq < tm

    @pl.when(j == 0)
    def _():
        sc = 1.0 + sc_ref[...]
        sh = sh_ref[...]
        h_prev = xp_ref[...] * sc + sh
        h_next = xn_ref[...] * sc + sh
        if not ends_inside:
            h_prev = jnp.where(((i * tm) & (seq - 1)) == 0, 0.0, h_prev)
            h_next = jnp.where((((i + 1) * tm) & (seq - 1)) == 0, 0.0, h_next)
        h_scr[0:HALO, :] = h_prev.astype(BF16)
        h_scr[HALO:HALO + tm, :] = (x_ref[...] * sc + sh).astype(BF16)
        h_scr[HALO + tm:, :] = h_next.astype(BF16)
        o_ref[...] = jnp.zeros_like(o_ref)

    pos = (i * tm + lax.broadcasted_iota(jnp.int32, (tm, 1), 0)) & (seq - 1)

    def up_conv(w_ref, c_ref):
        up = _dot(h_scr[...], w_ref[...])
        prev = _shift_rows(up, down=True)[HALO:HALO + tm]
        nxt = _shift_rows(up, down=False)[HALO:HALO + tm]
        if ends_inside:
            prev = jnp.where(pos == 0, 0.0, prev)
            nxt = jnp.where(pos == seq - 1, 0.0, nxt)
        c = c_ref[...]
        return prev * c[0:1] + up[HALO:HALO + tm] * c[1:2] + nxt * c[2:3]

    act = _silu(up_conv(wg_ref, cg_ref)) * up_conv(wu_ref, cu_ref)
    o_ref[...] += _dot(act.astype(BF16), wd_ref[...])

    @pl.when(j == pl.num_programs(1) - 1)
    def _():
        y = DEEPNORM_ALPHA * x_ref[...] + g_ref[...] * o_ref[...]
        o_ref[...] = _layer_norm(y, lw_ref[...], lb_ref[...])


def _ffn(x, seq, mod, w_up, ffn_conv, w_down, layer, ln_w, ln_b):
    rows = x.shape[0]
    tm, tf = FFN_TM, FFN_TF
    n_f = D_FF // tf
    tiles_per_mod = rows // mod.shape[0] // tm
    halo_per_tile = tm // HALO
    last_halo = rows // HALO - 1

    def mod_spec(chunk):
        return pl.BlockSpec((None, 1, D_MODEL), lambda i, j: (i // tiles_per_mod, 0, chunk))

    vec = pl.BlockSpec((1, D_MODEL), lambda i, j: (0, 0))
    return pl.pallas_call(
        functools.partial(_ffn_kernel, seq=seq, tm=tm),
        grid=(rows // tm, n_f),
        in_specs=[
            pl.BlockSpec((tm, D_MODEL), lambda i, j: (i, 0), pipeline_mode=pl.Buffered(1)),
            pl.BlockSpec((HALO, D_MODEL), lambda i, j: (jnp.maximum(i * halo_per_tile - 1, 0), 0)),
            pl.BlockSpec((HALO, D_MODEL), lambda i, j: (jnp.minimum((i + 1) * halo_per_tile, last_halo), 0)),
            mod_spec(3), mod_spec(4), mod_spec(5),
            pl.BlockSpec((None, D_MODEL, tf), lambda i, j: (layer, 0, j)),
            pl.BlockSpec((None, D_MODEL, tf), lambda i, j: (layer, 0, n_f + j)),
            pl.BlockSpec((None, 3, tf), lambda i, j: (layer, 0, j)),
            pl.BlockSpec((None, 3, tf), lambda i, j: (layer, 0, n_f + j)),
            pl.BlockSpec((None, tf, D_MODEL), lambda i, j: (layer, j, 0)),
            vec, vec,
        ],
        out_specs=pl.BlockSpec((tm, D_MODEL), lambda i, j: (i, 0)),
        out_shape=jax.ShapeDtypeStruct((rows, D_MODEL), F32),
        scratch_shapes=[pltpu.VMEM((tm + 2 * HALO, D_MODEL), BF16)],
        compiler_params=_params(2),
        name="ffn_ln",
    )(x, x, x, mod, mod, mod, w_up, w_up, ffn_conv, ffn_conv, w_down,
      ln_w.reshape(1, D_MODEL), ln_b.reshape(1, D_MODEL))


def _rope_tables(n_tokens):
    t = np.arange(n_tokens)
    n_freq = HEAD_DIM // 4
    inv_freq = ROPE_THETA ** (-np.arange(n_freq) / n_freq)
    ang = np.concatenate([(t // GRID_W)[:, None] * inv_freq[None], (t % GRID_W)[:, None] * inv_freq[None]], -1)
    cos, sin = np.cos(ang), np.sin(ang)
    return (jnp.asarray(np.concatenate([cos, cos], -1), F32), jnp.asarray(np.concatenate([-sin, sin], -1), F32))


def _dft_tables(seq):
    idx = np.arange(seq)
    ang = ((idx[:, None] * idx[None, :]) % (2 * seq)) * (math.pi / seq)
    return (*_split(jnp.asarray(np.cos(ang), F32)), *_split(jnp.asarray(np.sin(ang), F32)))


def _filter_features(seq):
    pos = np.arange(seq)
    t = pos / max(seq - 1, 1)
    ang = (2.0 * math.pi * pos / seq)[:, None] * np.linspace(1e-4, FILTER_BANDS - 1, FILTER_BANDS)[None, :]
    z = np.concatenate([t[:, None], np.cos(ang), -np.sin(ang)], -1)
    return jnp.asarray(np.pad(z, ((0, 0), (0, FILTER_PAD - FILTER_EMB))), F32)


def _pad_to(a, shape):
    return jnp.pad(a, [(0, s - d) for d, s in zip(a.shape, shape)])


def kernel(x_prompt, x_sample, cache_k, cache_v, state_ret, c, c_ctx, w_ada, b_ada, w_in, q_norm, k_norm, ret_decay, hy_conv, hf_w1, hf_b1, hf_freq, hf_w2, hf_b2, hf_w3, hy_bias, w_out, ln1_w, ln1_b, w_up, ffn_conv, w_down, ln2_w, ln2_b):
    n_ctx, seq_ctx, _ = x_prompt.shape
    n_lat, seq_lat, _ = x_sample.shape

    cond = _pad_to(jnp.concatenate([c_ctx[None, :], c], 0), (COND_ROWS, D_MODEL))
    mod_all = _ada(cond, w_ada, b_ada)

    w_in_b, w_out_b = w_in.astype(BF16), w_out.astype(BF16)
    w_up_b, w_down_b = w_up.astype(BF16), w_down.astype(BF16)
    log_gamma = jax.nn.log_sigmoid(ret_decay.astype(F32))
    abs_deltas = jnp.asarray(np.abs(np.linspace(math.log(HYENA_DECAY_TARGET) / HYENA_DECAY_PCT_MIN,
                                                math.log(HYENA_DECAY_TARGET) / HYENA_DECAY_PCT_MAX,
                                                HYENA_WIDTH))[None, :], F32)
    ret_zero = jnp.zeros((n_ctx, 1, 2, N_RET_HEADS, HEAD_DIM, HEAD_DIM), F32)

    groups = {
        "ctx": dict(n=n_ctx, seq=seq_ctx, rope=None, dft=_dft_tables(seq_ctx), feat=_filter_features(seq_ctx)),
        "lat": dict(n=n_lat, seq=seq_lat, rope=_rope_tables(seq_lat), dft=_dft_tables(seq_lat),
                    feat=_filter_features(seq_lat)),
    }
    xs = {"ctx": x_prompt.reshape(n_ctx * seq_ctx, D_MODEL), "lat": x_sample.reshape(n_lat * seq_lat, D_MODEL)}
    ks_out, vs_out, ss_out = [], [], []

    for l in range(DEPTH):
        mods = {"ctx": mod_all[l, 0:1, None, :], "lat": mod_all[l, 1:1 + n_lat, None, :]}
        pad2 = (FILTER_PAD, FILTER_PAD)
        filt_w = (_pad_to(hf_w1[l], pad2), _pad_to(hf_b1[l][None, :], (1, FILTER_PAD)),
                  _pad_to(hf_freq[l][None, :], (1, FILTER_PAD)), _pad_to(hf_w2[l], pad2),
                  _pad_to(hf_b2[l][None, :], (1, FILTER_PAD)), _pad_to(hf_w3[l], (FILTER_PAD, hf_w3.shape[-1])))
        for name, g in groups.items():
            n, seq, x, mod = g["n"], g["seq"], xs[name], mods[name]
            is_ctx = name == "ctx"
            proj = _inproj(x, mod, w_in_b, l)
            att_out = _attention(proj, n, seq, q_norm[l], k_norm[l], g["rope"],
                                 None if is_ctx else (cache_k, cache_v), l, emit_cache=is_ctx)
            ret, s_fin = _retention(proj, n, seq, log_gamma[l],
                                    ret_zero if is_ctx else state_ret, 0 if is_ctx else l)
            spectra = _hyena_filters(seq, g["feat"], *filt_w, abs_deltas, g["dft"])
            hyz = _hyena(proj, n, seq, hy_conv[l], hy_bias[l], spectra, g["dft"])
            if is_ctx:
                att, own_k, own_v = att_out
                ks_out.append(own_k)
                vs_out.append(own_v)
                ss_out.append(s_fin)
            else:
                att = att_out[0]
            x = _outproj(att, ret, hyz, x, mod, w_out_b, l, ln1_w[l], ln1_b[l])
            xs[name] = _ffn(x, seq, mod, w_up_b, ffn_conv, w_down_b, l, ln2_w[l], ln2_b[l])

    return (xs["ctx"].reshape(n_ctx, seq_ctx, D_MODEL), xs["lat"].reshape(n_lat, seq_lat, D_MODEL),
            jnp.stack(ks_out, axis=1), jnp.stack(vs_out, axis=1), jnp.stack(ss_out, axis=1))
```

```python
import functools
import math

import numpy as np
import jax
import jax.numpy as jnp
from jax import lax
from jax.experimental import pallas as pl
from jax.experimental.pallas import tpu as pltpu

F32 = jnp.float32
BF16 = jnp.bfloat16

D_MODEL = 2048
DEPTH = 2
GRID_W = 64
HEAD_DIM = 128
ATT_WIDTH = D_MODEL // 2
N_ATT_HEADS = ATT_WIDTH // HEAD_DIM
N_KV_HEADS = N_ATT_HEADS // 4
GQA_GROUP = N_ATT_HEADS // N_KV_HEADS
KV_WIDTH = N_KV_HEADS * HEAD_DIM
RET_WIDTH = D_MODEL // 4
N_RET_HEADS = RET_WIDTH // HEAD_DIM
HYENA_WIDTH = D_MODEL // 4
HYENA_ORDER = 2
IN_WIDTH = ATT_WIDTH + 2 * KV_WIDTH + 4 * RET_WIDTH + (HYENA_ORDER + 1) * HYENA_WIDTH
D_FF = 5632
ROPE_THETA = 10000.0
FILTER_BANDS = 16
FILTER_EMB = 1 + 2 * FILTER_BANDS
FILTER_HID = 64
HYENA_DECAY_TARGET = 1e-2
HYENA_DECAY_PCT_MIN = 0.3
HYENA_DECAY_PCT_MAX = 1.5
DEEPNORM_ALPHA = (2 * DEPTH) ** 0.25
EPS = 1e-6

COL_Q = 0
COL_K = ATT_WIDTH
COL_V = COL_K + KV_WIDTH
COL_RQ = COL_V + KV_WIDTH
COL_RK = COL_RQ + RET_WIDTH
COL_RV = COL_RK + RET_WIDTH
COL_RG = COL_RV + RET_WIDTH
COL_HY = COL_RG + RET_WIDTH

LANES = 128
HALO = 16
VMEM_LIMIT = 60 * 1024 * 1024
COND_ROWS = 16

ADA_TN = 1024
INPROJ_TM, INPROJ_TN = 512, 512
OUTPROJ_TM, OUTPROJ_SUB = 512, 256
FFN_TM, FFN_TF = 1024, 512
ATT_TQ, ATT_CHAIN = 1024, 256
ATT_ROWS = 1024
ATT_MXU_SUM_MIN_KEYS = 1024
RET_ROWS = 2048
HYENA_ROWS = 1024
FILTER_PAD = 128

NT_DIMS = (((1,), (1,)), ((), ()))
TN_DIMS = (((0,), (0,)), ((), ()))


def _params(n_grid):
    return pltpu.CompilerParams(dimension_semantics=("arbitrary",) * n_grid,
                                vmem_limit_bytes=VMEM_LIMIT)


def _dot(a, b):
    return jnp.dot(a, b, preferred_element_type=F32)


def _split(a):
    hi = a.astype(BF16)
    lo = (a - hi.astype(F32)).astype(BF16)
    return hi, lo


def _dot3(a_hi, a_lo, b_hi, b_lo):
    return _dot(a_hi, b_hi) + _dot(a_lo, b_hi) + _dot(a_hi, b_lo)


def _silu(x):
    return x * jax.nn.sigmoid(x)


def _layer_norm(y, w, b):
    yc = y - jnp.mean(y, -1, keepdims=True)
    var = jnp.mean(yc * yc, -1, keepdims=True)
    return yc * lax.rsqrt(var + EPS) * w + b


def _shift_rows(x, down):
    n = x.shape[0]
    return pltpu.roll(x, 1 if down else n - 1, axis=0)


def _ada_kernel(cond_ref, w_ref, b_ref, o_ref):
    a = _silu(cond_ref[...]).astype(BF16)
    o_ref[...] = _dot(a, w_ref[...].astype(BF16)) + b_ref[...]


def _ada(cond, w_ada, b_ada):
    n_out = w_ada.shape[-1]
    return pl.pallas_call(
        _ada_kernel,
        grid=(DEPTH, n_out // ADA_TN),
        in_specs=[
            pl.BlockSpec((COND_ROWS, D_MODEL), lambda l, j: (0, 0)),
            pl.BlockSpec((None, D_MODEL, ADA_TN), lambda l, j: (l, 0, j)),
            pl.BlockSpec((None, 1, ADA_TN), lambda l, j: (l, 0, j)),
        ],
        out_specs=pl.BlockSpec((None, COND_ROWS, ADA_TN), lambda l, j: (l, 0, j)),
        out_shape=jax.ShapeDtypeStruct((DEPTH, COND_ROWS, n_out), F32),
        compiler_params=_params(2),
        name="ada",
    )(cond, w_ada, b_ada.reshape(DEPTH, 1, n_out))


def _inproj_kernel(x_ref, sh_ref, sc_ref, w_ref, o_ref):
    h = (x_ref[...] * (1.0 + sc_ref[...]) + sh_ref[...]).astype(BF16)
    for n0 in range(0, IN_WIDTH, INPROJ_TN):
        o_ref[:, n0:n0 + INPROJ_TN] = _dot(h, w_ref[:, n0:n0 + INPROJ_TN])


def _inproj(x, mod, w_in, layer):
    rows = x.shape[0]
    tm = INPROJ_TM
    tiles_per_mod = rows // mod.shape[0] // tm
    return pl.pallas_call(
        _inproj_kernel,
        grid=(rows // tm,),
        in_specs=[
            pl.BlockSpec((tm, D_MODEL), lambda i: (i, 0)),
            pl.BlockSpec((None, 1, D_MODEL), lambda i: (i // tiles_per_mod, 0, 0)),
            pl.BlockSpec((None, 1, D_MODEL), lambda i: (i // tiles_per_mod, 0, 1)),
            pl.BlockSpec((None, D_MODEL, IN_WIDTH), lambda i: (layer, 0, 0), pipeline_mode=pl.Buffered(1)),
        ],
        out_specs=pl.BlockSpec((tm, IN_WIDTH), lambda i: (i, 0)),
        out_shape=jax.ShapeDtypeStruct((rows, IN_WIDTH), F32),
        compiler_params=_params(1),
        name="inproj",
    )(x, mod, mod, w_in)


def _rms_norm(x, w):
    return x * lax.rsqrt(jnp.mean(x * x, -1, keepdims=True) + EPS) * w


def _rope(x, cos_full, sin_signed):
    return x * cos_full + pltpu.roll(x, HEAD_DIM // 2, axis=1) * sin_signed


def _attn_kernel(*refs, seq, n_seq, rope, has_ctx, emit_cache):
    refs = list(refs)
    q_ref, k_ref, v_ref, qw_ref, kw_ref = refs[:5]
    del refs[:5]
    if rope:
        qcos_ref, qsin_ref, kcos_ref, ksin_ref = refs[:4]
        del refs[:4]
    if has_ctx:
        ck_ref, cv_ref = refs[:2]
        del refs[:2]
    o_ref = refs.pop(0)
    if emit_cache:
        ownk_ref, ownv_ref = refs[:2]
        del refs[:2]
    k_scr, v_scr = refs
    sums_on_mxu = v_scr.shape[1] > HEAD_DIM
    own = n_seq * seq

    @pl.when(pl.program_id(2) == 0)
    def _():
        kn = _rms_norm(k_ref[...], kw_ref[...])
        if emit_cache:
            for s in range(n_seq):
                ownk_ref[s] = kn[s * seq:(s + 1) * seq]
                ownv_ref[s] = v_ref[s * seq:(s + 1) * seq, :]
        if rope:
            kn = _rope(kn, kcos_ref[...], ksin_ref[...])
        k_scr[0:own, :] = kn.astype(BF16)
        v_scr[0:own, 0:HEAD_DIM] = v_ref[...].astype(BF16)
        if has_ctx:
            k_scr[own:, :] = ck_ref[...].astype(BF16)
            v_scr[own:, 0:HEAD_DIM] = cv_ref[...].astype(BF16)
        if sums_on_mxu:
            v_scr[:, HEAD_DIM:] = jnp.ones((v_scr.shape[0], HEAD_DIM), BF16)

    tq = q_ref.shape[0] // n_seq
    for s in range(n_seq):
        keys = slice(s * seq, (s + 1) * seq) if n_seq > 1 else slice(None)
        for g in range(GQA_GROUP):
            cols = slice(g * HEAD_DIM, (g + 1) * HEAD_DIM)
            for r0 in range(s * tq, (s + 1) * tq, ATT_CHAIN):
                rows = slice(r0, r0 + ATT_CHAIN)
                qn = _rms_norm(q_ref[rows, cols], qw_ref[...])
                if rope:
                    qn = _rope(qn, qcos_ref[rows, :], qsin_ref[rows, :])
                qb = (qn * (HEAD_DIM ** -0.5)).astype(BF16)
                sc = lax.dot_general(qb, k_scr[keys, :], NT_DIMS, preferred_element_type=F32)
                p = jnp.exp(sc - jnp.max(sc, -1, keepdims=True))
                o = _dot(p.astype(BF16), v_scr[keys, :])
                den = o[:, HEAD_DIM:HEAD_DIM + 1] if sums_on_mxu else jnp.sum(p, -1, keepdims=True)
                o_ref[rows, cols] = (o[:, :HEAD_DIM] / den).astype(o_ref.dtype)


def _attention(proj, n_batch, seq, q_norm, k_norm, rope_tabs, ctx_kv, layer, emit_cache):
    rows = proj.shape[0]
    tq = min(ATT_TQ, seq)
    nq = seq // tq
    group_w = GQA_GROUP * HEAD_DIM
    rope = rope_tabs is not None
    has_ctx = ctx_kv is not None
    n_seq = max(1, ATT_ROWS // seq) if not (rope or has_ctx) else 1
    keys_per_chain = seq + (ctx_kv[0].shape[3] if has_ctx else 0)
    value_width = 2 * HEAD_DIM if keys_per_chain >= ATT_MXU_SUM_MIN_KEYS else HEAD_DIM

    in_specs = [
        pl.BlockSpec((n_seq * tq, group_w), lambda b, kv, qi: (b * nq + qi, COL_Q // group_w + kv)),
        pl.BlockSpec((n_seq * seq, HEAD_DIM), lambda b, kv, qi: (b, COL_K // HEAD_DIM + kv)),
        pl.BlockSpec((n_seq * seq, HEAD_DIM), lambda b, kv, qi: (b, COL_V // HEAD_DIM + kv)),
        pl.BlockSpec((1, HEAD_DIM), lambda b, kv, qi: (0, 0)),
        pl.BlockSpec((1, HEAD_DIM), lambda b, kv, qi: (0, 0)),
    ]
    args = [proj, proj, proj, q_norm.reshape(1, HEAD_DIM), k_norm.reshape(1, HEAD_DIM)]
    if rope:
        cos_full, sin_signed = rope_tabs
        in_specs += [
            pl.BlockSpec((tq, HEAD_DIM), lambda b, kv, qi: (qi, 0)),
            pl.BlockSpec((tq, HEAD_DIM), lambda b, kv, qi: (qi, 0)),
            pl.BlockSpec((seq, HEAD_DIM), lambda b, kv, qi: (0, 0)),
            pl.BlockSpec((seq, HEAD_DIM), lambda b, kv, qi: (0, 0)),
        ]
        args += [cos_full, sin_signed, cos_full, sin_signed]
    if has_ctx:
        ck, cv = ctx_kv
        past = ck.shape[3]
        ctx_spec = pl.BlockSpec((None, None, None, past, HEAD_DIM), lambda b, kv, qi: (b, layer, kv, 0, 0))
        in_specs += [ctx_spec, ctx_spec]
        args += [ck, cv]

    out_specs = [pl.BlockSpec((n_seq * tq, group_w), lambda b, kv, qi: (b * nq + qi, kv))]
    out_shape = [jax.ShapeDtypeStruct((rows, ATT_WIDTH), BF16)]
    if emit_cache:
        own_spec = pl.BlockSpec((n_seq, None, seq, HEAD_DIM), lambda b, kv, qi: (b, kv, 0, 0))
        out_specs += [own_spec, own_spec]
        out_shape += [jax.ShapeDtypeStruct((n_batch, N_KV_HEADS, seq, HEAD_DIM), F32)] * 2

    return pl.pallas_call(
        functools.partial(_attn_kernel, seq=seq, n_seq=n_seq, rope=rope, has_ctx=has_ctx, emit_cache=emit_cache),
        grid=(n_batch // n_seq, N_KV_HEADS, nq),
        in_specs=in_specs,
        out_specs=out_specs,
        out_shape=out_shape,
        scratch_shapes=[pltpu.VMEM((n_seq * seq + (past if has_ctx else 0), width), BF16)
                        for width in (HEAD_DIM, value_width)],
        compiler_params=_params(3),
        name="attention",
    )(*args)


def _retention_kernel(lg_ref, q_ref, k_ref, v_ref, g_ref, s0_ref, o_ref, sf_ref, decay_scr, *, seq, n_seq):
    h = pl.program_id(0)
    lg_f = lg_ref[h]
    lg_b = lg_ref[N_RET_HEADS + h]

    @pl.when(pl.program_id(1) == 0)
    def _():
        i = lax.broadcasted_iota(jnp.int32, (seq, seq), 0)
        j = lax.broadcasted_iota(jnp.int32, (seq, seq), 1)
        d = (i - j).astype(F32)
        decay = jnp.exp(jnp.where(d >= 0, lg_f, lg_b) * jnp.abs(d))
        decay_scr[...] = jnp.where(d == 0, 2.0, decay)

    pos = lax.broadcasted_iota(jnp.int32, (seq, 1), 0).astype(F32)
    q_decay_f, q_decay_b = jnp.exp(lg_f * (pos + 1.0)), jnp.exp(lg_b * (seq - pos))
    k_decay_f, k_decay_b = jnp.exp(lg_f * (seq - 1.0 - pos)), jnp.exp(lg_b * pos)
    ones = jnp.ones((1, HEAD_DIM), F32)
    s_decay_f, s_decay_b = jnp.exp(ones * (lg_f * seq)), jnp.exp(ones * (lg_b * seq))

    for s in range(n_seq):
        rows = slice(s * seq, (s + 1) * seq)
        q = q_ref[rows, :]
        k = k_ref[rows, :] * (HEAD_DIM ** -0.5)
        vb = v_ref[rows, :].astype(BF16)
        a = lax.dot_general(q.astype(BF16), k.astype(BF16), NT_DIMS, preferred_element_type=F32)
        o = _dot((a * decay_scr[...]).astype(BF16), vb)
        s0_f = s0_ref[s, 0]
        s0_b = s0_ref[s, 1]
        o = o + _dot((q * q_decay_f).astype(BF16), s0_f.astype(BF16))
        o = o + _dot((q * q_decay_b).astype(BF16), s0_b.astype(BF16))

        kf = (k * k_decay_f).astype(BF16)
        kb = (k * k_decay_b).astype(BF16)
        sf_ref[s, 0] = s_decay_f * s0_f + lax.dot_general(kf, vb, TN_DIMS, preferred_element_type=F32)
        sf_ref[s, 1] = s_decay_b * s0_b + lax.dot_general(kb, vb, TN_DIMS, preferred_element_type=F32)

        oc = o - jnp.mean(o, -1, keepdims=True)
        var = jnp.mean(oc * oc, -1, keepdims=True)
        o_ref[rows, :] = (_silu(g_ref[rows, :]) * (oc * lax.rsqrt(var + EPS))).astype(o_ref.dtype)


def _retention(proj, n_batch, seq, log_gamma, s0, layer):
    rows = proj.shape[0]
    n_seq = max(1, RET_ROWS // seq)
    hd = HEAD_DIM

    def col(offset):
        return pl.BlockSpec((n_seq * seq, hd), lambda h, b: (b, offset // hd + h))

    return pl.pallas_call(
        functools.partial(_retention_kernel, seq=seq, n_seq=n_seq),
        grid=(N_RET_HEADS, n_batch // n_seq),
        in_specs=[
            pl.BlockSpec(memory_space=pltpu.SMEM),
            col(COL_RQ), col(COL_RK), col(COL_RV), col(COL_RG),
            pl.BlockSpec((n_seq, None, 2, None, hd, hd), lambda h, b: (b, layer, 0, h, 0, 0)),
        ],
        out_specs=[
            pl.BlockSpec((n_seq * seq, hd), lambda h, b: (b, h)),
            pl.BlockSpec((n_seq, 2, None, hd, hd), lambda h, b: (b, 0, h, 0, 0)),
        ],
        out_shape=[
            jax.ShapeDtypeStruct((rows, RET_WIDTH), BF16),
            jax.ShapeDtypeStruct((n_batch, 2, N_RET_HEADS, hd, hd), F32),
        ],
        scratch_shapes=[pltpu.VMEM((seq, seq), F32)],
        compiler_params=_params(2),
        name="retention",
    )(log_gamma.reshape(2 * N_RET_HEADS), proj, proj, proj, proj, s0)


def _alt_sign(n):
    r = lax.broadcasted_iota(jnp.int32, (n, 1), 0)
    return jnp.where((r & 1) == 0, 1.0, -1.0).astype(F32)


def _filter_kernel(z_ref, w1_ref, b1_ref, fr_ref, w2_ref, b2_ref, w3f_ref, w3b_ref, dl_ref,
                   ch_ref, cl_ref, sh_ref, sl_ref, kr_ref, ki_ref, kn_ref, *, seq):
    freq = fr_ref[...]
    hdn = jnp.sin(freq * (_dot3(*_split(z_ref[...]), *_split(w1_ref[...])) + b1_ref[...]))
    hdn = jnp.sin(freq * (_dot3(*_split(hdn), *_split(w2_ref[...])) + b2_ref[...]))
    hdn_hi, hdn_lo = _split(hdn)

    pos = lax.broadcasted_iota(jnp.int32, (seq, 1), 0)
    t = pos.astype(F32) / max(seq - 1, 1)
    decay = jnp.exp(-t * dl_ref[...])
    h_fwd = _dot3(hdn_hi, hdn_lo, *_split(w3f_ref[...])) * decay
    h_bwd = _dot3(hdn_hi, hdn_lo, *_split(w3b_ref[...])) * decay

    h_bwd_shift = jnp.where(pos == 0, 0.0, _shift_rows(h_bwd, down=True))
    alt = _alt_sign(seq)
    kr = _dot3(ch_ref[...], cl_ref[...], *_split(h_fwd + h_bwd_shift)) + alt * h_bwd[seq - 1:seq, :]
    ki = _dot3(sh_ref[...], sl_ref[...], *_split(h_bwd_shift - h_fwd))
    wgt = jnp.where(pos == 0, 1.0, 2.0) * (0.5 / seq)
    kr_ref[...] = kr * wgt
    ki_ref[...] = ki * wgt
    kn_ref[...] = jnp.sum(alt * (h_fwd - h_bwd), axis=0, keepdims=True) * (0.5 / seq)


def _hyena_filters(seq, zfeat, w1, b1, freq, w2, b2, w3, abs_deltas, dft):
    w = HYENA_WIDTH
    const = lambda o: (0, 0)
    once = pl.Buffered(1)
    mat = pl.BlockSpec((seq, seq), const, pipeline_mode=once)
    vec = pl.BlockSpec((1, FILTER_PAD), const)
    return pl.pallas_call(
        functools.partial(_filter_kernel, seq=seq),
        grid=(HYENA_ORDER,),
        in_specs=[
            pl.BlockSpec((seq, FILTER_PAD), const),
            pl.BlockSpec((FILTER_PAD, FILTER_PAD), const), vec, vec,
            pl.BlockSpec((FILTER_PAD, FILTER_PAD), const), vec,
            pl.BlockSpec((FILTER_PAD, w), lambda o: (0, 2 * o)),
            pl.BlockSpec((FILTER_PAD, w), lambda o: (0, 2 * o + 1)),
            pl.BlockSpec((1, w), const),
            mat, mat, mat, mat,
        ],
        out_specs=[
            pl.BlockSpec((None, seq, w), lambda o: (o, 0, 0)),
            pl.BlockSpec((None, seq, w), lambda o: (o, 0, 0)),
            pl.BlockSpec((None, 1, w), lambda o: (o, 0, 0)),
        ],
        out_shape=[
            jax.ShapeDtypeStruct((HYENA_ORDER, seq, HYENA_WIDTH), F32),
            jax.ShapeDtypeStruct((HYENA_ORDER, seq, HYENA_WIDTH), F32),
            jax.ShapeDtypeStruct((HYENA_ORDER, 1, HYENA_WIDTH), F32),
        ],
        compiler_params=_params(1),
        name="hyena_filters",
    )(zfeat, w1, b1, freq, w2, b2, w3, w3, abs_deltas, *dft)


def _dwconv3_seq(x, w, pos, seq):
    prev = jnp.where(pos == 0, 0.0, _shift_rows(x, down=True))
    nxt = jnp.where(pos == seq - 1, 0.0, _shift_rows(x, down=False))
    return prev * w[0:1] + x * w[1:2] + nxt * w[2:3]


def _hyena_kernel(v_ref, x1_ref, x2_ref, cv_ref, c1_ref, c2_ref, bias_ref, kr_ref, ki_ref, kn_ref,
                  cos_ref, sin_ref, o_ref, *, seq, n_seq):
    pos = lax.broadcasted_iota(jnp.int32, (seq, 1), 0)
    alt = _alt_sign(seq)

    def dft(m_ref, a):
        return _dot(m_ref[...], a.astype(BF16))

    for s in range(n_seq):
        rows = slice(s * seq, (s + 1) * seq)
        z = _dwconv3_seq(v_ref[rows, :], cv_ref[...], pos, seq)
        for o, (x_ref, c_ref) in enumerate(((x1_ref, c1_ref), (x2_ref, c2_ref))):
            ur = dft(cos_ref, z)
            ui = dft(sin_ref, z)
            u_nyq = jnp.sum(alt * z, axis=0, keepdims=True)
            kr, ki = kr_ref[o], ki_ref[o]
            yr = ur * kr + ui * ki
            yi = ui * kr - ur * ki
            y = dft(cos_ref, yr) + dft(sin_ref, yi) + alt * (u_nyq * kn_ref[o])
            z = _dwconv3_seq(x_ref[rows, :], c_ref[...], pos, seq) * (y + z * bias_ref[o:o + 1, :])
        o_ref[rows, :] = z.astype(o_ref.dtype)


def _hyena(proj, n_batch, seq, hy_conv, hy_bias, spectra, dft):
    rows = proj.shape[0]
    w = HYENA_WIDTH
    kr, ki, kn = spectra
    once = pl.Buffered(1)
    n_seq = max(1, HYENA_ROWS // seq)

    def part(p):
        return pl.BlockSpec((n_seq * seq, w), lambda b: (b, COL_HY // w + p))

    def conv(p):
        return pl.BlockSpec((3, w), lambda b: (0, p), pipeline_mode=once)

    mat = pl.BlockSpec((seq, seq), lambda b: (0, 0), pipeline_mode=once)
    spec = pl.BlockSpec((HYENA_ORDER, seq, w), lambda b: (0, 0, 0), pipeline_mode=once)
    return pl.pallas_call(
        functools.partial(_hyena_kernel, seq=seq, n_seq=n_seq),
        grid=(n_batch // n_seq,),
        in_specs=[
            part(0), part(1), part(2), conv(0), conv(1), conv(2),
            pl.BlockSpec((HYENA_ORDER, w), lambda b: (0, 0), pipeline_mode=once),
            spec, spec,
            pl.BlockSpec((HYENA_ORDER, 1, w), lambda b: (0, 0, 0), pipeline_mode=once),
            mat, mat,
        ],
        out_specs=pl.BlockSpec((n_seq * seq, w), lambda b: (b, 0)),
        out_shape=jax.ShapeDtypeStruct((rows, w), BF16),
        compiler_params=_params(1),
        name="hyena",
    )(proj, proj, proj, hy_conv, hy_conv, hy_conv, hy_bias, kr, ki, kn, dft[0], dft[2])


def _outproj_kernel(att_ref, ret_ref, hy_ref, x_ref, g_ref, w_ref, lw_ref, lb_ref, o_ref):
    r0, r1 = ATT_WIDTH, ATT_WIDTH + RET_WIDTH
    for t0 in range(0, x_ref.shape[0], OUTPROJ_SUB):
        rows = slice(t0, t0 + OUTPROJ_SUB)
        mix = (_dot(att_ref[rows, :], w_ref[0:r0, :]) + _dot(ret_ref[rows, :], w_ref[r0:r1, :])
               + _dot(hy_ref[rows, :], w_ref[r1:, :]))
        y = DEEPNORM_ALPHA * x_ref[rows, :] + g_ref[...] * mix
        o_ref[rows, :] = _layer_norm(y, lw_ref[...], lb_ref[...])


def _outproj(att, ret, hyz, x, mod, w_out, layer, ln_w, ln_b):
    rows = x.shape[0]
    tm = OUTPROJ_TM
    tiles_per_mod = rows // mod.shape[0] // tm
    row = lambda width: pl.BlockSpec((tm, width), lambda i: (i, 0))
    vec = pl.BlockSpec((1, D_MODEL), lambda i: (0, 0))
    return pl.pallas_call(
        _outproj_kernel,
        grid=(rows // tm,),
        in_specs=[
            row(ATT_WIDTH), row(RET_WIDTH), row(HYENA_WIDTH), row(D_MODEL),
            pl.BlockSpec((None, 1, D_MODEL), lambda i: (i // tiles_per_mod, 0, 2)),
            pl.BlockSpec((None, D_MODEL, D_MODEL), lambda i: (layer, 0, 0), pipeline_mode=pl.Buffered(1)),
            vec, vec,
        ],
        out_specs=row(D_MODEL),
        out_shape=jax.ShapeDtypeStruct((rows, D_MODEL), F32),
        compiler_params=_params(1),
        name="outproj_ln",
    )(att, ret, hyz, x, mod, w_out, ln_w.reshape(1, D_MODEL), ln_b.reshape(1, D_MODEL))


def _ffn_kernel(x_ref, xp_ref, xn_ref, sh_ref, sc_ref, g_ref, wg_ref, wu_ref, cg_ref, cu_ref, wd_ref,
                lw_ref, lb_ref, o_ref, h_scr, *, seq, tm):
    i = pl.program_id(0)
    j = pl.program_id(1)
    ends_inside = seq < tm

    @pl.when(j == 0)
    def _():
        sc = 1.0 + sc_ref[...]
        sh = sh_ref[...]
        h_prev = xp_ref[...] * sc + sh
        h_next = xn_ref[...] * sc + sh
        if not ends_inside:
            h_prev = jnp.where(((i * tm) & (seq - 1)) == 0, 0.0, h_prev)
            h_next = jnp.where((((i + 1) * tm) & (seq - 1)) == 0, 0.0, h_next)
        h_scr[0:HALO, :] = h_prev.astype(BF16)
        h_scr[HALO:HALO + tm, :] = (x_ref[...] * sc + sh).astype(BF16)
        h_scr[HALO + tm:, :] = h_next.astype(BF16)
        o_ref[...] = jnp.zeros_like(o_ref)

    pos = (i * tm + lax.broadcasted_iota(jnp.int32, (tm, 1), 0)) & (seq - 1)

    def up_conv(w_ref, c_ref):
        up = _dot(h_scr[...], w_ref[...])
        prev = _shift_rows(up, down=True)[HALO:HALO + tm]
        nxt = _shift_rows(up, down=False)[HALO:HALO + tm]
        if ends_inside:
            prev = jnp.where(pos == 0, 0.0, prev)
            nxt = jnp.where(pos == seq - 1, 0.0, nxt)
        c = c_ref[...]
        return prev * c[0:1] + up[HALO:HALO + tm] * c[1:2] + nxt * c[2:3]

    act = _silu(up_conv(wg_ref, cg_ref)) * up_conv(wu_ref, cu_ref)
    o_ref[...] += _dot(act.astype(BF16), wd_ref[...])

    @pl.when(j == pl.num_programs(1) - 1)
    def _():
        y = DEEPNORM_ALPHA * x_ref[...] + g_ref[...] * o_ref[...]
        o_ref[...] = _layer_norm(y, lw_ref[...], lb_ref[...])


def _ffn(x, seq, mod, w_up, ffn_conv, w_down, layer, ln_w, ln_b):
    rows = x.shape[0]
    tm, tf = FFN_TM, FFN_TF
    n_f = D_FF // tf
    tiles_per_mod = rows // mod.shape[0] // tm
    halo_per_tile = tm // HALO
    last_halo = rows // HALO - 1

    def mod_spec(chunk):
        return pl.BlockSpec((None, 1, D_MODEL), lambda i, j: (i // tiles_per_mod, 0, chunk))

    vec = pl.BlockSpec((1, D_MODEL), lambda i, j: (0, 0))
    return pl.pallas_call(
        functools.partial(_ffn_kernel, seq=seq, tm=tm),
        grid=(rows // tm, n_f),
        in_specs=[
            pl.BlockSpec((tm, D_MODEL), lambda i, j: (i, 0), pipeline_mode=pl.Buffered(1)),
            pl.BlockSpec((HALO, D_MODEL), lambda i, j: (jnp.maximum(i * halo_per_tile - 1, 0), 0)),
            pl.BlockSpec((HALO, D_MODEL), lambda i, j: (jnp.minimum((i + 1) * halo_per_tile, last_halo), 0)),
            mod_spec(3), mod_spec(4), mod_spec(5),
            pl.BlockSpec((None, D_MODEL, tf), lambda i, j: (layer, 0, j)),
            pl.BlockSpec((None, D_MODEL, tf), lambda i, j: (layer, 0, n_f + j)),
            pl.BlockSpec((None, 3, tf), lambda i, j: (layer, 0, j)),
            pl.BlockSpec((None, 3, tf), lambda i, j: (layer, 0, n_f + j)),
            pl.BlockSpec((None, tf, D_MODEL), lambda i, j: (layer, j, 0)),
            vec, vec,
        ],
        out_specs=pl.BlockSpec((tm, D_MODEL), lambda i, j: (i, 0)),
        out_shape=jax.ShapeDtypeStruct((rows, D_MODEL), F32),
        scratch_shapes=[pltpu.VMEM((tm + 2 * HALO, D_MODEL), BF16)],
        compiler_params=_params(2),
        name="ffn_ln",
    )(x, x, x, mod, mod, mod, w_up, w_up, ffn_conv, ffn_conv, w_down,
      ln_w.reshape(1, D_MODEL), ln_b.reshape(1, D_MODEL))


def _rope_tables(n_tokens):
    t = np.arange(n_tokens)
    n_freq = HEAD_DIM // 4
    inv_freq = ROPE_THETA ** (-np.arange(n_freq) / n_freq)
    ang = np.concatenate([(t // GRID_W)[:, None] * inv_freq[None], (t % GRID_W)[:, None] * inv_freq[None]], -1)
    cos, sin = np.cos(ang), np.sin(ang)
    return (jnp.asarray(np.concatenate([cos, cos], -1), F32), jnp.asarray(np.concatenate([-sin, sin], -1), F32))


def _dft_tables(seq):
    idx = np.arange(seq)
    ang = ((idx[:, None] * idx[None, :]) % (2 * seq)) * (math.pi / seq)
    return (*_split(jnp.asarray(np.cos(ang), F32)), *_split(jnp.asarray(np.sin(ang), F32)))


def _filter_features(seq):
    pos = np.arange(seq)
    t = pos / max(seq - 1, 1)
    ang = (2.0 * math.pi * pos / seq)[:, None] * np.linspace(1e-4, FILTER_BANDS - 1, FILTER_BANDS)[None, :]
    z = np.concatenate([t[:, None], np.cos(ang), -np.sin(ang)], -1)
    return jnp.asarray(np.pad(z, ((0, 0), (0, FILTER_PAD - FILTER_EMB))), F32)


def _pad_to(a, shape):
    return jnp.pad(a, [(0, s - d) for d, s in zip(a.shape, shape)])


def kernel(x_prompt, x_sample, cache_k, cache_v, state_ret, c, c_ctx, w_ada, b_ada, w_in, q_norm, k_norm, ret_decay, hy_conv, hf_w1, hf_b1, hf_freq, hf_w2, hf_b2, hf_w3, hy_bias, w_out, ln1_w, ln1_b, w_up, ffn_conv, w_down, ln2_w, ln2_b):
    n_ctx, seq_ctx, _ = x_prompt.shape
    n_lat, seq_lat, _ = x_sample.shape

    cond = _pad_to(jnp.concatenate([c_ctx[None, :], c], 0), (COND_ROWS, D_MODEL))
    mod_all = _ada(cond, w_ada, b_ada)

    w_in_b, w_out_b = w_in.astype(BF16), w_out.astype(BF16)
    w_up_b, w_down_b = w_up.astype(BF16), w_down.astype(BF16)
    log_gamma = jax.nn.log_sigmoid(ret_decay.astype(F32))
    abs_deltas = jnp.asarray(np.abs(np.linspace(math.log(HYENA_DECAY_TARGET) / HYENA_DECAY_PCT_MIN,
                                                math.log(HYENA_DECAY_TARGET) / HYENA_DECAY_PCT_MAX,
                                                HYENA_WIDTH))[None, :], F32)
    ret_zero = jnp.zeros((n_ctx, 1, 2, N_RET_HEADS, HEAD_DIM, HEAD_DIM), F32)

    groups = {
        "ctx": dict(n=n_ctx, seq=seq_ctx, rope=None, dft=_dft_tables(seq_ctx), feat=_filter_features(seq_ctx)),
        "lat": dict(n=n_lat, seq=seq_lat, rope=_rope_tables(seq_lat), dft=_dft_tables(seq_lat),
                    feat=_filter_features(seq_lat)),
    }
    xs = {"ctx": x_prompt.reshape(n_ctx * seq_ctx, D_MODEL), "lat": x_sample.reshape(n_lat * seq_lat, D_MODEL)}
    ks_out, vs_out, ss_out = [], [], []

    for l in range(DEPTH):
        mods = {"ctx": mod_all[l, 0:1, None, :], "lat": mod_all[l, 1:1 + n_lat, None, :]}
        pad2 = (FILTER_PAD, FILTER_PAD)
        filt_w = (_pad_to(hf_w1[l], pad2), _pad_to(hf_b1[l][None, :], (1, FILTER_PAD)),
                  _pad_to(hf_freq[l][None, :], (1, FILTER_PAD)), _pad_to(hf_w2[l], pad2),
                  _pad_to(hf_b2[l][None, :], (1, FILTER_PAD)), _pad_to(hf_w3[l], (FILTER_PAD, hf_w3.shape[-1])))
        for name, g in groups.items():
            n, seq, x, mod = g["n"], g["seq"], xs[name], mods[name]
            is_ctx = name == "ctx"
            proj = _inproj(x, mod, w_in_b, l)
            att_out = _attention(proj, n, seq, q_norm[l], k_norm[l], g["rope"],
                                 None if is_ctx else (cache_k, cache_v), l, emit_cache=is_ctx)
            ret, s_fin = _retention(proj, n, seq, log_gamma[l],
                                    ret_zero if is_ctx else state_ret, 0 if is_ctx else l)
            spectra = _hyena_filters(seq, g["feat"], *filt_w, abs_deltas, g["dft"])
            hyz = _hyena(proj, n, seq, hy_conv[l], hy_bias[l], spectra, g["dft"])
            if is_ctx:
                att, own_k, own_v = att_out
                ks_out.append(own_k)
                vs_out.append(own_v)
                ss_out.append(s_fin)
            else:
                att = att_out[0]
            x = _outproj(att, ret, hyz, x, mod, w_out_b, l, ln1_w[l], ln1_b[l])
            xs[name] = _ffn(x, seq, mod, w_up_b, ffn_conv, w_down_b, l, ln2_w[l], ln2_b[l])

    return (xs["ctx"].reshape(n_ctx, seq_ctx, D_MODEL), xs["lat"].reshape(n_lat, seq_lat, D_MODEL),
            jnp.stack(ks_out, axis=1), jnp.stack(vs_out, axis=1), jnp.stack(ss_out, axis=1))
```

```python
import functools
import math

import numpy as np
import jax
import jax.numpy as jnp
from jax import lax
from jax.experimental import pallas as pl
from jax.experimental.pallas import tpu as pltpu

F32 = jnp.float32
BF16 = jnp.bfloat16

D_MODEL = 2048
DEPTH = 2
GRID_W = 64
HEAD_DIM = 128
ATT_WIDTH = D_MODEL // 2
N_ATT_HEADS = ATT_WIDTH // HEAD_DIM
N_KV_HEADS = N_ATT_HEADS // 4
GQA_GROUP = N_ATT_HEADS // N_KV_HEADS
KV_WIDTH = N_KV_HEADS * HEAD_DIM
RET_WIDTH = D_MODEL // 4
N_RET_HEADS = RET_WIDTH // HEAD_DIM
HYENA_WIDTH = D_MODEL // 4
HYENA_ORDER = 2
IN_WIDTH = ATT_WIDTH + 2 * KV_WIDTH + 4 * RET_WIDTH + (HYENA_ORDER + 1) * HYENA_WIDTH
D_FF = 5632
ROPE_THETA = 10000.0
FILTER_BANDS = 16
FILTER_EMB = 1 + 2 * FILTER_BANDS
FILTER_HID = 64
HYENA_DECAY_TARGET = 1e-2
HYENA_DECAY_PCT_MIN = 0.3
HYENA_DECAY_PCT_MAX = 1.5
DEEPNORM_ALPHA = (2 * DEPTH) ** 0.25
EPS = 1e-6

COL_Q = 0
COL_K = ATT_WIDTH
COL_V = COL_K + KV_WIDTH
COL_RQ = COL_V + KV_WIDTH
COL_RK = COL_RQ + RET_WIDTH
COL_RV = COL_RK + RET_WIDTH
COL_RG = COL_RV + RET_WIDTH
COL_HY = COL_RG + RET_WIDTH

LANES = 128
HALO = 16
VMEM_LIMIT = 60 * 1024 * 1024
COND_ROWS = 16

ADA_TN = 1024
INPROJ_TM, INPROJ_TN = 256, 512
OUTPROJ_TM, OUTPROJ_SUB = 512, 256
FFN_TM, FFN_TF = 1024, 512
ATT_TQ, ATT_CHAIN = 1024, 256
ATT_ROWS = 1024
ATT_MXU_SUM_MIN_KEYS = 1024
RET_ROWS = 2048
HYENA_ROWS = 1024
FILTER_PAD = 128

NT_DIMS = (((1,), (1,)), ((), ()))
TN_DIMS = (((0,), (0,)), ((), ()))


def _params(n_grid):
    return pltpu.CompilerParams(dimension_semantics=("arbitrary",) * n_grid,
                                vmem_limit_bytes=VMEM_LIMIT)


def _dot(a, b):
    return jnp.dot(a, b, preferred_element_type=F32)


def _split(a):
    hi = a.astype(BF16)
    lo = (a - hi.astype(F32)).astype(BF16)
    return hi, lo


def _dot3(a_hi, a_lo, b_hi, b_lo):
    return _dot(a_hi, b_hi) + _dot(a_lo, b_hi) + _dot(a_hi, b_lo)


def _silu(x):
    return x * jax.nn.sigmoid(x)


def _layer_norm(y, w, b):
    yc = y - jnp.mean(y, -1, keepdims=True)
    var = jnp.mean(yc * yc, -1, keepdims=True)
    return yc * lax.rsqrt(var + EPS) * w + b


def _shift_rows(x, down):
    n = x.shape[0]
    return pltpu.roll(x, 1 if down else n - 1, axis=0)


def _ada_kernel(cond_ref, w_ref, b_ref, o_ref):
    a = _silu(cond_ref[...]).astype(BF16)
    o_ref[...] = _dot(a, w_ref[...].astype(BF16)) + b_ref[...]


def _ada(cond, w_ada, b_ada):
    n_out = w_ada.shape[-1]
    return pl.pallas_call(
        _ada_kernel,
        grid=(DEPTH, n_out // ADA_TN),
        in_specs=[
            pl.BlockSpec((COND_ROWS, D_MODEL), lambda l, j: (0, 0)),
            pl.BlockSpec((None, D_MODEL, ADA_TN), lambda l, j: (l, 0, j)),
            pl.BlockSpec((None, 1, ADA_TN), lambda l, j: (l, 0, j)),
        ],
        out_specs=pl.BlockSpec((None, COND_ROWS, ADA_TN), lambda l, j: (l, 0, j)),
        out_shape=jax.ShapeDtypeStruct((DEPTH, COND_ROWS, n_out), F32),
        compiler_params=_params(2),
        name="ada",
    )(cond, w_ada, b_ada.reshape(DEPTH, 1, n_out))


def _inproj_kernel(x_ref, sh_ref, sc_ref, w_ref, cast_src_ref, o_ref, cast_ref):
    h = (x_ref[...] * (1.0 + sc_ref[...]) + sh_ref[...]).astype(BF16)
    for n0 in range(0, IN_WIDTH, INPROJ_TN):
        o_ref[:, n0:n0 + INPROJ_TN] = _dot(h, w_ref[:, n0:n0 + INPROJ_TN])
    cast_ref[...] = cast_src_ref[...].astype(BF16)


def _inproj(x, mod, w_in, layer, cast_src):
    rows = x.shape[0]
    tm = INPROJ_TM
    n_steps = rows // tm
    tiles_per_mod = n_steps // mod.shape[0]
    _, k, n = cast_src.shape
    band = k // n_steps
    assert band * n_steps == k and band % HALO == 0, (k, n_steps)
    return pl.pallas_call(
        _inproj_kernel,
        grid=(n_steps,),
        in_specs=[
            pl.BlockSpec((tm, D_MODEL), lambda i: (i, 0)),
            pl.BlockSpec((None, 1, D_MODEL), lambda i: (i // tiles_per_mod, 0, 0)),
            pl.BlockSpec((None, 1, D_MODEL), lambda i: (i // tiles_per_mod, 0, 1)),
            pl.BlockSpec((None, D_MODEL, IN_WIDTH), lambda i: (layer, 0, 0), pipeline_mode=pl.Buffered(1)),
            pl.BlockSpec((None, band, n), lambda i: (layer, i, 0)),
        ],
        out_specs=[pl.BlockSpec((tm, IN_WIDTH), lambda i: (i, 0)), pl.BlockSpec((band, n), lambda i: (i, 0))],
        out_shape=[jax.ShapeDtypeStruct((rows, IN_WIDTH), F32), jax.ShapeDtypeStruct((k, n), BF16)],
        compiler_params=_params(1),
        name="inproj",
    )(x, mod, mod, w_in, cast_src)


def _rms_norm(x, w):
    return x * lax.rsqrt(jnp.mean(x * x, -1, keepdims=True) + EPS) * w


def _rope(x, cos_full, sin_signed):
    return x * cos_full + pltpu.roll(x, HEAD_DIM // 2, axis=1) * sin_signed


def _attn_kernel(*refs, seq, n_seq, rope, has_ctx, emit_cache):
    refs = list(refs)
    q_ref, k_ref, v_ref, qw_ref, kw_ref = refs[:5]
    del refs[:5]
    if rope:
        qcos_ref, qsin_ref, kcos_ref, ksin_ref = refs[:4]
        del refs[:4]
    if has_ctx:
        ck_ref, cv_ref = refs[:2]
        del refs[:2]
    o_ref = refs.pop(0)
    if emit_cache:
        ownk_ref, ownv_ref = refs[:2]
        del refs[:2]
    k_scr, v_scr = refs
    sums_on_mxu = v_scr.shape[1] > HEAD_DIM
    own = n_seq * seq

    @pl.when(pl.program_id(2) == 0)
    def _():
        kn = _rms_norm(k_ref[...], kw_ref[...])
        if emit_cache:
            for s in range(n_seq):
                ownk_ref[s] = kn[s * seq:(s + 1) * seq]
                ownv_ref[s] = v_ref[s * seq:(s + 1) * seq, :]
        if rope:
            kn = _rope(kn, kcos_ref[...], ksin_ref[...])
        k_scr[0:own, :] = kn.astype(BF16)
        v_scr[0:own, 0:HEAD_DIM] = v_ref[...].astype(BF16)
        if has_ctx:
            k_scr[own:, :] = ck_ref[...].astype(BF16)
            v_scr[own:, 0:HEAD_DIM] = cv_ref[...].astype(BF16)
        if sums_on_mxu:
            v_scr[:, HEAD_DIM:] = jnp.ones((v_scr.shape[0], HEAD_DIM), BF16)

    tq = q_ref.shape[0] // n_seq
    for s in range(n_seq):
        keys = slice(s * seq, (s + 1) * seq) if n_seq > 1 else slice(None)
        for g in range(GQA_GROUP):
            cols = slice(g * HEAD_DIM, (g + 1) * HEAD_DIM)
            for r0 in range(s * tq, (s + 1) * tq, ATT_CHAIN):
                rows = slice(r0, r0 + ATT_CHAIN)
                qn = _rms_norm(q_ref[rows, cols], qw_ref[...])
                if rope:
                    qn = _rope(qn, qcos_ref[rows, :], qsin_ref[rows, :])
                qb = (qn * (HEAD_DIM ** -0.5)).astype(BF16)
                sc = lax.dot_general(qb, k_scr[keys, :], NT_DIMS, preferred_element_type=F32)
                p = jnp.exp(sc - jnp.max(sc, -1, keepdims=True))
                o = _dot(p.astype(BF16), v_scr[keys, :])
                den = o[:, HEAD_DIM:HEAD_DIM + 1] if sums_on_mxu else jnp.sum(p, -1, keepdims=True)
                o_ref[rows, cols] = (o[:, :HEAD_DIM] / den).astype(o_ref.dtype)


def _attention(proj, n_batch, seq, q_norm, k_norm, rope_tabs, ctx_kv, layer, emit_cache):
    rows = proj.shape[0]
    tq = min(ATT_TQ, seq)
    nq = seq // tq
    group_w = GQA_GROUP * HEAD_DIM
    rope = rope_tabs is not None
    has_ctx = ctx_kv is not None
    n_seq = max(1, ATT_ROWS // seq) if not (rope or has_ctx) else 1
    keys_per_chain = seq + (ctx_kv[0].shape[3] if has_ctx else 0)
    value_width = 2 * HEAD_DIM if keys_per_chain >= ATT_MXU_SUM_MIN_KEYS else HEAD_DIM

    in_specs = [
        pl.BlockSpec((n_seq * tq, group_w), lambda b, kv, qi: (b * nq + qi, COL_Q // group_w + kv)),
        pl.BlockSpec((n_seq * seq, HEAD_DIM), lambda b, kv, qi: (b, COL_K // HEAD_DIM + kv)),
        pl.BlockSpec((n_seq * seq, HEAD_DIM), lambda b, kv, qi: (b, COL_V // HEAD_DIM + kv)),
        pl.BlockSpec((1, HEAD_DIM), lambda b, kv, qi: (0, 0)),
        pl.BlockSpec((1, HEAD_DIM), lambda b, kv, qi: (0, 0)),
    ]
    args = [proj, proj, proj, q_norm.reshape(1, HEAD_DIM), k_norm.reshape(1, HEAD_DIM)]
    if rope:
        cos_full, sin_signed = rope_tabs
        in_specs += [
            pl.BlockSpec((tq, HEAD_DIM), lambda b, kv, qi: (qi, 0)),
            pl.BlockSpec((tq, HEAD_DIM), lambda b, kv, qi: (qi, 0)),
            pl.BlockSpec((seq, HEAD_DIM), lambda b, kv, qi: (0, 0)),
            pl.BlockSpec((seq, HEAD_DIM), lambda b, kv, qi: (0, 0)),
        ]
        args += [cos_full, sin_signed, cos_full, sin_signed]
    if has_ctx:
        ck, cv = ctx_kv
        past = ck.shape[3]
        ctx_spec = pl.BlockSpec((None, None, None, past, HEAD_DIM), lambda b, kv, qi: (b, layer, kv, 0, 0))
        in_specs += [ctx_spec, ctx_spec]
        args += [ck, cv]

    out_specs = [pl.BlockSpec((n_seq * tq, group_w), lambda b, kv, qi: (b * nq + qi, kv))]
    out_shape = [jax.ShapeDtypeStruct((rows, ATT_WIDTH), BF16)]
    if emit_cache:
        own_spec = pl.BlockSpec((n_seq, None, seq, HEAD_DIM), lambda b, kv, qi: (b, kv, 0, 0))
        out_specs += [own_spec, own_spec]
        out_shape += [jax.ShapeDtypeStruct((n_batch, N_KV_HEADS, seq, HEAD_DIM), F32)] * 2

    return pl.pallas_call(
        functools.partial(_attn_kernel, seq=seq, n_seq=n_seq, rope=rope, has_ctx=has_ctx, emit_cache=emit_cache),
        grid=(n_batch // n_seq, N_KV_HEADS, nq),
        in_specs=in_specs,
        out_specs=out_specs,
        out_shape=out_shape,
        scratch_shapes=[pltpu.VMEM((n_seq * seq + (past if has_ctx else 0), width), BF16)
                        for width in (HEAD_DIM, value_width)],
        compiler_params=_params(3),
        name="attention",
    )(*args)


def _retention_kernel(lg_ref, q_ref, k_ref, v_ref, g_ref, s0_ref, o_ref, sf_ref, decay_scr, *, seq, n_seq):
    h = pl.program_id(0)
    lg_f = lg_ref[h]
    lg_b = lg_ref[N_RET_HEADS + h]

    @pl.when(pl.program_id(1) == 0)
    def _():
        i = lax.broadcasted_iota(jnp.int32, (seq, seq), 0)
        j = lax.broadcasted_iota(jnp.int32, (seq, seq), 1)
        d = (i - j).astype(F32)
        decay = jnp.exp(jnp.where(d >= 0, lg_f, lg_b) * jnp.abs(d))
        decay_scr[...] = jnp.where(d == 0, 2.0, decay)

    pos = lax.broadcasted_iota(jnp.int32, (seq, 1), 0).astype(F32)
    q_decay_f, q_decay_b = jnp.exp(lg_f * (pos + 1.0)), jnp.exp(lg_b * (seq - pos))
    k_decay_f, k_decay_b = jnp.exp(lg_f * (seq - 1.0 - pos)), jnp.exp(lg_b * pos)
    ones = jnp.ones((1, HEAD_DIM), F32)
    s_decay_f, s_decay_b = jnp.exp(ones * (lg_f * seq)), jnp.exp(ones * (lg_b * seq))

    for s in range(n_seq):
        rows = slice(s * seq, (s + 1) * seq)
        q = q_ref[rows, :]
        k = k_ref[rows, :] * (HEAD_DIM ** -0.5)
        vb = v_ref[rows, :].astype(BF16)
        a = lax.dot_general(q.astype(BF16), k.astype(BF16), NT_DIMS, preferred_element_type=F32)
        o = _dot((a * decay_scr[...]).astype(BF16), vb)
        s0_f = s0_ref[s, 0]
        s0_b = s0_ref[s, 1]
        o = o + _dot((q * q_decay_f).astype(BF16), s0_f.astype(BF16))
        o = o + _dot((q * q_decay_b).astype(BF16), s0_b.astype(BF16))

        kf = (k * k_decay_f).astype(BF16)
        kb = (k * k_decay_b).astype(BF16)
        sf_ref[s, 0] = s_decay_f * s0_f + lax.dot_general(kf, vb, TN_DIMS, preferred_element_type=F32)
        sf_ref[s, 1] = s_decay_b * s0_b + lax.dot_general(kb, vb, TN_DIMS, preferred_element_type=F32)

        oc = o - jnp.mean(o, -1, keepdims=True)
        var = jnp.mean(oc * oc, -1, keepdims=True)
        o_ref[rows, :] = (_silu(g_ref[rows, :]) * (oc * lax.rsqrt(var + EPS))).astype(o_ref.dtype)


def _retention(proj, n_batch, seq, log_gamma, s0, layer):
    rows = proj.shape[0]
    n_seq = max(1, RET_ROWS // seq)
    hd = HEAD_DIM

    def col(offset):
        return pl.BlockSpec((n_seq * seq, hd), lambda h, b: (b, offset // hd + h))

    return pl.pallas_call(
        functools.partial(_retention_kernel, seq=seq, n_seq=n_seq),
        grid=(N_RET_HEADS, n_batch // n_seq),
        in_specs=[
            pl.BlockSpec(memory_space=pltpu.SMEM),
            col(COL_RQ), col(COL_RK), col(COL_RV), col(COL_RG),
            pl.BlockSpec((n_seq, None, 2, None, hd, hd), lambda h, b: (b, layer, 0, h, 0, 0)),
        ],
        out_specs=[
            pl.BlockSpec((n_seq * seq, hd), lambda h, b: (b, h)),
            pl.BlockSpec((n_seq, 2, None, hd, hd), lambda h, b: (b, 0, h, 0, 0)),
        ],
        out_shape=[
            jax.ShapeDtypeStruct((rows, RET_WIDTH), BF16),
            jax.ShapeDtypeStruct((n_batch, 2, N_RET_HEADS, hd, hd), F32),
        ],
        scratch_shapes=[pltpu.VMEM((seq, seq), F32)],
        compiler_params=_params(2),
        name="retention",
    )(log_gamma.reshape(2 * N_RET_HEADS), proj, proj, proj, proj, s0)


def _alt_sign(n):
    r = lax.broadcasted_iota(jnp.int32, (n, 1), 0)
    return jnp.where((r & 1) == 0, 1.0, -1.0).astype(F32)


def _filter_kernel(z_ref, w1_ref, b1_ref, fr_ref, w2_ref, b2_ref, w3f_ref, w3b_ref, dl_ref,
                   ch_ref, cl_ref, sh_ref, sl_ref, kr_ref, ki_ref, kn_ref, *, seq):
    freq = fr_ref[...]
    hdn = jnp.sin(freq * (_dot3(*_split(z_ref[...]), *_split(w1_ref[...])) + b1_ref[...]))
    hdn = jnp.sin(freq * (_dot3(*_split(hdn), *_split(w2_ref[...])) + b2_ref[...]))
    hdn_hi, hdn_lo = _split(hdn)

    pos = lax.broadcasted_iota(jnp.int32, (seq, 1), 0)
    t = pos.astype(F32) / max(seq - 1, 1)
    decay = jnp.exp(-t * dl_ref[...])
    h_fwd = _dot3(hdn_hi, hdn_lo, *_split(w3f_ref[...])) * decay
    h_bwd = _dot3(hdn_hi, hdn_lo, *_split(w3b_ref[...])) * decay

    h_bwd_shift = jnp.where(pos == 0, 0.0, _shift_rows(h_bwd, down=True))
    alt = _alt_sign(seq)
    kr = _dot3(ch_ref[...], cl_ref[...], *_split(h_fwd + h_bwd_shift)) + alt * h_bwd[seq - 1:seq, :]
    ki = _dot3(sh_ref[...], sl_ref[...], *_split(h_bwd_shift - h_fwd))
    wgt = jnp.where(pos == 0, 1.0, 2.0) * (0.5 / seq)
    kr_ref[...] = kr * wgt
    ki_ref[...] = ki * wgt
    kn_ref[...] = jnp.sum(alt * (h_fwd - h_bwd), axis=0, keepdims=True) * (0.5 / seq)


def _hyena_filters(seq, zfeat, w1, b1, freq, w2, b2, w3, abs_deltas, dft):
    w = HYENA_WIDTH
    const = lambda o: (0, 0)
    once = pl.Buffered(1)
    mat = pl.BlockSpec((seq, seq), const, pipeline_mode=once)
    vec = pl.BlockSpec((1, FILTER_PAD), const)
    return pl.pallas_call(
        functools.partial(_filter_kernel, seq=seq),
        grid=(HYENA_ORDER,),
        in_specs=[
            pl.BlockSpec((seq, FILTER_PAD), const),
            pl.BlockSpec((FILTER_PAD, FILTER_PAD), const), vec, vec,
            pl.BlockSpec((FILTER_PAD, FILTER_PAD), const), vec,
            pl.BlockSpec((FILTER_PAD, w), lambda o: (0, 2 * o)),
            pl.BlockSpec((FILTER_PAD, w), lambda o: (0, 2 * o + 1)),
            pl.BlockSpec((1, w), const),
            mat, mat, mat, mat,
        ],
        out_specs=[
            pl.BlockSpec((None, seq, w), lambda o: (o, 0, 0)),
            pl.BlockSpec((None, seq, w), lambda o: (o, 0, 0)),
            pl.BlockSpec((None, 1, w), lambda o: (o, 0, 0)),
        ],
        out_shape=[
            jax.ShapeDtypeStruct((HYENA_ORDER, seq, HYENA_WIDTH), F32),
            jax.ShapeDtypeStruct((HYENA_ORDER, seq, HYENA_WIDTH), F32),
            jax.ShapeDtypeStruct((HYENA_ORDER, 1, HYENA_WIDTH), F32),
        ],
        compiler_params=_params(1),
        name="hyena_filters",
    )(zfeat, w1, b1, freq, w2, b2, w3, w3, abs_deltas, *dft)


def _dwconv3_seq(x, w, pos, seq):
    prev = jnp.where(pos == 0, 0.0, _shift_rows(x, down=True))
    nxt = jnp.where(pos == seq - 1, 0.0, _shift_rows(x, down=False))
    return prev * w[0:1] + x * w[1:2] + nxt * w[2:3]


def _hyena_kernel(v_ref, x1_ref, x2_ref, cv_ref, c1_ref, c2_ref, bias_ref, kr_ref, ki_ref, kn_ref,
                  cos_ref, sin_ref, o_ref, *, seq, n_seq):
    pos = lax.broadcasted_iota(jnp.int32, (seq, 1), 0)
    alt = _alt_sign(seq)

    def dft(m_ref, a):
        return _dot(m_ref[...], a.astype(BF16))

    for s in range(n_seq):
        rows = slice(s * seq, (s + 1) * seq)
        z = _dwconv3_seq(v_ref[rows, :], cv_ref[...], pos, seq)
        for o, (x_ref, c_ref) in enumerate(((x1_ref, c1_ref), (x2_ref, c2_ref))):
            ur = dft(cos_ref, z)
            ui = dft(sin_ref, z)
            u_nyq = jnp.sum(alt * z, axis=0, keepdims=True)
            kr, ki = kr_ref[o], ki_ref[o]
            yr = ur * kr + ui * ki
            yi = ui * kr - ur * ki
            y = dft(cos_ref, yr) + dft(sin_ref, yi) + alt * (u_nyq * kn_ref[o])
            z = _dwconv3_seq(x_ref[rows, :], c_ref[...], pos, seq) * (y + z * bias_ref[o:o + 1, :])
        o_ref[rows, :] = z.astype(o_ref.dtype)


def _hyena(proj, n_batch, seq, hy_conv, hy_bias, spectra, dft):
    rows = proj.shape[0]
    w = HYENA_WIDTH
    kr, ki, kn = spectra
    once = pl.Buffered(1)
    n_seq = max(1, HYENA_ROWS // seq)

    def part(p):
        return pl.BlockSpec((n_seq * seq, w), lambda b: (b, COL_HY // w + p))

    def conv(p):
        return pl.BlockSpec((3, w), lambda b: (0, p), pipeline_mode=once)

    mat = pl.BlockSpec((seq, seq), lambda b: (0, 0), pipeline_mode=once)
    spec = pl.BlockSpec((HYENA_ORDER, seq, w), lambda b: (0, 0, 0), pipeline_mode=once)
    return pl.pallas_call(
        functools.partial(_hyena_kernel, seq=seq, n_seq=n_seq),
        grid=(n_batch // n_seq,),
        in_specs=[
            part(0), part(1), part(2), conv(0), conv(1), conv(2),
            pl.BlockSpec((HYENA_ORDER, w), lambda b: (0, 0), pipeline_mode=once),
            spec, spec,
            pl.BlockSpec((HYENA_ORDER, 1, w), lambda b: (0, 0, 0), pipeline_mode=once),
            mat, mat,
        ],
        out_specs=pl.BlockSpec((n_seq * seq, w), lambda b: (b, 0)),
        out_shape=jax.ShapeDtypeStruct((rows, w), BF16),
        compiler_params=_params(1),
        name="hyena",
    )(proj, proj, proj, hy_conv, hy_conv, hy_conv, hy_bias, kr, ki, kn, dft[0], dft[2])


def _outproj_kernel(att_ref, ret_ref, hy_ref, x_ref, g_ref, w_ref, lw_ref, lb_ref, o_ref):
    r0, r1 = ATT_WIDTH, ATT_WIDTH + RET_WIDTH
    for t0 in range(0, x_ref.shape[0], OUTPROJ_SUB):
        rows = slice(t0, t0 + OUTPROJ_SUB)
        mix = (_dot(att_ref[rows, :], w_ref[0:r0, :]) + _dot(ret_ref[rows, :], w_ref[r0:r1, :])
               + _dot(hy_ref[rows, :], w_ref[r1:, :]))
        y = DEEPNORM_ALPHA * x_ref[rows, :] + g_ref[...] * mix
        o_ref[rows, :] = _layer_norm(y, lw_ref[...], lb_ref[...])


def _outproj(att, ret, hyz, x, mod, w_out, layer, ln_w, ln_b):
    rows = x.shape[0]
    tm = OUTPROJ_TM
    tiles_per_mod = rows // mod.shape[0] // tm
    row = lambda width: pl.BlockSpec((tm, width), lambda i: (i, 0))
    vec = pl.BlockSpec((1, D_MODEL), lambda i: (0, 0))
    return pl.pallas_call(
        _outproj_kernel,
        grid=(rows // tm,),
        in_specs=[
            row(ATT_WIDTH), row(RET_WIDTH), row(HYENA_WIDTH), row(D_MODEL),
            pl.BlockSpec((None, 1, D_MODEL), lambda i: (i // tiles_per_mod, 0, 2)),
            pl.BlockSpec((None, D_MODEL, D_MODEL), lambda i: (layer, 0, 0), pipeline_mode=pl.Buffered(1)),
            vec, vec,
        ],
        out_specs=row(D_MODEL),
        out_shape=jax.ShapeDtypeStruct((rows, D_MODEL), F32),
        compiler_params=_params(1),
        name="outproj_ln",
    )(att, ret, hyz, x, mod, w_out, ln_w.reshape(1, D_MODEL), ln_b.reshape(1, D_MODEL))


def _ffn_kernel(x_ref, xp_ref, xn_ref, sh_ref, sc_ref, g_ref, wg_ref, wu_ref, cg_ref, cu_ref, wd_ref,
                lw_ref, lb_ref, o_ref, h_scr, *, seq, tm):
    i = pl.program_id(0)
    j = pl.program_id(1)
    ends_inside = seq < tm

    @pl.when(j == 0)
    def _():
        sc = 1.0 + sc_ref[...]
        sh = sh_ref[...]
        h_prev = xp_ref[...] * sc + sh
        h_next = xn_ref[...] * sc + sh
        if not ends_inside:
            h_prev = jnp.where(((i * tm) & (seq - 1)) == 0, 0.0, h_prev)
            h_next = jnp.where((((i + 1) * tm) & (seq - 1)) == 0, 0.0, h_next)
        h_scr[0:HALO, :] = h_prev.astype(BF16)
        h_scr[HALO:HALO + tm, :] = (x_ref[...] * sc + sh).astype(BF16)
        h_scr[HALO + tm:, :] = h_next.astype(BF16)
        o_ref[...] = jnp.zeros_like(o_ref)

    pos = (i * tm + lax.broadcasted_iota(jnp.int32, (tm, 1), 0)) & (seq - 1)

    def up_conv(w_ref, c_ref):
        up = _dot(h_scr[...], w_ref[...])
        prev = _shift_rows(up, down=True)[HALO:HALO + tm]
        nxt = _shift_rows(up, down=False)[HALO:HALO + tm]
        if ends_inside:
            prev = jnp.where(pos == 0, 0.0, prev)
            nxt = jnp.where(pos == seq - 1, 0.0, nxt)
        c = c_ref[...]
        return prev * c[0:1] + up[HALO:HALO + tm] * c[1:2] + nxt * c[2:3]

    act = _silu(up_conv(wg_ref, cg_ref)) * up_conv(wu_ref, cu_ref)
    o_ref[...] += _dot(act.astype(BF16), wd_ref[...])

    @pl.when(j == pl.num_programs(1) - 1)
    def _():
        y = DEEPNORM_ALPHA * x_ref[...] + g_ref[...] * o_ref[...]
        o_ref[...] = _layer_norm(y, lw_ref[...], lb_ref[...])


def _ffn(x, seq, mod, w_up, ffn_conv, w_down, layer, ln_w, ln_b):
    rows = x.shape[0]
    tm, tf = FFN_TM, FFN_TF
    n_f = D_FF // tf
    tiles_per_mod = rows // mod.shape[0] // tm
    halo_per_tile = tm // HALO
    last_halo = rows // HALO - 1

    def mod_spec(chunk):
        return pl.BlockSpec((None, 1, D_MODEL), lambda i, j: (i // tiles_per_mod, 0, chunk))

    vec = pl.BlockSpec((1, D_MODEL), lambda i, j: (0, 0))
    return pl.pallas_call(
        functools.partial(_ffn_kernel, seq=seq, tm=tm),
        grid=(rows // tm, n_f),
        in_specs=[
            pl.BlockSpec((tm, D_MODEL), lambda i, j: (i, 0), pipeline_mode=pl.Buffered(1)),
            pl.BlockSpec((HALO, D_MODEL), lambda i, j: (jnp.maximum(i * halo_per_tile - 1, 0), 0)),
            pl.BlockSpec((HALO, D_MODEL), lambda i, j: (jnp.minimum((i + 1) * halo_per_tile, last_halo), 0)),
            mod_spec(3), mod_spec(4), mod_spec(5),
            pl.BlockSpec((D_MODEL, tf), lambda i, j: (0, j)),
            pl.BlockSpec((D_MODEL, tf), lambda i, j: (0, n_f + j)),
            pl.BlockSpec((None, 3, tf), lambda i, j: (layer, 0, j)),
            pl.BlockSpec((None, 3, tf), lambda i, j: (layer, 0, n_f + j)),
            pl.BlockSpec((tf, D_MODEL), lambda i, j: (j, 0)),
            vec, vec,
        ],
        out_specs=pl.BlockSpec((tm, D_MODEL), lambda i, j: (i, 0)),
        out_shape=jax.ShapeDtypeStruct((rows, D_MODEL), F32),
        scratch_shapes=[pltpu.VMEM((tm + 2 * HALO, D_MODEL), BF16)],
        compiler_params=_params(2),
        name="ffn_ln",
    )(x, x, x, mod, mod, mod, w_up, w_up, ffn_conv, ffn_conv, w_down,
      ln_w.reshape(1, D_MODEL), ln_b.reshape(1, D_MODEL))


def _rope_tables(n_tokens):
    t = np.arange(n_tokens)
    n_freq = HEAD_DIM // 4
    inv_freq = ROPE_THETA ** (-np.arange(n_freq) / n_freq)
    ang = np.concatenate([(t // GRID_W)[:, None] * inv_freq[None], (t % GRID_W)[:, None] * inv_freq[None]], -1)
    cos, sin = np.cos(ang), np.sin(ang)
    return (jnp.asarray(np.concatenate([cos, cos], -1), F32), jnp.asarray(np.concatenate([-sin, sin], -1), F32))


def _dft_tables(seq):
    idx = np.arange(seq)
    ang = ((idx[:, None] * idx[None, :]) % (2 * seq)) * (math.pi / seq)
    return (*_split(jnp.asarray(np.cos(ang), F32)), *_split(jnp.asarray(np.sin(ang), F32)))


def _filter_features(seq):
    pos = np.arange(seq)
    t = pos / max(seq - 1, 1)
    ang = (2.0 * math.pi * pos / seq)[:, None] * np.linspace(1e-4, FILTER_BANDS - 1, FILTER_BANDS)[None, :]
    z = np.concatenate([t[:, None], np.cos(ang), -np.sin(ang)], -1)
    return jnp.asarray(np.pad(z, ((0, 0), (0, FILTER_PAD - FILTER_EMB))), F32)


def _pad_to(a, shape):
    return jnp.pad(a, [(0, s - d) for d, s in zip(a.shape, shape)])


def kernel(x_prompt, x_sample, cache_k, cache_v, state_ret, c, c_ctx, w_ada, b_ada, w_in, q_norm, k_norm, ret_decay, hy_conv, hf_w1, hf_b1, hf_freq, hf_w2, hf_b2, hf_w3, hy_bias, w_out, ln1_w, ln1_b, w_up, ffn_conv, w_down, ln2_w, ln2_b):
    n_ctx, seq_ctx, _ = x_prompt.shape
    n_lat, seq_lat, _ = x_sample.shape

    cond = _pad_to(jnp.concatenate([c_ctx[None, :], c], 0), (COND_ROWS, D_MODEL))
    mod_all = _ada(cond, w_ada, b_ada)

    w_in_b, w_out_b = w_in.astype(BF16), w_out.astype(BF16)
    log_gamma = jax.nn.log_sigmoid(ret_decay.astype(F32))
    abs_deltas = jnp.asarray(np.abs(np.linspace(math.log(HYENA_DECAY_TARGET) / HYENA_DECAY_PCT_MIN,
                                                math.log(HYENA_DECAY_TARGET) / HYENA_DECAY_PCT_MAX,
                                                HYENA_WIDTH))[None, :], F32)
    ret_zero = jnp.zeros((n_ctx, 1, 2, N_RET_HEADS, HEAD_DIM, HEAD_DIM), F32)

    groups = {
        "ctx": dict(n=n_ctx, seq=seq_ctx, rope=None, dft=_dft_tables(seq_ctx), feat=_filter_features(seq_ctx)),
        "lat": dict(n=n_lat, seq=seq_lat, rope=_rope_tables(seq_lat), dft=_dft_tables(seq_lat),
                    feat=_filter_features(seq_lat)),
    }
    xs = {"ctx": x_prompt.reshape(n_ctx * seq_ctx, D_MODEL), "lat": x_sample.reshape(n_lat * seq_lat, D_MODEL)}
    ks_out, vs_out, ss_out = [], [], []

    for l in range(DEPTH):
        mods = {"ctx": mod_all[l, 0:1, None, :], "lat": mod_all[l, 1:1 + n_lat, None, :]}
        pad2 = (FILTER_PAD, FILTER_PAD)
        filt_w = (_pad_to(hf_w1[l], pad2), _pad_to(hf_b1[l][None, :], (1, FILTER_PAD)),
                  _pad_to(hf_freq[l][None, :], (1, FILTER_PAD)), _pad_to(hf_w2[l], pad2),
                  _pad_to(hf_b2[l][None, :], (1, FILTER_PAD)), _pad_to(hf_w3[l], (FILTER_PAD, hf_w3.shape[-1])))
        proj_ctx, w_down_b = _inproj(xs["ctx"], mods["ctx"], w_in_b, l, w_down)
        proj_lat, w_up_b = _inproj(xs["lat"], mods["lat"], w_in_b, l, w_up)
        projs = {"ctx": proj_ctx, "lat": proj_lat}
        for name, g in groups.items():
            n, seq, x, mod, proj = g["n"], g["seq"], xs[name], mods[name], projs[name]
            is_ctx = name == "ctx"
            att_out = _attention(proj, n, seq, q_norm[l], k_norm[l], g["rope"],
                                 None if is_ctx else (cache_k, cache_v), l, emit_cache=is_ctx)
            ret, s_fin = _retention(proj, n, seq, log_gamma[l],
                                    ret_zero if is_ctx else state_ret, 0 if is_ctx else l)
            spectra = _hyena_filters(seq, g["feat"], *filt_w, abs_deltas, g["dft"])
            hyz = _hyena(proj, n, seq, hy_conv[l], hy_bias[l], spectra, g["dft"])
            if is_ctx:
                att, own_k, own_v = att_out
                ks_out.append(own_k)
                vs_out.append(own_v)
                ss_out.append(s_fin)
            else:
                att = att_out[0]
            x = _outproj(att, ret, hyz, x, mod, w_out_b, l, ln1_w[l], ln1_b[l])
            xs[name] = _ffn(x, seq, mod, w_up_b, ffn_conv, w_down_b, l, ln2_w[l], ln2_b[l])

    return (xs["ctx"].reshape(n_ctx, seq_ctx, D_MODEL), xs["lat"].reshape(n_lat, seq_lat, D_MODEL),
            jnp.stack(ks_out, axis=1), jnp.stack(vs_out, axis=1), jnp.stack(ss_out, axis=1))
```

```python
import functools
import math

import numpy as np
import jax
import jax.numpy as jnp
from jax import lax
from jax.experimental import pallas as pl
from jax.experimental.pallas import tpu as pltpu

F32 = jnp.float32
BF16 = jnp.bfloat16

D_MODEL = 2048
DEPTH = 2
GRID_W = 64
HEAD_DIM = 128
ATT_WIDTH = D_MODEL // 2
N_ATT_HEADS = ATT_WIDTH // HEAD_DIM
N_KV_HEADS = N_ATT_HEADS // 4
GQA_GROUP = N_ATT_HEADS // N_KV_HEADS
KV_WIDTH = N_KV_HEADS * HEAD_DIM
RET_WIDTH = D_MODEL // 4
N_RET_HEADS = RET_WIDTH // HEAD_DIM
HYENA_WIDTH = D_MODEL // 4
HYENA_ORDER = 2
IN_WIDTH = ATT_WIDTH + 2 * KV_WIDTH + 4 * RET_WIDTH + (HYENA_ORDER + 1) * HYENA_WIDTH
D_FF = 5632
ROPE_THETA = 10000.0
FILTER_BANDS = 16
FILTER_EMB = 1 + 2 * FILTER_BANDS
FILTER_HID = 64
HYENA_DECAY_TARGET = 1e-2
HYENA_DECAY_PCT_MIN = 0.3
HYENA_DECAY_PCT_MAX = 1.5
DEEPNORM_ALPHA = (2 * DEPTH) ** 0.25
EPS = 1e-6

COL_Q = 0
COL_K = ATT_WIDTH
COL_V = COL_K + KV_WIDTH
COL_RQ = COL_V + KV_WIDTH
COL_RK = COL_RQ + RET_WIDTH
COL_RV = COL_RK + RET_WIDTH
COL_RG = COL_RV + RET_WIDTH
COL_HY = COL_RG + RET_WIDTH

LANES = 128
HALO = 16
VMEM_LIMIT = 60 * 1024 * 1024
COND_ROWS = 16

ADA_TN = 1024
INPROJ_TM, INPROJ_TN = 256, 512
OUTPROJ_TM, OUTPROJ_SUB = 512, 256
FFN_TM, FFN_TF = 1024, 512
ATT_TQ, ATT_CHAIN = 1024, 256
ATT_ROWS = 1024
ATT_MXU_SUM_MIN_KEYS = 1024
RET_ROWS = 2048
HYENA_ROWS = 1024
FILTER_PAD = 128

NT_DIMS = (((1,), (1,)), ((), ()))
TN_DIMS = (((0,), (0,)), ((), ()))


def _params(n_grid):
    return pltpu.CompilerParams(dimension_semantics=("arbitrary",) * n_grid,
                                vmem_limit_bytes=VMEM_LIMIT)


def _dot(a, b):
    return jnp.dot(a, b, preferred_element_type=F32)


def _split(a):
    hi = a.astype(BF16)
    lo = (a - hi.astype(F32)).astype(BF16)
    return hi, lo


def _dot3(a_hi, a_lo, b_hi, b_lo):
    return _dot(a_hi, b_hi) + _dot(a_lo, b_hi) + _dot(a_hi, b_lo)


def _silu(x):
    return x * jax.nn.sigmoid(x)


def _layer_norm(y, w, b):
    yc = y - jnp.mean(y, -1, keepdims=True)
    var = jnp.mean(yc * yc, -1, keepdims=True)
    return yc * lax.rsqrt(var + EPS) * w + b


def _shift_rows(x, down):
    n = x.shape[0]
    return pltpu.roll(x, 1 if down else n - 1, axis=0)


def _ada_kernel(cond_ref, w_ref, b_ref, o_ref):
    a = _silu(cond_ref[...]).astype(BF16)
    o_ref[...] = _dot(a, w_ref[...].astype(BF16)) + b_ref[...]


def _ada(cond, w_ada, b_ada):
    n_out = w_ada.shape[-1]
    return pl.pallas_call(
        _ada_kernel,
        grid=(DEPTH, n_out // ADA_TN),
        in_specs=[
            pl.BlockSpec((COND_ROWS, D_MODEL), lambda l, j: (0, 0)),
            pl.BlockSpec((None, D_MODEL, ADA_TN), lambda l, j: (l, 0, j)),
            pl.BlockSpec((None, 1, ADA_TN), lambda l, j: (l, 0, j)),
        ],
        out_specs=pl.BlockSpec((None, COND_ROWS, ADA_TN), lambda l, j: (l, 0, j)),
        out_shape=jax.ShapeDtypeStruct((DEPTH, COND_ROWS, n_out), F32),
        compiler_params=_params(2),
        name="ada",
    )(cond, w_ada, b_ada.reshape(DEPTH, 1, n_out))


def _inproj_kernel(x_ref, sh_ref, sc_ref, w_ref, *refs):
    n_casts = len(refs) // 2
    src_refs, o_ref, cast_refs = refs[:n_casts], refs[n_casts], refs[n_casts + 1:]
    h = (x_ref[...] * (1.0 + sc_ref[...]) + sh_ref[...]).astype(BF16)
    for n0 in range(0, IN_WIDTH, INPROJ_TN):
        o_ref[:, n0:n0 + INPROJ_TN] = _dot(h, w_ref[:, n0:n0 + INPROJ_TN])
    for src_ref, cast_ref in zip(src_refs, cast_refs):
        cast_ref[...] = src_ref[...].astype(BF16)


def _inproj(x, mod, w_in, casts):
    rows = x.shape[0]
    tm = INPROJ_TM
    n_steps = rows // tm
    tiles_per_mod = n_steps // mod.shape[0]
    cast_in, cast_out, cast_shape = [], [], []
    for src, layer in casts:
        _, k, n = src.shape
        band = k // n_steps
        assert band * n_steps == k and band % HALO == 0, (k, n_steps)
        cast_in.append(pl.BlockSpec((None, band, n), lambda i, layer=layer: (layer, i, 0)))
        cast_out.append(pl.BlockSpec((band, n), lambda i: (i, 0)))
        cast_shape.append(jax.ShapeDtypeStruct((k, n), BF16))
    proj, *cast = pl.pallas_call(
        _inproj_kernel,
        grid=(n_steps,),
        in_specs=[
            pl.BlockSpec((tm, D_MODEL), lambda i: (i, 0)),
            pl.BlockSpec((None, 1, D_MODEL), lambda i: (i // tiles_per_mod, 0, 0)),
            pl.BlockSpec((None, 1, D_MODEL), lambda i: (i // tiles_per_mod, 0, 1)),
            pl.BlockSpec((D_MODEL, IN_WIDTH), lambda i: (0, 0), pipeline_mode=pl.Buffered(1)),
            *cast_in,
        ],
        out_specs=[pl.BlockSpec((tm, IN_WIDTH), lambda i: (i, 0)), *cast_out],
        out_shape=[jax.ShapeDtypeStruct((rows, IN_WIDTH), F32), *cast_shape],
        compiler_params=_params(1),
        name="inproj",
    )(x, mod, mod, w_in, *[src for src, _ in casts])
    return proj, cast


def _rms_norm(x, w):
    return x * lax.rsqrt(jnp.mean(x * x, -1, keepdims=True) + EPS) * w


def _rope(x, cos_full, sin_signed):
    return x * cos_full + pltpu.roll(x, HEAD_DIM // 2, axis=1) * sin_signed


def _attn_kernel(*refs, seq, n_seq, rope, has_ctx, emit_cache):
    refs = list(refs)
    q_ref, k_ref, v_ref, qw_ref, kw_ref = refs[:5]
    del refs[:5]
    if rope:
        qcos_ref, qsin_ref, kcos_ref, ksin_ref = refs[:4]
        del refs[:4]
    if has_ctx:
        ck_ref, cv_ref = refs[:2]
        del refs[:2]
    o_ref = refs.pop(0)
    if emit_cache:
        ownk_ref, ownv_ref = refs[:2]
        del refs[:2]
    k_scr, v_scr = refs
    sums_on_mxu = v_scr.shape[1] > HEAD_DIM
    own = n_seq * seq

    @pl.when(pl.program_id(2) == 0)
    def _():
        kn = _rms_norm(k_ref[...], kw_ref[...])
        if emit_cache:
            for s in range(n_seq):
                ownk_ref[s] = kn[s * seq:(s + 1) * seq]
                ownv_ref[s] = v_ref[s * seq:(s + 1) * seq, :]
        if rope:
            kn = _rope(kn, kcos_ref[...], ksin_ref[...])
        k_scr[0:own, :] = kn.astype(BF16)
        v_scr[0:own, 0:HEAD_DIM] = v_ref[...].astype(BF16)
        if has_ctx:
            k_scr[own:, :] = ck_ref[...].astype(BF16)
            v_scr[own:, 0:HEAD_DIM] = cv_ref[...].astype(BF16)
        if sums_on_mxu:
            v_scr[:, HEAD_DIM:] = jnp.ones((v_scr.shape[0], HEAD_DIM), BF16)

    tq = q_ref.shape[0] // n_seq
    for s in range(n_seq):
        keys = slice(s * seq, (s + 1) * seq) if n_seq > 1 else slice(None)
        for g in range(GQA_GROUP):
            cols = slice(g * HEAD_DIM, (g + 1) * HEAD_DIM)
            for r0 in range(s * tq, (s + 1) * tq, ATT_CHAIN):
                rows = slice(r0, r0 + ATT_CHAIN)
                qn = _rms_norm(q_ref[rows, cols], qw_ref[...])
                if rope:
                    qn = _rope(qn, qcos_ref[rows, :], qsin_ref[rows, :])
                qb = (qn * (HEAD_DIM ** -0.5)).astype(BF16)
                sc = lax.dot_general(qb, k_scr[keys, :], NT_DIMS, preferred_element_type=F32)
                p = jnp.exp(sc - jnp.max(sc, -1, keepdims=True))
                o = _dot(p.astype(BF16), v_scr[keys, :])
                den = o[:, HEAD_DIM:HEAD_DIM + 1] if sums_on_mxu else jnp.sum(p, -1, keepdims=True)
                o_ref[rows, cols] = (o[:, :HEAD_DIM] / den).astype(o_ref.dtype)


def _attention(proj, n_batch, seq, q_norm, k_norm, rope_tabs, ctx_kv, layer, emit_cache):
    rows = proj.shape[0]
    tq = min(ATT_TQ, seq)
    nq = seq // tq
    group_w = GQA_GROUP * HEAD_DIM
    rope = rope_tabs is not None
    has_ctx = ctx_kv is not None
    n_seq = max(1, ATT_ROWS // seq) if not (rope or has_ctx) else 1
    keys_per_chain = seq + (ctx_kv[0].shape[3] if has_ctx else 0)
    value_width = 2 * HEAD_DIM if keys_per_chain >= ATT_MXU_SUM_MIN_KEYS else HEAD_DIM

    in_specs = [
        pl.BlockSpec((n_seq * tq, group_w), lambda b, kv, qi: (b * nq + qi, COL_Q // group_w + kv)),
        pl.BlockSpec((n_seq * seq, HEAD_DIM), lambda b, kv, qi: (b, COL_K // HEAD_DIM + kv)),
        pl.BlockSpec((n_seq * seq, HEAD_DIM), lambda b, kv, qi: (b, COL_V // HEAD_DIM + kv)),
        pl.BlockSpec((1, HEAD_DIM), lambda b, kv, qi: (0, 0)),
        pl.BlockSpec((1, HEAD_DIM), lambda b, kv, qi: (0, 0)),
    ]
    args = [proj, proj, proj, q_norm.reshape(1, HEAD_DIM), k_norm.reshape(1, HEAD_DIM)]
    if rope:
        cos_full, sin_signed = rope_tabs
        in_specs += [
            pl.BlockSpec((tq, HEAD_DIM), lambda b, kv, qi: (qi, 0)),
            pl.BlockSpec((tq, HEAD_DIM), lambda b, kv, qi: (qi, 0)),
            pl.BlockSpec((seq, HEAD_DIM), lambda b, kv, qi: (0, 0)),
            pl.BlockSpec((seq, HEAD_DIM), lambda b, kv, qi: (0, 0)),
        ]
        args += [cos_full, sin_signed, cos_full, sin_signed]
    if has_ctx:
        ck, cv = ctx_kv
        past = ck.shape[3]
        ctx_spec = pl.BlockSpec((None, None, None, past, HEAD_DIM), lambda b, kv, qi: (b, layer, kv, 0, 0))
        in_specs += [ctx_spec, ctx_spec]
        args += [ck, cv]

    out_specs = [pl.BlockSpec((n_seq * tq, group_w), lambda b, kv, qi: (b * nq + qi, kv))]
    out_shape = [jax.ShapeDtypeStruct((rows, ATT_WIDTH), BF16)]
    if emit_cache:
        own_spec = pl.BlockSpec((n_seq, None, seq, HEAD_DIM), lambda b, kv, qi: (b, kv, 0, 0))
        out_specs += [own_spec, own_spec]
        out_shape += [jax.ShapeDtypeStruct((n_batch, N_KV_HEADS, seq, HEAD_DIM), F32)] * 2

    return pl.pallas_call(
        functools.partial(_attn_kernel, seq=seq, n_seq=n_seq, rope=rope, has_ctx=has_ctx, emit_cache=emit_cache),
        grid=(n_batch // n_seq, N_KV_HEADS, nq),
        in_specs=in_specs,
        out_specs=out_specs,
        out_shape=out_shape,
        scratch_shapes=[pltpu.VMEM((n_seq * seq + (past if has_ctx else 0), width), BF16)
                        for width in (HEAD_DIM, value_width)],
        compiler_params=_params(3),
        name="attention",
    )(*args)


def _retention_kernel(lg_ref, q_ref, k_ref, v_ref, g_ref, s0_ref, o_ref, sf_ref, decay_scr, *, seq, n_seq):
    h = pl.program_id(0)
    lg_f = lg_ref[h]
    lg_b = lg_ref[N_RET_HEADS + h]

    @pl.when(pl.program_id(1) == 0)
    def _():
        i = lax.broadcasted_iota(jnp.int32, (seq, seq), 0)
        j = lax.broadcasted_iota(jnp.int32, (seq, seq), 1)
        d = (i - j).astype(F32)
        decay = jnp.exp(jnp.where(d >= 0, lg_f, lg_b) * jnp.abs(d))
        decay_scr[...] = jnp.where(d == 0, 2.0, decay)

    pos = lax.broadcasted_iota(jnp.int32, (seq, 1), 0).astype(F32)
    q_decay_f, q_decay_b = jnp.exp(lg_f * (pos + 1.0)), jnp.exp(lg_b * (seq - pos))
    k_decay_f, k_decay_b = jnp.exp(lg_f * (seq - 1.0 - pos)), jnp.exp(lg_b * pos)
    ones = jnp.ones((1, HEAD_DIM), F32)
    s_decay_f, s_decay_b = jnp.exp(ones * (lg_f * seq)), jnp.exp(ones * (lg_b * seq))

    for s in range(n_seq):
        rows = slice(s * seq, (s + 1) * seq)
        q = q_ref[rows, :]
        k = k_ref[rows, :] * (HEAD_DIM ** -0.5)
        vb = v_ref[rows, :].astype(BF16)
        a = lax.dot_general(q.astype(BF16), k.astype(BF16), NT_DIMS, preferred_element_type=F32)
        o = _dot((a * decay_scr[...]).astype(BF16), vb)
        s0_f = s0_ref[s, 0]
        s0_b = s0_ref[s, 1]
        o = o + _dot((q * q_decay_f).astype(BF16), s0_f.astype(BF16))
        o = o + _dot((q * q_decay_b).astype(BF16), s0_b.astype(BF16))

        kf = (k * k_decay_f).astype(BF16)
        kb = (k * k_decay_b).astype(BF16)
        sf_ref[s, 0] = s_decay_f * s0_f + lax.dot_general(kf, vb, TN_DIMS, preferred_element_type=F32)
        sf_ref[s, 1] = s_decay_b * s0_b + lax.dot_general(kb, vb, TN_DIMS, preferred_element_type=F32)

        oc = o - jnp.mean(o, -1, keepdims=True)
        var = jnp.mean(oc * oc, -1, keepdims=True)
        o_ref[rows, :] = (_silu(g_ref[rows, :]) * (oc * lax.rsqrt(var + EPS))).astype(o_ref.dtype)


def _retention(proj, n_batch, seq, log_gamma, s0, layer):
    rows = proj.shape[0]
    n_seq = max(1, RET_ROWS // seq)
    hd = HEAD_DIM

    def col(offset):
        return pl.BlockSpec((n_seq * seq, hd), lambda h, b: (b, offset // hd + h))

    return pl.pallas_call(
        functools.partial(_retention_kernel, seq=seq, n_seq=n_seq),
        grid=(N_RET_HEADS, n_batch // n_seq),
        in_specs=[
            pl.BlockSpec(memory_space=pltpu.SMEM),
            col(COL_RQ), col(COL_RK), col(COL_RV), col(COL_RG),
            pl.BlockSpec((n_seq, None, 2, None, hd, hd), lambda h, b: (b, layer, 0, h, 0, 0)),
        ],
        out_specs=[
            pl.BlockSpec((n_seq * seq, hd), lambda h, b: (b, h)),
            pl.BlockSpec((n_seq, 2, None, hd, hd), lambda h, b: (b, 0, h, 0, 0)),
        ],
        out_shape=[
            jax.ShapeDtypeStruct((rows, RET_WIDTH), BF16),
            jax.ShapeDtypeStruct((n_batch, 2, N_RET_HEADS, hd, hd), F32),
        ],
        scratch_shapes=[pltpu.VMEM((seq, seq), F32)],
        compiler_params=_params(2),
        name="retention",
    )(log_gamma.reshape(2 * N_RET_HEADS), proj, proj, proj, proj, s0)


def _alt_sign(n):
    r = lax.broadcasted_iota(jnp.int32, (n, 1), 0)
    return jnp.where((r & 1) == 0, 1.0, -1.0).astype(F32)


def _filter_kernel(z_ref, w1_ref, b1_ref, fr_ref, w2_ref, b2_ref, w3f_ref, w3b_ref, dl_ref,
                   ch_ref, cl_ref, sh_ref, sl_ref, kr_ref, ki_ref, kn_ref, *, seq):
    freq = fr_ref[...]
    hdn = jnp.sin(freq * (_dot3(*_split(z_ref[...]), *_split(w1_ref[...])) + b1_ref[...]))
    hdn = jnp.sin(freq * (_dot3(*_split(hdn), *_split(w2_ref[...])) + b2_ref[...]))
    hdn_hi, hdn_lo = _split(hdn)

    pos = lax.broadcasted_iota(jnp.int32, (seq, 1), 0)
    t = pos.astype(F32) / max(seq - 1, 1)
    decay = jnp.exp(-t * dl_ref[...])
    h_fwd = _dot3(hdn_hi, hdn_lo, *_split(w3f_ref[...])) * decay
    h_bwd = _dot3(hdn_hi, hdn_lo, *_split(w3b_ref[...])) * decay

    h_bwd_shift = jnp.where(pos == 0, 0.0, _shift_rows(h_bwd, down=True))
    alt = _alt_sign(seq)
    kr = _dot3(ch_ref[...], cl_ref[...], *_split(h_fwd + h_bwd_shift)) + alt * h_bwd[seq - 1:seq, :]
    ki = _dot3(sh_ref[...], sl_ref[...], *_split(h_bwd_shift - h_fwd))
    wgt = jnp.where(pos == 0, 1.0, 2.0) * (0.5 / seq)
    kr_ref[...] = kr * wgt
    ki_ref[...] = ki * wgt
    kn_ref[...] = jnp.sum(alt * (h_fwd - h_bwd), axis=0, keepdims=True) * (0.5 / seq)


def _hyena_filters(seq, zfeat, w1, b1, freq, w2, b2, w3, abs_deltas, dft):
    w = HYENA_WIDTH
    const = lambda o: (0, 0)
    once = pl.Buffered(1)
    mat = pl.BlockSpec((seq, seq), const, pipeline_mode=once)
    vec = pl.BlockSpec((1, FILTER_PAD), const)
    return pl.pallas_call(
        functools.partial(_filter_kernel, seq=seq),
        grid=(HYENA_ORDER,),
        in_specs=[
            pl.BlockSpec((seq, FILTER_PAD), const),
            pl.BlockSpec((FILTER_PAD, FILTER_PAD), const), vec, vec,
            pl.BlockSpec((FILTER_PAD, FILTER_PAD), const), vec,
            pl.BlockSpec((FILTER_PAD, w), lambda o: (0, 2 * o)),
            pl.BlockSpec((FILTER_PAD, w), lambda o: (0, 2 * o + 1)),
            pl.BlockSpec((1, w), const),
            mat, mat, mat, mat,
        ],
        out_specs=[
            pl.BlockSpec((None, seq, w), lambda o: (o, 0, 0)),
            pl.BlockSpec((None, seq, w), lambda o: (o, 0, 0)),
            pl.BlockSpec((None, 1, w), lambda o: (o, 0, 0)),
        ],
        out_shape=[
            jax.ShapeDtypeStruct((HYENA_ORDER, seq, HYENA_WIDTH), F32),
            jax.ShapeDtypeStruct((HYENA_ORDER, seq, HYENA_WIDTH), F32),
            jax.ShapeDtypeStruct((HYENA_ORDER, 1, HYENA_WIDTH), F32),
        ],
        compiler_params=_params(1),
        name="hyena_filters",
    )(zfeat, w1, b1, freq, w2, b2, w3, w3, abs_deltas, *dft)


def _dwconv3_seq(x, w, pos, seq):
    prev = jnp.where(pos == 0, 0.0, _shift_rows(x, down=True))
    nxt = jnp.where(pos == seq - 1, 0.0, _shift_rows(x, down=False))
    return prev * w[0:1] + x * w[1:2] + nxt * w[2:3]


def _hyena_kernel(v_ref, x1_ref, x2_ref, cv_ref, c1_ref, c2_ref, bias_ref, kr_ref, ki_ref, kn_ref,
                  cos_ref, sin_ref, o_ref, *, seq, n_seq):
    pos = lax.broadcasted_iota(jnp.int32, (seq, 1), 0)
    alt = _alt_sign(seq)

    def dft(m_ref, a):
        return _dot(m_ref[...], a.astype(BF16))

    for s in range(n_seq):
        rows = slice(s * seq, (s + 1) * seq)
        z = _dwconv3_seq(v_ref[rows, :], cv_ref[...], pos, seq)
        for o, (x_ref, c_ref) in enumerate(((x1_ref, c1_ref), (x2_ref, c2_ref))):
            ur = dft(cos_ref, z)
            ui = dft(sin_ref, z)
            u_nyq = jnp.sum(alt * z, axis=0, keepdims=True)
            kr, ki = kr_ref[o], ki_ref[o]
            yr = ur * kr + ui * ki
            yi = ui * kr - ur * ki
            y = dft(cos_ref, yr) + dft(sin_ref, yi) + alt * (u_nyq * kn_ref[o])
            z = _dwconv3_seq(x_ref[rows, :], c_ref[...], pos, seq) * (y + z * bias_ref[o:o + 1, :])
        o_ref[rows, :] = z.astype(o_ref.dtype)


def _hyena(proj, n_batch, seq, hy_conv, hy_bias, spectra, dft):
    rows = proj.shape[0]
    w = HYENA_WIDTH
    kr, ki, kn = spectra
    once = pl.Buffered(1)
    n_seq = max(1, HYENA_ROWS // seq)

    def part(p):
        return pl.BlockSpec((n_seq * seq, w), lambda b: (b, COL_HY // w + p))

    def conv(p):
        return pl.BlockSpec((3, w), lambda b: (0, p), pipeline_mode=once)

    mat = pl.BlockSpec((seq, seq), lambda b: (0, 0), pipeline_mode=once)
    spec = pl.BlockSpec((HYENA_ORDER, seq, w), lambda b: (0, 0, 0), pipeline_mode=once)
    return pl.pallas_call(
        functools.partial(_hyena_kernel, seq=seq, n_seq=n_seq),
        grid=(n_batch // n_seq,),
        in_specs=[
            part(0), part(1), part(2), conv(0), conv(1), conv(2),
            pl.BlockSpec((HYENA_ORDER, w), lambda b: (0, 0), pipeline_mode=once),
            spec, spec,
            pl.BlockSpec((HYENA_ORDER, 1, w), lambda b: (0, 0, 0), pipeline_mode=once),
            mat, mat,
        ],
        out_specs=pl.BlockSpec((n_seq * seq, w), lambda b: (b, 0)),
        out_shape=jax.ShapeDtypeStruct((rows, w), BF16),
        compiler_params=_params(1),
        name="hyena",
    )(proj, proj, proj, hy_conv, hy_conv, hy_conv, hy_bias, kr, ki, kn, dft[0], dft[2])


def _outproj_kernel(att_ref, ret_ref, hy_ref, x_ref, g_ref, w_ref, lw_ref, lb_ref, o_ref):
    r0, r1 = ATT_WIDTH, ATT_WIDTH + RET_WIDTH
    for t0 in range(0, x_ref.shape[0], OUTPROJ_SUB):
        rows = slice(t0, t0 + OUTPROJ_SUB)
        mix = (_dot(att_ref[rows, :], w_ref[0:r0, :]) + _dot(ret_ref[rows, :], w_ref[r0:r1, :])
               + _dot(hy_ref[rows, :], w_ref[r1:, :]))
        y = DEEPNORM_ALPHA * x_ref[rows, :] + g_ref[...] * mix
        o_ref[rows, :] = _layer_norm(y, lw_ref[...], lb_ref[...])


def _outproj(att, ret, hyz, x, mod, w_out, ln_w, ln_b):
    rows = x.shape[0]
    tm = OUTPROJ_TM
    tiles_per_mod = rows // mod.shape[0] // tm
    row = lambda width: pl.BlockSpec((tm, width), lambda i: (i, 0))
    vec = pl.BlockSpec((1, D_MODEL), lambda i: (0, 0))
    return pl.pallas_call(
        _outproj_kernel,
        grid=(rows // tm,),
        in_specs=[
            row(ATT_WIDTH), row(RET_WIDTH), row(HYENA_WIDTH), row(D_MODEL),
            pl.BlockSpec((None, 1, D_MODEL), lambda i: (i // tiles_per_mod, 0, 2)),
            pl.BlockSpec((D_MODEL, D_MODEL), lambda i: (0, 0), pipeline_mode=pl.Buffered(1)),
            vec, vec,
        ],
        out_specs=row(D_MODEL),
        out_shape=jax.ShapeDtypeStruct((rows, D_MODEL), F32),
        compiler_params=_params(1),
        name="outproj_ln",
    )(att, ret, hyz, x, mod, w_out, ln_w.reshape(1, D_MODEL), ln_b.reshape(1, D_MODEL))


def _ffn_kernel(x_ref, xp_ref, xn_ref, sh_ref, sc_ref, g_ref, wg_ref, wu_ref, cg_ref, cu_ref, wd_ref,
                lw_ref, lb_ref, o_ref, h_scr, *, seq, tm):
    i = pl.program_id(0)
    j = pl.program_id(1)
    ends_inside = seq < tm

    @pl.when(j == 0)
    def _():
        sc = 1.0 + sc_ref[...]
        sh = sh_ref[...]
        h_prev = xp_ref[...] * sc + sh
        h_next = xn_ref[...] * sc + sh
        if not ends_inside:
            h_prev = jnp.where(((i * tm) & (seq - 1)) == 0, 0.0, h_prev)
            h_next = jnp.where((((i + 1) * tm) & (seq - 1)) == 0, 0.0, h_next)
        h_scr[0:HALO, :] = h_prev.astype(BF16)
        h_scr[HALO:HALO + tm, :] = (x_ref[...] * sc + sh).astype(BF16)
        h_scr[HALO + tm:, :] = h_next.astype(BF16)
        o_ref[...] = jnp.zeros_like(o_ref)

    pos = (i * tm + lax.broadcasted_iota(jnp.int32, (tm, 1), 0)) & (seq - 1)

    def up_conv(w_ref, c_ref):
        up = _dot(h_scr[...], w_ref[...])
        prev = _shift_rows(up, down=True)[HALO:HALO + tm]
        nxt = _shift_rows(up, down=False)[HALO:HALO + tm]
        if ends_inside:
            prev = jnp.where(pos == 0, 0.0, prev)
            nxt = jnp.where(pos == seq - 1, 0.0, nxt)
        c = c_ref[...]
        return prev * c[0:1] + up[HALO:HALO + tm] * c[1:2] + nxt * c[2:3]

    act = _silu(up_conv(wg_ref, cg_ref)) * up_conv(wu_ref, cu_ref)
    o_ref[...] += _dot(act.astype(BF16), wd_ref[...])

    @pl.when(j == pl.num_programs(1) - 1)
    def _():
        y = DEEPNORM_ALPHA * x_ref[...] + g_ref[...] * o_ref[...]
        o_ref[...] = _layer_norm(y, lw_ref[...], lb_ref[...])


def _ffn(x, seq, mod, w_up, ffn_conv, w_down, layer, ln_w, ln_b):
    rows = x.shape[0]
    tm, tf = FFN_TM, FFN_TF
    n_f = D_FF // tf
    tiles_per_mod = rows // mod.shape[0] // tm
    halo_per_tile = tm // HALO
    last_halo = rows // HALO - 1

    def mod_spec(chunk):
        return pl.BlockSpec((None, 1, D_MODEL), lambda i, j: (i // tiles_per_mod, 0, chunk))

    vec = pl.BlockSpec((1, D_MODEL), lambda i, j: (0, 0))
    return pl.pallas_call(
        functools.partial(_ffn_kernel, seq=seq, tm=tm),
        grid=(rows // tm, n_f),
        in_specs=[
            pl.BlockSpec((tm, D_MODEL), lambda i, j: (i, 0), pipeline_mode=pl.Buffered(1)),
            pl.BlockSpec((HALO, D_MODEL), lambda i, j: (jnp.maximum(i * halo_per_tile - 1, 0), 0)),
            pl.BlockSpec((HALO, D_MODEL), lambda i, j: (jnp.minimum((i + 1) * halo_per_tile, last_halo), 0)),
            mod_spec(3), mod_spec(4), mod_spec(5),
            pl.BlockSpec((D_MODEL, tf), lambda i, j: (0, j)),
            pl.BlockSpec((D_MODEL, tf), lambda i, j: (0, n_f + j)),
            pl.BlockSpec((None, 3, tf), lambda i, j: (layer, 0, j)),
            pl.BlockSpec((None, 3, tf), lambda i, j: (layer, 0, n_f + j)),
            pl.BlockSpec((tf, D_MODEL), lambda i, j: (j, 0)),
            vec, vec,
        ],
        out_specs=pl.BlockSpec((tm, D_MODEL), lambda i, j: (i, 0)),
        out_shape=jax.ShapeDtypeStruct((rows, D_MODEL), F32),
        scratch_shapes=[pltpu.VMEM((tm + 2 * HALO, D_MODEL), BF16)],
        compiler_params=_params(2),
        name="ffn_ln",
    )(x, x, x, mod, mod, mod, w_up, w_up, ffn_conv, ffn_conv, w_down,
      ln_w.reshape(1, D_MODEL), ln_b.reshape(1, D_MODEL))


def _rope_tables(n_tokens):
    t = np.arange(n_tokens)
    n_freq = HEAD_DIM // 4
    inv_freq = ROPE_THETA ** (-np.arange(n_freq) / n_freq)
    ang = np.concatenate([(t // GRID_W)[:, None] * inv_freq[None], (t % GRID_W)[:, None] * inv_freq[None]], -1)
    cos, sin = np.cos(ang), np.sin(ang)
    return (jnp.asarray(np.concatenate([cos, cos], -1), F32), jnp.asarray(np.concatenate([-sin, sin], -1), F32))


def _dft_tables(seq):
    idx = np.arange(seq)
    ang = ((idx[:, None] * idx[None, :]) % (2 * seq)) * (math.pi / seq)
    return (*_split(jnp.asarray(np.cos(ang), F32)), *_split(jnp.asarray(np.sin(ang), F32)))


def _filter_features(seq):
    pos = np.arange(seq)
    t = pos / max(seq - 1, 1)
    ang = (2.0 * math.pi * pos / seq)[:, None] * np.linspace(1e-4, FILTER_BANDS - 1, FILTER_BANDS)[None, :]
    z = np.concatenate([t[:, None], np.cos(ang), -np.sin(ang)], -1)
    return jnp.asarray(np.pad(z, ((0, 0), (0, FILTER_PAD - FILTER_EMB))), F32)


def _pad_to(a, shape):
    return jnp.pad(a, [(0, s - d) for d, s in zip(a.shape, shape)])


def kernel(x_prompt, x_sample, cache_k, cache_v, state_ret, c, c_ctx, w_ada, b_ada, w_in, q_norm, k_norm, ret_decay, hy_conv, hf_w1, hf_b1, hf_freq, hf_w2, hf_b2, hf_w3, hy_bias, w_out, ln1_w, ln1_b, w_up, ffn_conv, w_down, ln2_w, ln2_b):
    n_ctx, seq_ctx, _ = x_prompt.shape
    n_lat, seq_lat, _ = x_sample.shape

    cond = _pad_to(jnp.concatenate([c_ctx[None, :], c], 0), (COND_ROWS, D_MODEL))
    mod_all = _ada(cond, w_ada, b_ada)

    w_in_b = w_in[0].astype(BF16)
    log_gamma = jax.nn.log_sigmoid(ret_decay.astype(F32))
    abs_deltas = jnp.asarray(np.abs(np.linspace(math.log(HYENA_DECAY_TARGET) / HYENA_DECAY_PCT_MIN,
                                                math.log(HYENA_DECAY_TARGET) / HYENA_DECAY_PCT_MAX,
                                                HYENA_WIDTH))[None, :], F32)
    ret_zero = jnp.zeros((n_ctx, 1, 2, N_RET_HEADS, HEAD_DIM, HEAD_DIM), F32)

    groups = {
        "ctx": dict(n=n_ctx, seq=seq_ctx, rope=None, dft=_dft_tables(seq_ctx), feat=_filter_features(seq_ctx)),
        "lat": dict(n=n_lat, seq=seq_lat, rope=_rope_tables(seq_lat), dft=_dft_tables(seq_lat),
                    feat=_filter_features(seq_lat)),
    }
    xs = {"ctx": x_prompt.reshape(n_ctx * seq_ctx, D_MODEL), "lat": x_sample.reshape(n_lat * seq_lat, D_MODEL)}
    ks_out, vs_out, ss_out = [], [], []

    for l in range(DEPTH):
        mods = {"ctx": mod_all[l, 0:1, None, :], "lat": mod_all[l, 1:1 + n_lat, None, :]}
        pad2 = (FILTER_PAD, FILTER_PAD)
        filt_w = (_pad_to(hf_w1[l], pad2), _pad_to(hf_b1[l][None, :], (1, FILTER_PAD)),
                  _pad_to(hf_freq[l][None, :], (1, FILTER_PAD)), _pad_to(hf_w2[l], pad2),
                  _pad_to(hf_b2[l][None, :], (1, FILTER_PAD)), _pad_to(hf_w3[l], (FILTER_PAD, hf_w3.shape[-1])))
        next_w_in = [(w_in, l + 1)] if l + 1 < DEPTH else []
        proj_ctx, (w_down_b, w_out_b) = _inproj(xs["ctx"], mods["ctx"], w_in_b, [(w_down, l), (w_out, l)])
        proj_lat, (w_up_b, *w_in_next) = _inproj(xs["lat"], mods["lat"], w_in_b, [(w_up, l)] + next_w_in)
        if w_in_next:
            w_in_b = w_in_next[0]
        projs = {"ctx": proj_ctx, "lat": proj_lat}
        for name, g in groups.items():
            n, seq, x, mod, proj = g["n"], g["seq"], xs[name], mods[name], projs[name]
            is_ctx = name == "ctx"
            att_out = _attention(proj, n, seq, q_norm[l], k_norm[l], g["rope"],
                                 None if is_ctx else (cache_k, cache_v), l, emit_cache=is_ctx)
            ret, s_fin = _retention(proj, n, seq, log_gamma[l],
                                    ret_zero if is_ctx else state_ret, 0 if is_ctx else l)
            spectra = _hyena_filters(seq, g["feat"], *filt_w, abs_deltas, g["dft"])
            hyz = _hyena(proj, n, seq, hy_conv[l], hy_bias[l], spectra, g["dft"])
            if is_ctx:
                att, own_k, own_v = att_out
                ks_out.append(own_k)
                vs_out.append(own_v)
                ss_out.append(s_fin)
            else:
                att = att_out[0]
            x = _outproj(att, ret, hyz, x, mod, w_out_b, ln1_w[l], ln1_b[l])
            xs[name] = _ffn(x, seq, mod, w_up_b, ffn_conv, w_down_b, l, ln2_w[l], ln2_b[l])

    return (xs["ctx"].reshape(n_ctx, seq_ctx, D_MODEL), xs["lat"].reshape(n_lat, seq_lat, D_MODEL),
            jnp.stack(ks_out, axis=1), jnp.stack(vs_out, axis=1), jnp.stack(ss_out, axis=1))
```

```python
import functools
import math

import numpy as np
import jax
import jax.numpy as jnp
from jax import lax
from jax.experimental import pallas as pl
from jax.experimental.pallas import tpu as pltpu

F32 = jnp.float32
BF16 = jnp.bfloat16

D_MODEL = 2048
DEPTH = 2
GRID_W = 64
HEAD_DIM = 128
ATT_WIDTH = D_MODEL // 2
N_ATT_HEADS = ATT_WIDTH // HEAD_DIM
N_KV_HEADS = N_ATT_HEADS // 4
GQA_GROUP = N_ATT_HEADS // N_KV_HEADS
KV_WIDTH = N_KV_HEADS * HEAD_DIM
RET_WIDTH = D_MODEL // 4
N_RET_HEADS = RET_WIDTH // HEAD_DIM
HYENA_WIDTH = D_MODEL // 4
HYENA_ORDER = 2
IN_WIDTH = ATT_WIDTH + 2 * KV_WIDTH + 4 * RET_WIDTH + (HYENA_ORDER + 1) * HYENA_WIDTH
D_FF = 5632
ROPE_THETA = 10000.0
FILTER_BANDS = 16
FILTER_EMB = 1 + 2 * FILTER_BANDS
FILTER_HID = 64
HYENA_DECAY_TARGET = 1e-2
HYENA_DECAY_PCT_MIN = 0.3
HYENA_DECAY_PCT_MAX = 1.5
DEEPNORM_ALPHA = (2 * DEPTH) ** 0.25
EPS = 1e-6

COL_Q = 0
COL_K = ATT_WIDTH
COL_V = COL_K + KV_WIDTH
COL_RQ = COL_V + KV_WIDTH
COL_RK = COL_RQ + RET_WIDTH
COL_RV = COL_RK + RET_WIDTH
COL_RG = COL_RV + RET_WIDTH
COL_HY = COL_RG + RET_WIDTH

LANES = 128
HALO = 16
VMEM_LIMIT = 60 * 1024 * 1024
COND_ROWS = 16

ADA_TN = 1024
INPROJ_TM, INPROJ_TN = 256, 512
OUTPROJ_TM, OUTPROJ_SUB = 512, 256
FFN_TM, FFN_TF = 1024, 512
ATT_TQ, ATT_CHAIN = 1024, 256
ATT_ROWS = 1024
ATT_MXU_SUM_MIN_KEYS = 1024
RET_ROWS = 2048
HYENA_ROWS = 1024
FILTER_PAD = 128

NT_DIMS = (((1,), (1,)), ((), ()))
TN_DIMS = (((0,), (0,)), ((), ()))


def _params(n_grid):
    return pltpu.CompilerParams(dimension_semantics=("arbitrary",) * n_grid,
                                vmem_limit_bytes=VMEM_LIMIT)


def _dot(a, b):
    return jnp.dot(a, b, preferred_element_type=F32)


def _split(a):
    hi = a.astype(BF16)
    lo = (a - hi.astype(F32)).astype(BF16)
    return hi, lo


def _dot3(a_hi, a_lo, b_hi, b_lo):
    return _dot(a_hi, b_hi) + _dot(a_lo, b_hi) + _dot(a_hi, b_lo)


def _silu(x):
    return x * jax.nn.sigmoid(x)


def _layer_norm(y, w, b):
    yc = y - jnp.mean(y, -1, keepdims=True)
    var = jnp.mean(yc * yc, -1, keepdims=True)
    return yc * lax.rsqrt(var + EPS) * w + b


def _shift_rows(x, down):
    n = x.shape[0]
    return pltpu.roll(x, 1 if down else n - 1, axis=0)


def _ada_kernel(cond_ref, w_ref, b_ref, o_ref):
    a = _silu(cond_ref[...]).astype(BF16)
    o_ref[...] = _dot(a, w_ref[...].astype(BF16)) + b_ref[...]


def _ada(cond, w_ada, b_ada):
    n_out = w_ada.shape[-1]
    return pl.pallas_call(
        _ada_kernel,
        grid=(DEPTH, n_out // ADA_TN),
        in_specs=[
            pl.BlockSpec((COND_ROWS, D_MODEL), lambda l, j: (0, 0)),
            pl.BlockSpec((None, D_MODEL, ADA_TN), lambda l, j: (l, 0, j)),
            pl.BlockSpec((None, 1, ADA_TN), lambda l, j: (l, 0, j)),
        ],
        out_specs=pl.BlockSpec((None, COND_ROWS, ADA_TN), lambda l, j: (l, 0, j)),
        out_shape=jax.ShapeDtypeStruct((DEPTH, COND_ROWS, n_out), F32),
        compiler_params=_params(2),
        name="ada",
    )(cond, w_ada, b_ada.reshape(DEPTH, 1, n_out))


def _inproj_kernel(x_ref, sh_ref, sc_ref, w_ref, *refs):
    n_casts = len(refs) // 2
    src_refs, o_ref, cast_refs = refs[:n_casts], refs[n_casts], refs[n_casts + 1:]
    h = (x_ref[...] * (1.0 + sc_ref[...]) + sh_ref[...]).astype(BF16)
    for n0 in range(0, IN_WIDTH, INPROJ_TN):
        o_ref[:, n0:n0 + INPROJ_TN] = _dot(h, w_ref[:, n0:n0 + INPROJ_TN])
    for src_ref, cast_ref in zip(src_refs, cast_refs):
        cast_ref[...] = src_ref[...].astype(BF16)


def _inproj(x, mod, w_in, casts):
    rows = x.shape[0]
    tm = INPROJ_TM
    n_steps = rows // tm
    tiles_per_mod = n_steps // mod.shape[0]
    cast_in, cast_out, cast_shape = [], [], []
    for src, layer in casts:
        _, k, n = src.shape
        band = k // n_steps
        assert band * n_steps == k and band % HALO == 0, (k, n_steps)
        cast_in.append(pl.BlockSpec((None, band, n), lambda i, layer=layer: (layer, i, 0)))
        cast_out.append(pl.BlockSpec((band, n), lambda i: (i, 0)))
        cast_shape.append(jax.ShapeDtypeStruct((k, n), BF16))
    proj, *cast = pl.pallas_call(
        _inproj_kernel,
        grid=(n_steps,),
        in_specs=[
            pl.BlockSpec((tm, D_MODEL), lambda i: (i, 0)),
            pl.BlockSpec((None, 1, D_MODEL), lambda i: (i // tiles_per_mod, 0, 0)),
            pl.BlockSpec((None, 1, D_MODEL), lambda i: (i // tiles_per_mod, 0, 1)),
            pl.BlockSpec((D_MODEL, IN_WIDTH), lambda i: (0, 0), pipeline_mode=pl.Buffered(1)),
            *cast_in,
        ],
        out_specs=[pl.BlockSpec((tm, IN_WIDTH), lambda i: (i, 0)), *cast_out],
        out_shape=[jax.ShapeDtypeStruct((rows, IN_WIDTH), F32), *cast_shape],
        compiler_params=_params(1),
        name="inproj",
    )(x, mod, mod, w_in, *[src for src, _ in casts])
    return proj, cast


def _rms_norm(x, w):
    return x * lax.rsqrt(jnp.mean(x * x, -1, keepdims=True) + EPS) * w


def _rope(x, cos_full, sin_signed):
    return x * cos_full + pltpu.roll(x, HEAD_DIM // 2, axis=1) * sin_signed


def _attn_kernel(*refs, seq, n_seq, rope, has_ctx, emit_cache):
    refs = list(refs)
    q_ref, k_ref, v_ref, qw_ref, kw_ref = refs[:5]
    del refs[:5]
    if rope:
        qcos_ref, qsin_ref, kcos_ref, ksin_ref = refs[:4]
        del refs[:4]
    if has_ctx:
        ck_ref, cv_ref = refs[:2]
        del refs[:2]
    o_ref = refs.pop(0)
    if emit_cache:
        ownk_ref, ownv_ref = refs[:2]
        del refs[:2]
    k_scr, v_scr = refs
    sums_on_mxu = v_scr.shape[1] > HEAD_DIM
    own = n_seq * seq

    @pl.when(pl.program_id(2) == 0)
    def _():
        kn = _rms_norm(k_ref[...], kw_ref[...])
        if emit_cache:
            for s in range(n_seq):
                ownk_ref[s] = kn[s * seq:(s + 1) * seq]
                ownv_ref[s] = v_ref[s * seq:(s + 1) * seq, :]
        if rope:
            kn = _rope(kn, kcos_ref[...], ksin_ref[...])
        k_scr[0:own, :] = kn.astype(BF16)
        v_scr[0:own, 0:HEAD_DIM] = v_ref[...].astype(BF16)
        if has_ctx:
            k_scr[own:, :] = ck_ref[...].astype(BF16)
            v_scr[own:, 0:HEAD_DIM] = cv_ref[...].astype(BF16)
        if sums_on_mxu:
            v_scr[:, HEAD_DIM:] = jnp.ones((v_scr.shape[0], HEAD_DIM), BF16)

    tq = q_ref.shape[0] // n_seq
    for s in range(n_seq):
        keys = slice(s * seq, (s + 1) * seq) if n_seq > 1 else slice(None)
        for g in range(GQA_GROUP):
            cols = slice(g * HEAD_DIM, (g + 1) * HEAD_DIM)
            for r0 in range(s * tq, (s + 1) * tq, ATT_CHAIN):
                rows = slice(r0, r0 + ATT_CHAIN)
                qn = _rms_norm(q_ref[rows, cols], qw_ref[...])
                if rope:
                    qn = _rope(qn, qcos_ref[rows, :], qsin_ref[rows, :])
                qb = (qn * (HEAD_DIM ** -0.5)).astype(BF16)
                sc = lax.dot_general(qb, k_scr[keys, :], NT_DIMS, preferred_element_type=F32)
                p = jnp.exp(sc - jnp.max(sc, -1, keepdims=True))
                o = _dot(p.astype(BF16), v_scr[keys, :])
                den = o[:, HEAD_DIM:HEAD_DIM + 1] if sums_on_mxu else jnp.sum(p, -1, keepdims=True)
                o_ref[rows, cols] = (o[:, :HEAD_DIM] / den).astype(o_ref.dtype)


def _attention(proj, n_batch, seq, q_norm, k_norm, rope_tabs, ctx_kv, layer, emit_cache):
    rows = proj.shape[0]
    tq = min(ATT_TQ, seq)
    nq = seq // tq
    group_w = GQA_GROUP * HEAD_DIM
    rope = rope_tabs is not None
    has_ctx = ctx_kv is not None
    n_seq = max(1, ATT_ROWS // seq) if not (rope or has_ctx) else 1
    keys_per_chain = seq + (ctx_kv[0].shape[3] if has_ctx else 0)
    value_width = 2 * HEAD_DIM if keys_per_chain >= ATT_MXU_SUM_MIN_KEYS else HEAD_DIM

    in_specs = [
        pl.BlockSpec((n_seq * tq, group_w), lambda b, kv, qi: (b * nq + qi, COL_Q // group_w + kv)),
        pl.BlockSpec((n_seq * seq, HEAD_DIM), lambda b, kv, qi: (b, COL_K // HEAD_DIM + kv)),
        pl.BlockSpec((n_seq * seq, HEAD_DIM), lambda b, kv, qi: (b, COL_V // HEAD_DIM + kv)),
        pl.BlockSpec((1, HEAD_DIM), lambda b, kv, qi: (0, 0)),
        pl.BlockSpec((1, HEAD_DIM), lambda b, kv, qi: (0, 0)),
    ]
    args = [proj, proj, proj, q_norm.reshape(1, HEAD_DIM), k_norm.reshape(1, HEAD_DIM)]
    if rope:
        cos_full, sin_signed = rope_tabs
        in_specs += [
            pl.BlockSpec((tq, HEAD_DIM), lambda b, kv, qi: (qi, 0)),
            pl.BlockSpec((tq, HEAD_DIM), lambda b, kv, qi: (qi, 0)),
            pl.BlockSpec((seq, HEAD_DIM), lambda b, kv, qi: (0, 0)),
            pl.BlockSpec((seq, HEAD_DIM), lambda b, kv, qi: (0, 0)),
        ]
        args += [cos_full, sin_signed, cos_full, sin_signed]
    if has_ctx:
        ck, cv = ctx_kv
        past = ck.shape[3]
        ctx_spec = pl.BlockSpec((None, None, None, past, HEAD_DIM), lambda b, kv, qi: (b, layer, kv, 0, 0))
        in_specs += [ctx_spec, ctx_spec]
        args += [ck, cv]

    out_specs = [pl.BlockSpec((n_seq * tq, group_w), lambda b, kv, qi: (b * nq + qi, kv))]
    out_shape = [jax.ShapeDtypeStruct((rows, ATT_WIDTH), BF16)]
    if emit_cache:
        own_spec = pl.BlockSpec((n_seq, None, seq, HEAD_DIM), lambda b, kv, qi: (b, kv, 0, 0))
        out_specs += [own_spec, own_spec]
        out_shape += [jax.ShapeDtypeStruct((n_batch, N_KV_HEADS, seq, HEAD_DIM), F32)] * 2

    return pl.pallas_call(
        functools.partial(_attn_kernel, seq=seq, n_seq=n_seq, rope=rope, has_ctx=has_ctx, emit_cache=emit_cache),
        grid=(n_batch // n_seq, N_KV_HEADS, nq),
        in_specs=in_specs,
        out_specs=out_specs,
        out_shape=out_shape,
        scratch_shapes=[pltpu.VMEM((n_seq * seq + (past if has_ctx else 0), width), BF16)
                        for width in (HEAD_DIM, value_width)],
        compiler_params=_params(3),
        name="attention",
    )(*args)


def _retention_kernel(lg_ref, q_ref, k_ref, v_ref, g_ref, *refs, seq, n_seq):
    s0_ref = refs[0] if len(refs) == 4 else None
    o_ref, sf_ref, decay_scr = refs[-3:]
    h = pl.program_id(0)
    lg_f = lg_ref[h]
    lg_b = lg_ref[N_RET_HEADS + h]

    @pl.when(pl.program_id(1) == 0)
    def _():
        i = lax.broadcasted_iota(jnp.int32, (seq, seq), 0)
        j = lax.broadcasted_iota(jnp.int32, (seq, seq), 1)
        d = (i - j).astype(F32)
        decay = jnp.exp(jnp.where(d >= 0, lg_f, lg_b) * jnp.abs(d))
        decay_scr[...] = jnp.where(d == 0, 2.0, decay)

    pos = lax.broadcasted_iota(jnp.int32, (seq, 1), 0).astype(F32)
    q_decay_f, q_decay_b = jnp.exp(lg_f * (pos + 1.0)), jnp.exp(lg_b * (seq - pos))
    k_decay_f, k_decay_b = jnp.exp(lg_f * (seq - 1.0 - pos)), jnp.exp(lg_b * pos)
    ones = jnp.ones((1, HEAD_DIM), F32)
    s_decay_f, s_decay_b = jnp.exp(ones * (lg_f * seq)), jnp.exp(ones * (lg_b * seq))

    for s in range(n_seq):
        rows = slice(s * seq, (s + 1) * seq)
        q = q_ref[rows, :]
        k = k_ref[rows, :] * (HEAD_DIM ** -0.5)
        vb = v_ref[rows, :].astype(BF16)
        a = lax.dot_general(q.astype(BF16), k.astype(BF16), NT_DIMS, preferred_element_type=F32)
        o = _dot((a * decay_scr[...]).astype(BF16), vb)
        kf = (k * k_decay_f).astype(BF16)
        kb = (k * k_decay_b).astype(BF16)
        s_f = lax.dot_general(kf, vb, TN_DIMS, preferred_element_type=F32)
        s_b = lax.dot_general(kb, vb, TN_DIMS, preferred_element_type=F32)
        if s0_ref is not None:
            s0_f = s0_ref[s, 0]
            s0_b = s0_ref[s, 1]
            o = o + _dot((q * q_decay_f).astype(BF16), s0_f.astype(BF16))
            o = o + _dot((q * q_decay_b).astype(BF16), s0_b.astype(BF16))
            s_f = s_decay_f * s0_f + s_f
            s_b = s_decay_b * s0_b + s_b
        sf_ref[s, 0] = s_f
        sf_ref[s, 1] = s_b

        oc = o - jnp.mean(o, -1, keepdims=True)
        var = jnp.mean(oc * oc, -1, keepdims=True)
        o_ref[rows, :] = (_silu(g_ref[rows, :]) * (oc * lax.rsqrt(var + EPS))).astype(o_ref.dtype)


def _retention(proj, n_batch, seq, log_gamma, s0, layer):
    rows = proj.shape[0]
    n_seq = max(1, RET_ROWS // seq)
    hd = HEAD_DIM

    def col(offset):
        return pl.BlockSpec((n_seq * seq, hd), lambda h, b: (b, offset // hd + h))

    return pl.pallas_call(
        functools.partial(_retention_kernel, seq=seq, n_seq=n_seq),
        grid=(N_RET_HEADS, n_batch // n_seq),
        in_specs=[
            pl.BlockSpec(memory_space=pltpu.SMEM),
            col(COL_RQ), col(COL_RK), col(COL_RV), col(COL_RG),
            *([pl.BlockSpec((n_seq, None, 2, None, hd, hd), lambda h, b: (b, layer, 0, h, 0, 0))]
              if s0 is not None else []),
        ],
        out_specs=[
            pl.BlockSpec((n_seq * seq, hd), lambda h, b: (b, h)),
            pl.BlockSpec((n_seq, 2, None, hd, hd), lambda h, b: (b, 0, h, 0, 0)),
        ],
        out_shape=[
            jax.ShapeDtypeStruct((rows, RET_WIDTH), BF16),
            jax.ShapeDtypeStruct((n_batch, 2, N_RET_HEADS, hd, hd), F32),
        ],
        scratch_shapes=[pltpu.VMEM((seq, seq), F32)],
        compiler_params=_params(2),
        name="retention",
    )(log_gamma.reshape(2 * N_RET_HEADS), proj, proj, proj, proj, *([s0] if s0 is not None else []))


def _alt_sign(n):
    r = lax.broadcasted_iota(jnp.int32, (n, 1), 0)
    return jnp.where((r & 1) == 0, 1.0, -1.0).astype(F32)


def _filter_kernel(z_ref, w1_ref, b1_ref, fr_ref, w2_ref, b2_ref, w3f_ref, w3b_ref, dl_ref,
                   ch_ref, cl_ref, sh_ref, sl_ref, kr_ref, ki_ref, kn_ref, *, seq):
    freq = fr_ref[...]
    hdn = jnp.sin(freq * (_dot3(*_split(z_ref[...]), *_split(w1_ref[...])) + b1_ref[...]))
    hdn = jnp.sin(freq * (_dot3(*_split(hdn), *_split(w2_ref[...])) + b2_ref[...]))
    hdn_hi, hdn_lo = _split(hdn)

    pos = lax.broadcasted_iota(jnp.int32, (seq, 1), 0)
    t = pos.astype(F32) / max(seq - 1, 1)
    decay = jnp.exp(-t * dl_ref[...])
    h_fwd = _dot3(hdn_hi, hdn_lo, *_split(w3f_ref[...])) * decay
    h_bwd = _dot3(hdn_hi, hdn_lo, *_split(w3b_ref[...])) * decay

    h_bwd_shift = jnp.where(pos == 0, 0.0, _shift_rows(h_bwd, down=True))
    alt = _alt_sign(seq)
    kr = _dot3(ch_ref[...], cl_ref[...], *_split(h_fwd + h_bwd_shift)) + alt * h_bwd[seq - 1:seq, :]
    ki = _dot3(sh_ref[...], sl_ref[...], *_split(h_bwd_shift - h_fwd))
    wgt = jnp.where(pos == 0, 1.0, 2.0) * (0.5 / seq)
    kr_ref[...] = kr * wgt
    ki_ref[...] = ki * wgt
    kn_ref[...] = jnp.sum(alt * (h_fwd - h_bwd), axis=0, keepdims=True) * (0.5 / seq)


def _hyena_filters(seq, zfeat, w1, b1, freq, w2, b2, w3, abs_deltas, dft):
    w = HYENA_WIDTH
    const = lambda o: (0, 0)
    once = pl.Buffered(1)
    mat = pl.BlockSpec((seq, seq), const, pipeline_mode=once)
    vec = pl.BlockSpec((1, FILTER_PAD), const)
    return pl.pallas_call(
        functools.partial(_filter_kernel, seq=seq),
        grid=(HYENA_ORDER,),
        in_specs=[
            pl.BlockSpec((seq, FILTER_PAD), const),
            pl.BlockSpec((FILTER_PAD, FILTER_PAD), const), vec, vec,
            pl.BlockSpec((FILTER_PAD, FILTER_PAD), const), vec,
            pl.BlockSpec((FILTER_PAD, w), lambda o: (0, 2 * o)),
            pl.BlockSpec((FILTER_PAD, w), lambda o: (0, 2 * o + 1)),
            pl.BlockSpec((1, w), const),
            mat, mat, mat, mat,
        ],
        out_specs=[
            pl.BlockSpec((None, seq, w), lambda o: (o, 0, 0)),
            pl.BlockSpec((None, seq, w), lambda o: (o, 0, 0)),
            pl.BlockSpec((None, 1, w), lambda o: (o, 0, 0)),
        ],
        out_shape=[
            jax.ShapeDtypeStruct((HYENA_ORDER, seq, HYENA_WIDTH), F32),
            jax.ShapeDtypeStruct((HYENA_ORDER, seq, HYENA_WIDTH), F32),
            jax.ShapeDtypeStruct((HYENA_ORDER, 1, HYENA_WIDTH), F32),
        ],
        compiler_params=_params(1),
        name="hyena_filters",
    )(zfeat, w1, b1, freq, w2, b2, w3, w3, abs_deltas, *dft)


def _dwconv3_seq(x, w, pos, seq):
    prev = jnp.where(pos == 0, 0.0, _shift_rows(x, down=True))
    nxt = jnp.where(pos == seq - 1, 0.0, _shift_rows(x, down=False))
    return prev * w[0:1] + x * w[1:2] + nxt * w[2:3]


def _hyena_kernel(v_ref, x1_ref, x2_ref, cv_ref, c1_ref, c2_ref, bias_ref, kr_ref, ki_ref, kn_ref,
                  cos_ref, sin_ref, o_ref, *, seq, n_seq):
    pos = lax.broadcasted_iota(jnp.int32, (seq, 1), 0)
    alt = _alt_sign(seq)

    def dft(m_ref, a):
        return _dot(m_ref[...], a.astype(BF16))

    for s in range(n_seq):
        rows = slice(s * seq, (s + 1) * seq)
        z = _dwconv3_seq(v_ref[rows, :], cv_ref[...], pos, seq)
        for o, (x_ref, c_ref) in enumerate(((x1_ref, c1_ref), (x2_ref, c2_ref))):
            ur = dft(cos_ref, z)
            ui = dft(sin_ref, z)
            u_nyq = jnp.sum(alt * z, axis=0, keepdims=True)
            kr, ki = kr_ref[o], ki_ref[o]
            yr = ur * kr + ui * ki
            yi = ui * kr - ur * ki
            y = dft(cos_ref, yr) + dft(sin_ref, yi) + alt * (u_nyq * kn_ref[o])
            z = _dwconv3_seq(x_ref[rows, :], c_ref[...], pos, seq) * (y + z * bias_ref[o:o + 1, :])
        o_ref[rows, :] = z.astype(o_ref.dtype)


def _hyena(proj, n_batch, seq, hy_conv, hy_bias, spectra, dft):
    rows = proj.shape[0]
    w = HYENA_WIDTH
    kr, ki, kn = spectra
    once = pl.Buffered(1)
    n_seq = max(1, HYENA_ROWS // seq)

    def part(p):
        return pl.BlockSpec((n_seq * seq, w), lambda b: (b, COL_HY // w + p))

    def conv(p):
        return pl.BlockSpec((3, w), lambda b: (0, p), pipeline_mode=once)

    mat = pl.BlockSpec((seq, seq), lambda b: (0, 0), pipeline_mode=once)
    spec = pl.BlockSpec((HYENA_ORDER, seq, w), lambda b: (0, 0, 0), pipeline_mode=once)
    return pl.pallas_call(
        functools.partial(_hyena_kernel, seq=seq, n_seq=n_seq),
        grid=(n_batch // n_seq,),
        in_specs=[
            part(0), part(1), part(2), conv(0), conv(1), conv(2),
            pl.BlockSpec((HYENA_ORDER, w), lambda b: (0, 0), pipeline_mode=once),
            spec, spec,
            pl.BlockSpec((HYENA_ORDER, 1, w), lambda b: (0, 0, 0), pipeline_mode=once),
            mat, mat,
        ],
        out_specs=pl.BlockSpec((n_seq * seq, w), lambda b: (b, 0)),
        out_shape=jax.ShapeDtypeStruct((rows, w), BF16),
        compiler_params=_params(1),
        name="hyena",
    )(proj, proj, proj, hy_conv, hy_conv, hy_conv, hy_bias, kr, ki, kn, dft[0], dft[2])


def _outproj_kernel(att_ref, ret_ref, hy_ref, x_ref, g_ref, w_ref, lw_ref, lb_ref, o_ref):
    r0, r1 = ATT_WIDTH, ATT_WIDTH + RET_WIDTH
    for t0 in range(0, x_ref.shape[0], OUTPROJ_SUB):
        rows = slice(t0, t0 + OUTPROJ_SUB)
        mix = (_dot(att_ref[rows, :], w_ref[0:r0, :]) + _dot(ret_ref[rows, :], w_ref[r0:r1, :])
               + _dot(hy_ref[rows, :], w_ref[r1:, :]))
        y = DEEPNORM_ALPHA * x_ref[rows, :] + g_ref[...] * mix
        o_ref[rows, :] = _layer_norm(y, lw_ref[...], lb_ref[...])


def _outproj(att, ret, hyz, x, mod, w_out, ln_w, ln_b):
    rows = x.shape[0]
    tm = OUTPROJ_TM
    tiles_per_mod = rows // mod.shape[0] // tm
    row = lambda width: pl.BlockSpec((tm, width), lambda i: (i, 0))
    vec = pl.BlockSpec((1, D_MODEL), lambda i: (0, 0))
    return pl.pallas_call(
        _outproj_kernel,
        grid=(rows // tm,),
        in_specs=[
            row(ATT_WIDTH), row(RET_WIDTH), row(HYENA_WIDTH), row(D_MODEL),
            pl.BlockSpec((None, 1, D_MODEL), lambda i: (i // tiles_per_mod, 0, 2)),
            pl.BlockSpec((D_MODEL, D_MODEL), lambda i: (0, 0), pipeline_mode=pl.Buffered(1)),
            vec, vec,
        ],
        out_specs=row(D_MODEL),
        out_shape=jax.ShapeDtypeStruct((rows, D_MODEL), F32),
        compiler_params=_params(1),
        name="outproj_ln",
    )(att, ret, hyz, x, mod, w_out, ln_w.reshape(1, D_MODEL), ln_b.reshape(1, D_MODEL))


def _ffn_kernel(x_ref, xp_ref, xn_ref, sh_ref, sc_ref, g_ref, wg_ref, wu_ref, cg_ref, cu_ref, wd_ref,
                lw_ref, lb_ref, o_ref, h_scr, *, seq, tm):
    i = pl.program_id(0)
    j = pl.program_id(1)
    ends_inside = seq < tm

    @pl.when(j == 0)
    def _():
        sc = 1.0 + sc_ref[...]
        sh = sh_ref[...]
        h_prev = xp_ref[...] * sc + sh
        h_next = xn_ref[...] * sc + sh
        if not ends_inside:
            h_prev = jnp.where(((i * tm) & (seq - 1)) == 0, 0.0, h_prev)
            h_next = jnp.where((((i + 1) * tm) & (seq - 1)) == 0, 0.0, h_next)
        h_scr[0:HALO, :] = h_prev.astype(BF16)
        h_scr[HALO:HALO + tm, :] = (x_ref[...] * sc + sh).astype(BF16)
        h_scr[HALO + tm:, :] = h_next.astype(BF16)
        o_ref[...] = jnp.zeros_like(o_ref)

    pos = (i * tm + lax.broadcasted_iota(jnp.int32, (tm, 1), 0)) & (seq - 1)

    def up_conv(w_ref, c_ref):
        up = _dot(h_scr[...], w_ref[...])
        prev = _shift_rows(up, down=True)[HALO:HALO + tm]
        nxt = _shift_rows(up, down=False)[HALO:HALO + tm]
        if ends_inside:
            prev = jnp.where(pos == 0, 0.0, prev)
            nxt = jnp.where(pos == seq - 1, 0.0, nxt)
        c = c_ref[...]
        return prev * c[0:1] + up[HALO:HALO + tm] * c[1:2] + nxt * c[2:3]

    act = _silu(up_conv(wg_ref, cg_ref)) * up_conv(wu_ref, cu_ref)
    o_ref[...] += _dot(act.astype(BF16), wd_ref[...])

    @pl.when(j == pl.num_programs(1) - 1)
    def _():
        y = DEEPNORM_ALPHA * x_ref[...] + g_ref[...] * o_ref[...]
        o_ref[...] = _layer_norm(y, lw_ref[...], lb_ref[...])


def _ffn(x, seq, mod, w_up, ffn_conv, w_down, layer, ln_w, ln_b):
    rows = x.shape[0]
    tm, tf = FFN_TM, FFN_TF
    n_f = D_FF // tf
    tiles_per_mod = rows // mod.shape[0] // tm
    halo_per_tile = tm // HALO
    last_halo = rows // HALO - 1

    def mod_spec(chunk):
        return pl.BlockSpec((None, 1, D_MODEL), lambda i, j: (i // tiles_per_mod, 0, chunk))

    vec = pl.BlockSpec((1, D_MODEL), lambda i, j: (0, 0))
    return pl.pallas_call(
        functools.partial(_ffn_kernel, seq=seq, tm=tm),
        grid=(rows // tm, n_f),
        in_specs=[
            pl.BlockSpec((tm, D_MODEL), lambda i, j: (i, 0), pipeline_mode=pl.Buffered(1)),
            pl.BlockSpec((HALO, D_MODEL), lambda i, j: (jnp.maximum(i * halo_per_tile - 1, 0), 0)),
            pl.BlockSpec((HALO, D_MODEL), lambda i, j: (jnp.minimum((i + 1) * halo_per_tile, last_halo), 0)),
            mod_spec(3), mod_spec(4), mod_spec(5),
            pl.BlockSpec((D_MODEL, tf), lambda i, j: (0, j)),
            pl.BlockSpec((D_MODEL, tf), lambda i, j: (0, n_f + j)),
            pl.BlockSpec((None, 3, tf), lambda i, j: (layer, 0, j)),
            pl.BlockSpec((None, 3, tf), lambda i, j: (layer, 0, n_f + j)),
            pl.BlockSpec((tf, D_MODEL), lambda i, j: (j, 0)),
            vec, vec,
        ],
        out_specs=pl.BlockSpec((tm, D_MODEL), lambda i, j: (i, 0)),
        out_shape=jax.ShapeDtypeStruct((rows, D_MODEL), F32),
        scratch_shapes=[pltpu.VMEM((tm + 2 * HALO, D_MODEL), BF16)],
        compiler_params=_params(2),
        name="ffn_ln",
    )(x, x, x, mod, mod, mod, w_up, w_up, ffn_conv, ffn_conv, w_down,
      ln_w.reshape(1, D_MODEL), ln_b.reshape(1, D_MODEL))


def _rope_tables(n_tokens):
    t = np.arange(n_tokens)
    n_freq = HEAD_DIM // 4
    inv_freq = ROPE_THETA ** (-np.arange(n_freq) / n_freq)
    ang = np.concatenate([(t // GRID_W)[:, None] * inv_freq[None], (t % GRID_W)[:, None] * inv_freq[None]], -1)
    cos, sin = np.cos(ang), np.sin(ang)
    return (jnp.asarray(np.concatenate([cos, cos], -1), F32), jnp.asarray(np.concatenate([-sin, sin], -1), F32))


def _dft_tables(seq):
    idx = np.arange(seq)
    ang = ((idx[:, None] * idx[None, :]) % (2 * seq)) * (math.pi / seq)
    return (*_split(jnp.asarray(np.cos(ang), F32)), *_split(jnp.asarray(np.sin(ang), F32)))


def _filter_features(seq):
    pos = np.arange(seq)
    t = pos / max(seq - 1, 1)
    ang = (2.0 * math.pi * pos / seq)[:, None] * np.linspace(1e-4, FILTER_BANDS - 1, FILTER_BANDS)[None, :]
    z = np.concatenate([t[:, None], np.cos(ang), -np.sin(ang)], -1)
    return jnp.asarray(np.pad(z, ((0, 0), (0, FILTER_PAD - FILTER_EMB))), F32)


def _pad_to(a, shape):
    return jnp.pad(a, [(0, s - d) for d, s in zip(a.shape, shape)])


def kernel(x_prompt, x_sample, cache_k, cache_v, state_ret, c, c_ctx, w_ada, b_ada, w_in, q_norm, k_norm, ret_decay, hy_conv, hf_w1, hf_b1, hf_freq, hf_w2, hf_b2, hf_w3, hy_bias, w_out, ln1_w, ln1_b, w_up, ffn_conv, w_down, ln2_w, ln2_b):
    n_ctx, seq_ctx, _ = x_prompt.shape
    n_lat, seq_lat, _ = x_sample.shape

    cond = _pad_to(jnp.concatenate([c_ctx[None, :], c], 0), (COND_ROWS, D_MODEL))
    mod_all = _ada(cond, w_ada, b_ada)

    w_in_b = w_in[0].astype(BF16)
    log_gamma = jax.nn.log_sigmoid(ret_decay.astype(F32))
    abs_deltas = jnp.asarray(np.abs(np.linspace(math.log(HYENA_DECAY_TARGET) / HYENA_DECAY_PCT_MIN,
                                                math.log(HYENA_DECAY_TARGET) / HYENA_DECAY_PCT_MAX,
                                                HYENA_WIDTH))[None, :], F32)

    groups = {
        "ctx": dict(n=n_ctx, seq=seq_ctx, rope=None, dft=_dft_tables(seq_ctx), feat=_filter_features(seq_ctx)),
        "lat": dict(n=n_lat, seq=seq_lat, rope=_rope_tables(seq_lat), dft=_dft_tables(seq_lat),
                    feat=_filter_features(seq_lat)),
    }
    xs = {"ctx": x_prompt.reshape(n_ctx * seq_ctx, D_MODEL), "lat": x_sample.reshape(n_lat * seq_lat, D_MODEL)}
    ks_out, vs_out, ss_out = [], [], []

    for l in range(DEPTH):
        mods = {"ctx": mod_all[l, 0:1, None, :], "lat": mod_all[l, 1:1 + n_lat, None, :]}
        pad2 = (FILTER_PAD, FILTER_PAD)
        filt_w = (_pad_to(hf_w1[l], pad2), _pad_to(hf_b1[l][None, :], (1, FILTER_PAD)),
                  _pad_to(hf_freq[l][None, :], (1, FILTER_PAD)), _pad_to(hf_w2[l], pad2),
                  _pad_to(hf_b2[l][None, :], (1, FILTER_PAD)), _pad_to(hf_w3[l], (FILTER_PAD, hf_w3.shape[-1])))
        next_w_in = [(w_in, l + 1)] if l + 1 < DEPTH else []
        proj_ctx, (w_down_b, w_out_b) = _inproj(xs["ctx"], mods["ctx"], w_in_b, [(w_down, l), (w_out, l)])
        proj_lat, (w_up_b, *w_in_next) = _inproj(xs["lat"], mods["lat"], w_in_b, [(w_up, l)] + next_w_in)
        if w_in_next:
            w_in_b = w_in_next[0]
        projs = {"ctx": proj_ctx, "lat": proj_lat}
        for name, g in groups.items():
            n, seq, x, mod, proj = g["n"], g["seq"], xs[name], mods[name], projs[name]
            is_ctx = name == "ctx"
            att_out = _attention(proj, n, seq, q_norm[l], k_norm[l], g["rope"],
                                 None if is_ctx else (cache_k, cache_v), l, emit_cache=is_ctx)
            ret, s_fin = _retention(proj, n, seq, log_gamma[l],
                                    None if is_ctx else state_ret, l)
            spectra = _hyena_filters(seq, g["feat"], *filt_w, abs_deltas, g["dft"])
            hyz = _hyena(proj, n, seq, hy_conv[l], hy_bias[l], spectra, g["dft"])
            if is_ctx:
                att, own_k, own_v = att_out
                ks_out.append(own_k)
                vs_out.append(own_v)
                ss_out.append(s_fin)
            else:
                att = att_out[0]
            x = _outproj(att, ret, hyz, x, mod, w_out_b, ln1_w[l], ln1_b[l])
            xs[name] = _ffn(x, seq, mod, w_up_b, ffn_conv, w_down_b, l, ln2_w[l], ln2_b[l])

    return (xs["ctx"].reshape(n_ctx, seq_ctx, D_MODEL), xs["lat"].reshape(n_lat, seq_lat, D_MODEL),
            jnp.stack(ks_out, axis=1), jnp.stack(vs_out, axis=1), jnp.stack(ss_out, axis=1))
```

```python
import functools
import math

import numpy as np
import jax
import jax.numpy as jnp
from jax import lax
from jax.experimental import pallas as pl
from jax.experimental.pallas import tpu as pltpu

F32 = jnp.float32
BF16 = jnp.bfloat16

D_MODEL = 2048
DEPTH = 2
GRID_W = 64
HEAD_DIM = 128
ATT_WIDTH = D_MODEL // 2
N_ATT_HEADS = ATT_WIDTH // HEAD_DIM
N_KV_HEADS = N_ATT_HEADS // 4
GQA_GROUP = N_ATT_HEADS // N_KV_HEADS
KV_WIDTH = N_KV_HEADS * HEAD_DIM
RET_WIDTH = D_MODEL // 4
N_RET_HEADS = RET_WIDTH // HEAD_DIM
HYENA_WIDTH = D_MODEL // 4
HYENA_ORDER = 2
IN_WIDTH = ATT_WIDTH + 2 * KV_WIDTH + 4 * RET_WIDTH + (HYENA_ORDER + 1) * HYENA_WIDTH
D_FF = 5632
ROPE_THETA = 10000.0
FILTER_BANDS = 16
FILTER_EMB = 1 + 2 * FILTER_BANDS
FILTER_HID = 64
HYENA_DECAY_TARGET = 1e-2
HYENA_DECAY_PCT_MIN = 0.3
HYENA_DECAY_PCT_MAX = 1.5
DEEPNORM_ALPHA = (2 * DEPTH) ** 0.25
EPS = 1e-6

COL_Q = 0
COL_K = ATT_WIDTH
COL_V = COL_K + KV_WIDTH
COL_RQ = COL_V + KV_WIDTH
COL_RK = COL_RQ + RET_WIDTH
COL_RV = COL_RK + RET_WIDTH
COL_RG = COL_RV + RET_WIDTH
COL_HY = COL_RG + RET_WIDTH

LANES = 128
HALO = 16
VMEM_LIMIT = 60 * 1024 * 1024
COND_ROWS = 16

ADA_TN = 1024
INPROJ_TM, INPROJ_TN = 256, 512
OUTPROJ_TM, OUTPROJ_SUB = 512, 256
FFN_TM, FFN_TF = 1024, 512
ATT_TQ, ATT_CHAIN = 1024, 256
ATT_ROWS = 1024
ATT_MXU_SUM_MIN_KEYS = 1024
RET_ROWS = 2048
HYENA_ROWS = 1024
FILTER_PAD = 128

NT_DIMS = (((1,), (1,)), ((), ()))
TN_DIMS = (((0,), (0,)), ((), ()))


def _params(n_grid):
    return pltpu.CompilerParams(dimension_semantics=("arbitrary",) * n_grid,
                                vmem_limit_bytes=VMEM_LIMIT)


def _dot(a, b):
    return jnp.dot(a, b, preferred_element_type=F32)


def _split(a):
    hi = a.astype(BF16)
    lo = (a - hi.astype(F32)).astype(BF16)
    return hi, lo


def _dot3(a_hi, a_lo, b_hi, b_lo):
    return _dot(a_hi, b_hi) + _dot(a_lo, b_hi) + _dot(a_hi, b_lo)


def _silu(x):
    return x * jax.nn.sigmoid(x)


def _layer_norm(y, w, b):
    yc = y - jnp.mean(y, -1, keepdims=True)
    var = jnp.mean(yc * yc, -1, keepdims=True)
    return yc * lax.rsqrt(var + EPS) * w + b


def _shift_rows(x, down):
    n = x.shape[0]
    return pltpu.roll(x, 1 if down else n - 1, axis=0)


def _ada_kernel(cond_ref, w_ref, b_ref, o_ref):
    a = _silu(cond_ref[...]).astype(BF16)
    o_ref[...] = _dot(a, w_ref[...].astype(BF16)) + b_ref[...]


def _ada(cond, w_ada, b_ada):
    n_out = w_ada.shape[-1]
    return pl.pallas_call(
        _ada_kernel,
        grid=(DEPTH, n_out // ADA_TN),
        in_specs=[
            pl.BlockSpec((COND_ROWS, D_MODEL), lambda l, j: (0, 0)),
            pl.BlockSpec((None, D_MODEL, ADA_TN), lambda l, j: (l, 0, j)),
            pl.BlockSpec((None, 1, ADA_TN), lambda l, j: (l, 0, j)),
        ],
        out_specs=pl.BlockSpec((None, COND_ROWS, ADA_TN), lambda l, j: (l, 0, j)),
        out_shape=jax.ShapeDtypeStruct((DEPTH, COND_ROWS, n_out), F32),
        compiler_params=_params(2),
        name="ada",
    )(cond, w_ada, b_ada.reshape(DEPTH, 1, n_out))


def _inproj_kernel(x_ref, sh_ref, sc_ref, w_ref, *refs):
    n_casts = len(refs) // 2
    src_refs, o_ref, cast_refs = refs[:n_casts], refs[n_casts], refs[n_casts + 1:]
    h = (x_ref[...] * (1.0 + sc_ref[...]) + sh_ref[...]).astype(BF16)
    for n0 in range(0, IN_WIDTH, INPROJ_TN):
        o_ref[:, n0:n0 + INPROJ_TN] = _dot(h, w_ref[:, n0:n0 + INPROJ_TN])
    for src_ref, cast_ref in zip(src_refs, cast_refs):
        cast_ref[...] = src_ref[...].astype(BF16)


def _inproj(x, mod, w_in, casts):
    rows = x.shape[0]
    tm = INPROJ_TM
    n_steps = rows // tm
    tiles_per_mod = n_steps // mod.shape[0]
    cast_in, cast_out, cast_shape = [], [], []
    for src, layer in casts:
        _, k, n = src.shape
        band = k // n_steps
        assert band * n_steps == k and band % HALO == 0, (k, n_steps)
        cast_in.append(pl.BlockSpec((None, band, n), lambda i, layer=layer: (layer, i, 0)))
        cast_out.append(pl.BlockSpec((band, n), lambda i: (i, 0)))
        cast_shape.append(jax.ShapeDtypeStruct((k, n), BF16))
    proj, *cast = pl.pallas_call(
        _inproj_kernel,
        grid=(n_steps,),
        in_specs=[
            pl.BlockSpec((tm, D_MODEL), lambda i: (i, 0)),
            pl.BlockSpec((None, 1, D_MODEL), lambda i: (i // tiles_per_mod, 0, 0)),
            pl.BlockSpec((None, 1, D_MODEL), lambda i: (i // tiles_per_mod, 0, 1)),
            pl.BlockSpec((D_MODEL, IN_WIDTH), lambda i: (0, 0), pipeline_mode=pl.Buffered(1)),
            *cast_in,
        ],
        out_specs=[pl.BlockSpec((tm, IN_WIDTH), lambda i: (i, 0)), *cast_out],
        out_shape=[jax.ShapeDtypeStruct((rows, IN_WIDTH), F32), *cast_shape],
        compiler_params=_params(1),
        name="inproj",
    )(x, mod, mod, w_in, *[src for src, _ in casts])
    return proj, cast


def _rms_norm(x, w):
    return x * lax.rsqrt(jnp.mean(x * x, -1, keepdims=True) + EPS) * w


def _rope(x, cos_full, sin_signed):
    return x * cos_full + pltpu.roll(x, HEAD_DIM // 2, axis=1) * sin_signed


def _attn_kernel(*refs, seq, n_seq, rope, has_ctx, emit_cache, n_prev):
    refs = list(refs)
    q_ref, k_ref, v_ref, qw_ref, kw_ref = refs[:5]
    del refs[:5]
    if rope:
        qcos_ref, qsin_ref, kcos_ref, ksin_ref = refs[:4]
        del refs[:4]
    if has_ctx:
        ck_ref, cv_ref = refs[:2]
        del refs[:2]
    if n_prev:
        pk_ref, pv_ref = refs[:2]
        del refs[:2]
    o_ref = refs.pop(0)
    if emit_cache:
        ownk_ref, ownv_ref = refs[:2]
        del refs[:2]
    k_scr, v_scr = refs
    sums_on_mxu = v_scr.shape[1] > HEAD_DIM
    own = n_seq * seq

    @pl.when(pl.program_id(2) == 0)
    def _():
        kn = _rms_norm(k_ref[...], kw_ref[...])
        if emit_cache:
            for s in range(n_seq):
                for p in range(n_prev):
                    ownk_ref[s, p] = pk_ref[s, p]
                    ownv_ref[s, p] = pv_ref[s, p]
                ownk_ref[s, n_prev] = kn[s * seq:(s + 1) * seq]
                ownv_ref[s, n_prev] = v_ref[s * seq:(s + 1) * seq, :]
        if rope:
            kn = _rope(kn, kcos_ref[...], ksin_ref[...])
        k_scr[0:own, :] = kn.astype(BF16)
        v_scr[0:own, 0:HEAD_DIM] = v_ref[...].astype(BF16)
        if has_ctx:
            k_scr[own:, :] = ck_ref[...].astype(BF16)
            v_scr[own:, 0:HEAD_DIM] = cv_ref[...].astype(BF16)
        if sums_on_mxu:
            v_scr[:, HEAD_DIM:] = jnp.ones((v_scr.shape[0], HEAD_DIM), BF16)

    tq = q_ref.shape[0] // n_seq
    for s in range(n_seq):
        keys = slice(s * seq, (s + 1) * seq) if n_seq > 1 else slice(None)
        for g in range(GQA_GROUP):
            cols = slice(g * HEAD_DIM, (g + 1) * HEAD_DIM)
            for r0 in range(s * tq, (s + 1) * tq, ATT_CHAIN):
                rows = slice(r0, r0 + ATT_CHAIN)
                qn = _rms_norm(q_ref[rows, cols], qw_ref[...])
                if rope:
                    qn = _rope(qn, qcos_ref[rows, :], qsin_ref[rows, :])
                qb = (qn * (HEAD_DIM ** -0.5)).astype(BF16)
                sc = lax.dot_general(qb, k_scr[keys, :], NT_DIMS, preferred_element_type=F32)
                p = jnp.exp(sc - jnp.max(sc, -1, keepdims=True))
                o = _dot(p.astype(BF16), v_scr[keys, :])
                den = o[:, HEAD_DIM:HEAD_DIM + 1] if sums_on_mxu else jnp.sum(p, -1, keepdims=True)
                o_ref[rows, cols] = (o[:, :HEAD_DIM] / den).astype(o_ref.dtype)


def _attention(proj, n_batch, seq, q_norm, k_norm, rope_tabs, ctx_kv, layer, emit_cache, cache_prev=None):
    rows = proj.shape[0]
    tq = min(ATT_TQ, seq)
    nq = seq // tq
    group_w = GQA_GROUP * HEAD_DIM
    rope = rope_tabs is not None
    has_ctx = ctx_kv is not None
    n_seq = max(1, ATT_ROWS // seq) if not (rope or has_ctx) else 1
    keys_per_chain = seq + (ctx_kv[0].shape[3] if has_ctx else 0)
    value_width = 2 * HEAD_DIM if keys_per_chain >= ATT_MXU_SUM_MIN_KEYS else HEAD_DIM

    in_specs = [
        pl.BlockSpec((n_seq * tq, group_w), lambda b, kv, qi: (b * nq + qi, COL_Q // group_w + kv)),
        pl.BlockSpec((n_seq * seq, HEAD_DIM), lambda b, kv, qi: (b, COL_K // HEAD_DIM + kv)),
        pl.BlockSpec((n_seq * seq, HEAD_DIM), lambda b, kv, qi: (b, COL_V // HEAD_DIM + kv)),
        pl.BlockSpec((1, HEAD_DIM), lambda b, kv, qi: (0, 0)),
        pl.BlockSpec((1, HEAD_DIM), lambda b, kv, qi: (0, 0)),
    ]
    args = [proj, proj, proj, q_norm.reshape(1, HEAD_DIM), k_norm.reshape(1, HEAD_DIM)]
    if rope:
        cos_full, sin_signed = rope_tabs
        in_specs += [
            pl.BlockSpec((tq, HEAD_DIM), lambda b, kv, qi: (qi, 0)),
            pl.BlockSpec((tq, HEAD_DIM), lambda b, kv, qi: (qi, 0)),
            pl.BlockSpec((seq, HEAD_DIM), lambda b, kv, qi: (0, 0)),
            pl.BlockSpec((seq, HEAD_DIM), lambda b, kv, qi: (0, 0)),
        ]
        args += [cos_full, sin_signed, cos_full, sin_signed]
    if has_ctx:
        ck, cv = ctx_kv
        past = ck.shape[3]
        ctx_spec = pl.BlockSpec((None, None, None, past, HEAD_DIM), lambda b, kv, qi: (b, layer, kv, 0, 0))
        in_specs += [ctx_spec, ctx_spec]
        args += [ck, cv]

    n_prev = cache_prev[0].shape[1] if cache_prev is not None else 0
    if n_prev:
        prev_spec = pl.BlockSpec((n_seq, n_prev, None, seq, HEAD_DIM), lambda b, kv, qi: (b, 0, kv, 0, 0))
        in_specs += [prev_spec, prev_spec]
        args += list(cache_prev)

    out_specs = [pl.BlockSpec((n_seq * tq, group_w), lambda b, kv, qi: (b * nq + qi, kv))]
    out_shape = [jax.ShapeDtypeStruct((rows, ATT_WIDTH), BF16)]
    if emit_cache:
        own_spec = pl.BlockSpec((n_seq, n_prev + 1, None, seq, HEAD_DIM), lambda b, kv, qi: (b, 0, kv, 0, 0))
        out_specs += [own_spec, own_spec]
        out_shape += [jax.ShapeDtypeStruct((n_batch, n_prev + 1, N_KV_HEADS, seq, HEAD_DIM), F32)] * 2

    return pl.pallas_call(
        functools.partial(_attn_kernel, seq=seq, n_seq=n_seq, rope=rope, has_ctx=has_ctx, emit_cache=emit_cache,
                          n_prev=n_prev),
        grid=(n_batch // n_seq, N_KV_HEADS, nq),
        in_specs=in_specs,
        out_specs=out_specs,
        out_shape=out_shape,
        scratch_shapes=[pltpu.VMEM((n_seq * seq + (past if has_ctx else 0), width), BF16)
                        for width in (HEAD_DIM, value_width)],
        compiler_params=_params(3),
        name="attention",
    )(*args)


def _retention_kernel(lg_ref, q_ref, k_ref, v_ref, g_ref, *refs, seq, n_seq):
    s0_ref = refs[0] if len(refs) == 4 else None
    o_ref, sf_ref, decay_scr = refs[-3:]
    h = pl.program_id(0)
    lg_f = lg_ref[h]
    lg_b = lg_ref[N_RET_HEADS + h]

    @pl.when(pl.program_id(1) == 0)
    def _():
        i = lax.broadcasted_iota(jnp.int32, (seq, seq), 0)
        j = lax.broadcasted_iota(jnp.int32, (seq, seq), 1)
        d = (i - j).astype(F32)
        decay = jnp.exp(jnp.where(d >= 0, lg_f, lg_b) * jnp.abs(d))
        decay_scr[...] = jnp.where(d == 0, 2.0, decay)

    pos = lax.broadcasted_iota(jnp.int32, (seq, 1), 0).astype(F32)
    q_decay_f, q_decay_b = jnp.exp(lg_f * (pos + 1.0)), jnp.exp(lg_b * (seq - pos))
    k_decay_f, k_decay_b = jnp.exp(lg_f * (seq - 1.0 - pos)), jnp.exp(lg_b * pos)
    ones = jnp.ones((1, HEAD_DIM), F32)
    s_decay_f, s_decay_b = jnp.exp(ones * (lg_f * seq)), jnp.exp(ones * (lg_b * seq))

    for s in range(n_seq):
        rows = slice(s * seq, (s + 1) * seq)
        q = q_ref[rows, :]
        k = k_ref[rows, :] * (HEAD_DIM ** -0.5)
        vb = v_ref[rows, :].astype(BF16)
        a = lax.dot_general(q.astype(BF16), k.astype(BF16), NT_DIMS, preferred_element_type=F32)
        o = _dot((a * decay_scr[...]).astype(BF16), vb)
        kf = (k * k_decay_f).astype(BF16)
        kb = (k * k_decay_b).astype(BF16)
        s_f = lax.dot_general(kf, vb, TN_DIMS, preferred_element_type=F32)
        s_b = lax.dot_general(kb, vb, TN_DIMS, preferred_element_type=F32)
        if s0_ref is not None:
            s0_f = s0_ref[s, 0]
            s0_b = s0_ref[s, 1]
            o = o + _dot((q * q_decay_f).astype(BF16), s0_f.astype(BF16))
            o = o + _dot((q * q_decay_b).astype(BF16), s0_b.astype(BF16))
            s_f = s_decay_f * s0_f + s_f
            s_b = s_decay_b * s0_b + s_b
        sf_ref[s, 0] = s_f
        sf_ref[s, 1] = s_b

        oc = o - jnp.mean(o, -1, keepdims=True)
        var = jnp.mean(oc * oc, -1, keepdims=True)
        o_ref[rows, :] = (_silu(g_ref[rows, :]) * (oc * lax.rsqrt(var + EPS))).astype(o_ref.dtype)


def _retention(proj, n_batch, seq, log_gamma, s0, layer):
    rows = proj.shape[0]
    n_seq = max(1, RET_ROWS // seq)
    hd = HEAD_DIM

    def col(offset):
        return pl.BlockSpec((n_seq * seq, hd), lambda h, b: (b, offset // hd + h))

    return pl.pallas_call(
        functools.partial(_retention_kernel, seq=seq, n_seq=n_seq),
        grid=(N_RET_HEADS, n_batch // n_seq),
        in_specs=[
            pl.BlockSpec(memory_space=pltpu.SMEM),
            col(COL_RQ), col(COL_RK), col(COL_RV), col(COL_RG),
            *([pl.BlockSpec((n_seq, None, 2, None, hd, hd), lambda h, b: (b, layer, 0, h, 0, 0))]
              if s0 is not None else []),
        ],
        out_specs=[
            pl.BlockSpec((n_seq * seq, hd), lambda h, b: (b, h)),
            pl.BlockSpec((n_seq, 2, None, hd, hd), lambda h, b: (b, 0, h, 0, 0)),
        ],
        out_shape=[
            jax.ShapeDtypeStruct((rows, RET_WIDTH), BF16),
            jax.ShapeDtypeStruct((n_batch, 2, N_RET_HEADS, hd, hd), F32),
        ],
        scratch_shapes=[pltpu.VMEM((seq, seq), F32)],
        compiler_params=_params(2),
        name="retention",
    )(log_gamma.reshape(2 * N_RET_HEADS), proj, proj, proj, proj, *([s0] if s0 is not None else []))


def _alt_sign(n):
    r = lax.broadcasted_iota(jnp.int32, (n, 1), 0)
    return jnp.where((r & 1) == 0, 1.0, -1.0).astype(F32)


def _filter_kernel(z_ref, w1_ref, b1_ref, fr_ref, w2_ref, b2_ref, w3f_ref, w3b_ref, dl_ref,
                   ch_ref, cl_ref, sh_ref, sl_ref, kr_ref, ki_ref, kn_ref, *, seq):
    freq = fr_ref[...]
    hdn = jnp.sin(freq * (_dot3(*_split(z_ref[...]), *_split(w1_ref[...])) + b1_ref[...]))
    hdn = jnp.sin(freq * (_dot3(*_split(hdn), *_split(w2_ref[...])) + b2_ref[...]))
    hdn_hi, hdn_lo = _split(hdn)

    pos = lax.broadcasted_iota(jnp.int32, (seq, 1), 0)
    t = pos.astype(F32) / max(seq - 1, 1)
    decay = jnp.exp(-t * dl_ref[...])
    h_fwd = _dot3(hdn_hi, hdn_lo, *_split(w3f_ref[...])) * decay
    h_bwd = _dot3(hdn_hi, hdn_lo, *_split(w3b_ref[...])) * decay

    h_bwd_shift = jnp.where(pos == 0, 0.0, _shift_rows(h_bwd, down=True))
    alt = _alt_sign(seq)
    kr = _dot3(ch_ref[...], cl_ref[...], *_split(h_fwd + h_bwd_shift)) + alt * h_bwd[seq - 1:seq, :]
    ki = _dot3(sh_ref[...], sl_ref[...], *_split(h_bwd_shift - h_fwd))
    wgt = jnp.where(pos == 0, 1.0, 2.0) * (0.5 / seq)
    kr_ref[...] = kr * wgt
    ki_ref[...] = ki * wgt
    kn_ref[...] = jnp.sum(alt * (h_fwd - h_bwd), axis=0, keepdims=True) * (0.5 / seq)


def _hyena_filters(seq, zfeat, w1, b1, freq, w2, b2, w3, abs_deltas, dft):
    w = HYENA_WIDTH
    const = lambda o: (0, 0)
    once = pl.Buffered(1)
    mat = pl.BlockSpec((seq, seq), const, pipeline_mode=once)
    vec = pl.BlockSpec((1, FILTER_PAD), const)
    return pl.pallas_call(
        functools.partial(_filter_kernel, seq=seq),
        grid=(HYENA_ORDER,),
        in_specs=[
            pl.BlockSpec((seq, FILTER_PAD), const),
            pl.BlockSpec((FILTER_PAD, FILTER_PAD), const), vec, vec,
            pl.BlockSpec((FILTER_PAD, FILTER_PAD), const), vec,
            pl.BlockSpec((FILTER_PAD, w), lambda o: (0, 2 * o)),
            pl.BlockSpec((FILTER_PAD, w), lambda o: (0, 2 * o + 1)),
            pl.BlockSpec((1, w), const),
            mat, mat, mat, mat,
        ],
        out_specs=[
            pl.BlockSpec((None, seq, w), lambda o: (o, 0, 0)),
            pl.BlockSpec((None, seq, w), lambda o: (o, 0, 0)),
            pl.BlockSpec((None, 1, w), lambda o: (o, 0, 0)),
        ],
        out_shape=[
            jax.ShapeDtypeStruct((HYENA_ORDER, seq, HYENA_WIDTH), F32),
            jax.ShapeDtypeStruct((HYENA_ORDER, seq, HYENA_WIDTH), F32),
            jax.ShapeDtypeStruct((HYENA_ORDER, 1, HYENA_WIDTH), F32),
        ],
        compiler_params=_params(1),
        name="hyena_filters",
    )(zfeat, w1, b1, freq, w2, b2, w3, w3, abs_deltas, *dft)


def _dwconv3_seq(x, w, pos, seq):
    prev = jnp.where(pos == 0, 0.0, _shift_rows(x, down=True))
    nxt = jnp.where(pos == seq - 1, 0.0, _shift_rows(x, down=False))
    return prev * w[0:1] + x * w[1:2] + nxt * w[2:3]


def _hyena_kernel(v_ref, x1_ref, x2_ref, cv_ref, c1_ref, c2_ref, bias_ref, kr_ref, ki_ref, kn_ref,
                  cos_ref, sin_ref, o_ref, *, seq, n_seq):
    pos = lax.broadcasted_iota(jnp.int32, (seq, 1), 0)
    alt = _alt_sign(seq)

    def dft(m_ref, a):
        return _dot(m_ref[...], a.astype(BF16))

    for s in range(n_seq):
        rows = slice(s * seq, (s + 1) * seq)
        z = _dwconv3_seq(v_ref[rows, :], cv_ref[...], pos, seq)
        for o, (x_ref, c_ref) in enumerate(((x1_ref, c1_ref), (x2_ref, c2_ref))):
            ur = dft(cos_ref, z)
            ui = dft(sin_ref, z)
            u_nyq = jnp.sum(alt * z, axis=0, keepdims=True)
            kr, ki = kr_ref[o], ki_ref[o]
            yr = ur * kr + ui * ki
            yi = ui * kr - ur * ki
            y = dft(cos_ref, yr) + dft(sin_ref, yi) + alt * (u_nyq * kn_ref[o])
            z = _dwconv3_seq(x_ref[rows, :], c_ref[...], pos, seq) * (y + z * bias_ref[o:o + 1, :])
        o_ref[rows, :] = z.astype(o_ref.dtype)


def _hyena(proj, n_batch, seq, hy_conv, hy_bias, spectra, dft):
    rows = proj.shape[0]
    w = HYENA_WIDTH
    kr, ki, kn = spectra
    once = pl.Buffered(1)
    n_seq = max(1, HYENA_ROWS // seq)

    def part(p):
        return pl.BlockSpec((n_seq * seq, w), lambda b: (b, COL_HY // w + p))

    def conv(p):
        return pl.BlockSpec((3, w), lambda b: (0, p), pipeline_mode=once)

    mat = pl.BlockSpec((seq, seq), lambda b: (0, 0), pipeline_mode=once)
    spec = pl.BlockSpec((HYENA_ORDER, seq, w), lambda b: (0, 0, 0), pipeline_mode=once)
    return pl.pallas_call(
        functools.partial(_hyena_kernel, seq=seq, n_seq=n_seq),
        grid=(n_batch // n_seq,),
        in_specs=[
            part(0), part(1), part(2), conv(0), conv(1), conv(2),
            pl.BlockSpec((HYENA_ORDER, w), lambda b: (0, 0), pipeline_mode=once),
            spec, spec,
            pl.BlockSpec((HYENA_ORDER, 1, w), lambda b: (0, 0, 0), pipeline_mode=once),
            mat, mat,
        ],
        out_specs=pl.BlockSpec((n_seq * seq, w), lambda b: (b, 0)),
        out_shape=jax.ShapeDtypeStruct((rows, w), BF16),
        compiler_params=_params(1),
        name="hyena",
    )(proj, proj, proj, hy_conv, hy_conv, hy_conv, hy_bias, kr, ki, kn, dft[0], dft[2])


def _outproj_kernel(att_ref, ret_ref, hy_ref, x_ref, g_ref, w_ref, lw_ref, lb_ref, o_ref):
    r0, r1 = ATT_WIDTH, ATT_WIDTH + RET_WIDTH
    for t0 in range(0, x_ref.shape[0], OUTPROJ_SUB):
        rows = slice(t0, t0 + OUTPROJ_SUB)
        mix = (_dot(att_ref[rows, :], w_ref[0:r0, :]) + _dot(ret_ref[rows, :], w_ref[r0:r1, :])
               + _dot(hy_ref[rows, :], w_ref[r1:, :]))
        y = DEEPNORM_ALPHA * x_ref[rows, :] + g_ref[...] * mix
        o_ref[rows, :] = _layer_norm(y, lw_ref[...], lb_ref[...])


def _outproj(att, ret, hyz, x, mod, w_out, ln_w, ln_b):
    rows = x.shape[0]
    tm = OUTPROJ_TM
    tiles_per_mod = rows // mod.shape[0] // tm
    row = lambda width: pl.BlockSpec((tm, width), lambda i: (i, 0))
    vec = pl.BlockSpec((1, D_MODEL), lambda i: (0, 0))
    return pl.pallas_call(
        _outproj_kernel,
        grid=(rows // tm,),
        in_specs=[
            row(ATT_WIDTH), row(RET_WIDTH), row(HYENA_WIDTH), row(D_MODEL),
            pl.BlockSpec((None, 1, D_MODEL), lambda i: (i // tiles_per_mod, 0, 2)),
            pl.BlockSpec((D_MODEL, D_MODEL), lambda i: (0, 0), pipeline_mode=pl.Buffered(1)),
            vec, vec,
        ],
        out_specs=row(D_MODEL),
        out_shape=jax.ShapeDtypeStruct((rows, D_MODEL), F32),
        compiler_params=_params(1),
        name="outproj_ln",
    )(att, ret, hyz, x, mod, w_out, ln_w.reshape(1, D_MODEL), ln_b.reshape(1, D_MODEL))


def _ffn_kernel(x_ref, xp_ref, xn_ref, sh_ref, sc_ref, g_ref, wg_ref, wu_ref, cg_ref, cu_ref, wd_ref,
                lw_ref, lb_ref, o_ref, h_scr, *, seq, tm):
    i = pl.program_id(0)
    j = pl.program_id(1)
    ends_inside = seq < tm

    @pl.when(j == 0)
    def _():
        sc = 1.0 + sc_ref[...]
        sh = sh_ref[...]
        h_prev = xp_ref[...] * sc + sh
        h_next = xn_ref[...] * sc + sh
        if not ends_inside:
            h_prev = jnp.where(((i * tm) & (seq - 1)) == 0, 0.0, h_prev)
            h_next = jnp.where((((i + 1) * tm) & (seq - 1)) == 0, 0.0, h_next)
        h_scr[0:HALO, :] = h_prev.astype(BF16)
        h_scr[HALO:HALO + tm, :] = (x_ref[...] * sc + sh).astype(BF16)
        h_scr[HALO + tm:, :] = h_next.astype(BF16)
        o_ref[...] = jnp.zeros_like(o_ref)

    pos = (i * tm + lax.broadcasted_iota(jnp.int32, (tm, 1), 0)) & (seq - 1)

    def up_conv(w_ref, c_ref):
        up = _dot(h_scr[...], w_ref[...])
        prev = _shift_rows(up, down=True)[HALO:HALO + tm]
        nxt = _shift_rows(up, down=False)[HALO:HALO + tm]
        if ends_inside:
            prev = jnp.where(pos == 0, 0.0, prev)
            nxt = jnp.where(pos == seq - 1, 0.0, nxt)
        c = c_ref[...]
        return prev * c[0:1] + up[HALO:HALO + tm] * c[1:2] + nxt * c[2:3]

    act = _silu(up_conv(wg_ref, cg_ref)) * up_conv(wu_ref, cu_ref)
    o_ref[...] += _dot(act.astype(BF16), wd_ref[...])

    @pl.when(j == pl.num_programs(1) - 1)
    def _():
        y = DEEPNORM_ALPHA * x_ref[...] + g_ref[...] * o_ref[...]
        o_ref[...] = _layer_norm(y, lw_ref[...], lb_ref[...])


def _ffn(x, seq, mod, w_up, ffn_conv, w_down, layer, ln_w, ln_b):
    rows = x.shape[0]
    tm, tf = FFN_TM, FFN_TF
    n_f = D_FF // tf
    tiles_per_mod = rows // mod.shape[0] // tm
    halo_per_tile = tm // HALO
    last_halo = rows // HALO - 1

    def mod_spec(chunk):
        return pl.BlockSpec((None, 1, D_MODEL), lambda i, j: (i // tiles_per_mod, 0, chunk))

    vec = pl.BlockSpec((1, D_MODEL), lambda i, j: (0, 0))
    return pl.pallas_call(
        functools.partial(_ffn_kernel, seq=seq, tm=tm),
        grid=(rows // tm, n_f),
        in_specs=[
            pl.BlockSpec((tm, D_MODEL), lambda i, j: (i, 0), pipeline_mode=pl.Buffered(1)),
            pl.BlockSpec((HALO, D_MODEL), lambda i, j: (jnp.maximum(i * halo_per_tile - 1, 0), 0)),
            pl.BlockSpec((HALO, D_MODEL), lambda i, j: (jnp.minimum((i + 1) * halo_per_tile, last_halo), 0)),
            mod_spec(3), mod_spec(4), mod_spec(5),
            pl.BlockSpec((D_MODEL, tf), lambda i, j: (0, j)),
            pl.BlockSpec((D_MODEL, tf), lambda i, j: (0, n_f + j)),
            pl.BlockSpec((None, 3, tf), lambda i, j: (layer, 0, j)),
            pl.BlockSpec((None, 3, tf), lambda i, j: (layer, 0, n_f + j)),
            pl.BlockSpec((tf, D_MODEL), lambda i, j: (j, 0)),
            vec, vec,
        ],
        out_specs=pl.BlockSpec((tm, D_MODEL), lambda i, j: (i, 0)),
        out_shape=jax.ShapeDtypeStruct((rows, D_MODEL), F32),
        scratch_shapes=[pltpu.VMEM((tm + 2 * HALO, D_MODEL), BF16)],
        compiler_params=_params(2),
        name="ffn_ln",
    )(x, x, x, mod, mod, mod, w_up, w_up, ffn_conv, ffn_conv, w_down,
      ln_w.reshape(1, D_MODEL), ln_b.reshape(1, D_MODEL))


def _rope_tables(n_tokens):
    t = np.arange(n_tokens)
    n_freq = HEAD_DIM // 4
    inv_freq = ROPE_THETA ** (-np.arange(n_freq) / n_freq)
    ang = np.concatenate([(t // GRID_W)[:, None] * inv_freq[None], (t % GRID_W)[:, None] * inv_freq[None]], -1)
    cos, sin = np.cos(ang), np.sin(ang)
    return (jnp.asarray(np.concatenate([cos, cos], -1), F32), jnp.asarray(np.concatenate([-sin, sin], -1), F32))


def _dft_tables(seq):
    idx = np.arange(seq)
    ang = ((idx[:, None] * idx[None, :]) % (2 * seq)) * (math.pi / seq)
    return (*_split(jnp.asarray(np.cos(ang), F32)), *_split(jnp.asarray(np.sin(ang), F32)))


def _filter_features(seq):
    pos = np.arange(seq)
    t = pos / max(seq - 1, 1)
    ang = (2.0 * math.pi * pos / seq)[:, None] * np.linspace(1e-4, FILTER_BANDS - 1, FILTER_BANDS)[None, :]
    z = np.concatenate([t[:, None], np.cos(ang), -np.sin(ang)], -1)
    return jnp.asarray(np.pad(z, ((0, 0), (0, FILTER_PAD - FILTER_EMB))), F32)


def _pad_to(a, shape):
    return jnp.pad(a, [(0, s - d) for d, s in zip(a.shape, shape)])


def kernel(x_prompt, x_sample, cache_k, cache_v, state_ret, c, c_ctx, w_ada, b_ada, w_in, q_norm, k_norm, ret_decay, hy_conv, hf_w1, hf_b1, hf_freq, hf_w2, hf_b2, hf_w3, hy_bias, w_out, ln1_w, ln1_b, w_up, ffn_conv, w_down, ln2_w, ln2_b):
    n_ctx, seq_ctx, _ = x_prompt.shape
    n_lat, seq_lat, _ = x_sample.shape

    cond = _pad_to(jnp.concatenate([c_ctx[None, :], c], 0), (COND_ROWS, D_MODEL))
    mod_all = _ada(cond, w_ada, b_ada)

    w_in_b = w_in[0].astype(BF16)
    log_gamma = jax.nn.log_sigmoid(ret_decay.astype(F32))
    abs_deltas = jnp.asarray(np.abs(np.linspace(math.log(HYENA_DECAY_TARGET) / HYENA_DECAY_PCT_MIN,
                                                math.log(HYENA_DECAY_TARGET) / HYENA_DECAY_PCT_MAX,
                                                HYENA_WIDTH))[None, :], F32)

    groups = {
        "ctx": dict(n=n_ctx, seq=seq_ctx, rope=None, dft=_dft_tables(seq_ctx), feat=_filter_features(seq_ctx)),
        "lat": dict(n=n_lat, seq=seq_lat, rope=_rope_tables(seq_lat), dft=_dft_tables(seq_lat),
                    feat=_filter_features(seq_lat)),
    }
    xs = {"ctx": x_prompt.reshape(n_ctx * seq_ctx, D_MODEL), "lat": x_sample.reshape(n_lat * seq_lat, D_MODEL)}
    kv_cache, ss_out = None, []

    for l in range(DEPTH):
        mods = {"ctx": mod_all[l, 0:1, None, :], "lat": mod_all[l, 1:1 + n_lat, None, :]}
        pad2 = (FILTER_PAD, FILTER_PAD)
        filt_w = (_pad_to(hf_w1[l], pad2), _pad_to(hf_b1[l][None, :], (1, FILTER_PAD)),
                  _pad_to(hf_freq[l][None, :], (1, FILTER_PAD)), _pad_to(hf_w2[l], pad2),
                  _pad_to(hf_b2[l][None, :], (1, FILTER_PAD)), _pad_to(hf_w3[l], (FILTER_PAD, hf_w3.shape[-1])))
        next_w_in = [(w_in, l + 1)] if l + 1 < DEPTH else []
        proj_ctx, (w_down_b, w_out_b) = _inproj(xs["ctx"], mods["ctx"], w_in_b, [(w_down, l), (w_out, l)])
        proj_lat, (w_up_b, *w_in_next) = _inproj(xs["lat"], mods["lat"], w_in_b, [(w_up, l)] + next_w_in)
        if w_in_next:
            w_in_b = w_in_next[0]
        projs = {"ctx": proj_ctx, "lat": proj_lat}
        for name, g in groups.items():
            n, seq, x, mod, proj = g["n"], g["seq"], xs[name], mods[name], projs[name]
            is_ctx = name == "ctx"
            att_out = _attention(proj, n, seq, q_norm[l], k_norm[l], g["rope"],
                                 None if is_ctx else (cache_k, cache_v), l, emit_cache=is_ctx,
                                 cache_prev=kv_cache if is_ctx else None)
            ret, s_fin = _retention(proj, n, seq, log_gamma[l],
                                    None if is_ctx else state_ret, l)
            spectra = _hyena_filters(seq, g["feat"], *filt_w, abs_deltas, g["dft"])
            hyz = _hyena(proj, n, seq, hy_conv[l], hy_bias[l], spectra, g["dft"])
            if is_ctx:
                att, *kv_cache = att_out
                ss_out.append(s_fin)
            else:
                att = att_out[0]
            x = _outproj(att, ret, hyz, x, mod, w_out_b, ln1_w[l], ln1_b[l])
            xs[name] = _ffn(x, seq, mod, w_up_b, ffn_conv, w_down_b, l, ln2_w[l], ln2_b[l])

    return (xs["ctx"].reshape(n_ctx, seq_ctx, D_MODEL), xs["lat"].reshape(n_lat, seq_lat, D_MODEL),
            kv_cache[0], kv_cache[1], jnp.stack(ss_out, axis=1))
```
